```python
import math
import jax, jax.numpy as jnp
from jax import lax
import numpy as np

D_MODEL = 2048
BATCH = 4
SEQ = 4096
DEPTH = 4

N_EVEN = (DEPTH + 1) // 2
N_ODD = DEPTH // 2
DEEPNORM_ALPHA = (2 * DEPTH) ** 0.25
DEEPNORM_BETA = (8 * DEPTH) ** -0.25
MLA_HEADS = 16
Q_LORA = 1536
KV_LORA = 512
NOPE_DIM = 128
ROPE_DIM = 64
V_DIM = 128
ROPE_THETA = 10000.0
Q_BLOCK = 128
MLA_IN = Q_LORA + KV_LORA + ROPE_DIM
MLA_OUT = MLA_HEADS * V_DIM
RWKV_HEADS = 32
RWKV_HEAD_DIM = 64
RWKV_WIDTH = RWKV_HEADS * RWKV_HEAD_DIM
DECAY_LORA = 96
AAA_LORA = 96
GATE_LORA = 256
RWKV_IN = 3 * RWKV_WIDTH + 2 * DECAY_LORA + 2 * AAA_LORA + GATE_LORA
RWKV_DECAY_SCALE = 0.6065306597126334
RWKV_GN_EPS = 64e-5
EVEN_IN = MLA_IN + RWKV_IN
EVEN_MIX = MLA_OUT + RWKV_WIDTH
GDN_K_HEADS = 16
GDN_V_HEADS = 32
GDN_DK = 128
GDN_DV = 128
GDN_KEY_WIDTH = GDN_K_HEADS * GDN_DK
GDN_VAL_WIDTH = GDN_V_HEADS * GDN_DV
GDN_QKV = 2 * GDN_KEY_WIDTH + GDN_VAL_WIDTH
CONV_WIDTH = 5
CHUNK = 64
ODD_IN = GDN_QKV + GDN_VAL_WIDTH + 4 * GDN_V_HEADS
N_EXPERTS = 16
EXPERT_FF = 1024
CAPACITY_FACTOR = 2

kernel_name = 'hybrid_mla_rwkv7_gdn_ecmoe_encoder'


def layer_norm(x, g, b, eps=1e-5):
    xf = x.astype(jnp.float32)
    mu = jnp.mean(xf, -1, keepdims=True)
    var = jnp.mean(jnp.square(xf - mu), -1, keepdims=True)
    return ((xf - mu) * lax.rsqrt(var + eps) * g + b).astype(x.dtype)


def rms_norm(x, g, eps=1e-6):
    xf = x.astype(jnp.float32)
    return (xf * lax.rsqrt(jnp.mean(jnp.square(xf), -1, keepdims=True) + eps) * g).astype(x.dtype)


def l2_normalize(x, eps=1e-6):
    xf = x.astype(jnp.float32)
    return (xf * lax.rsqrt(jnp.sum(jnp.square(xf), -1, keepdims=True) + eps)).astype(x.dtype)


def rope_tables(positions):
    inv = ROPE_THETA ** (-jnp.arange(0, ROPE_DIM, 2, dtype=jnp.float32) / ROPE_DIM)
    ang = positions.astype(jnp.float32)[..., None] * inv
    return jnp.cos(ang), jnp.sin(ang)


def apply_rope(x, cos, sin):
    x1, x2 = jnp.split(x.astype(jnp.float32), 2, axis=-1)
    return jnp.concatenate([x1 * cos - x2 * sin, x1 * sin + x2 * cos], -1).astype(x.dtype)


def centred_shift(z):
    prev = jnp.pad(z[:, :-1], ((0, 0), (1, 0), (0, 0)))
    nxt = jnp.pad(z[:, 1:], ((0, 0), (0, 1), (0, 0)))
    return 0.5 * (prev + nxt)


def mla_mixer(p, positions, q_norm, w_uq, kv_norm, w_ukv):
    Bn, T, _ = p.shape
    c_q, c_kv, k_rope = jnp.split(p, [Q_LORA, Q_LORA + KV_LORA], axis=-1)
    q = (rms_norm(c_q, q_norm) @ w_uq).reshape(Bn, T, MLA_HEADS, NOPE_DIM + ROPE_DIM)
    kv = (rms_norm(c_kv, kv_norm) @ w_ukv).reshape(Bn, T, MLA_HEADS, NOPE_DIM + V_DIM)
    q_nope, q_rope = q[..., :NOPE_DIM], q[..., NOPE_DIM:]
    k_nope, v = kv[..., :NOPE_DIM], kv[..., NOPE_DIM:]
    cos, sin = rope_tables(positions)
    q_rope = apply_rope(q_rope, cos[:, :, None], sin[:, :, None])
    k_rope = apply_rope(k_rope, cos, sin)
    scale = (NOPE_DIM + ROPE_DIM) ** -0.5
    nb = T // Q_BLOCK

    def blocks(z):
        return jnp.moveaxis(z.reshape(Bn, nb, Q_BLOCK, *z.shape[2:]), 1, 0)

    def attend(qb):
        qn, qr = qb
        s = jnp.einsum('bqhd,bkhd->bhqk', qn, k_nope) + jnp.einsum('bqhd,bkd->bhqk', qr, k_rope)
        pr = jax.nn.softmax(s.astype(jnp.float32) * scale, axis=-1).astype(v.dtype)
        return jnp.einsum('bhqk,bkhd->bqhd', pr, v)

    o = lax.map(attend, (blocks(q_nope), blocks(q_rope)))
    return jnp.moveaxis(o, 0, 1).reshape(Bn, T, MLA_OUT)


def rwkv7_scan(r, w, k, v, a, b, reverse):
    Bn, T, H, N = r.shape

    def step(S, inp):
        r_t, w_t, k_t, v_t, a_t, b_t = inp
        sa = jnp.einsum('bhij,bhj->bhi', S, a_t)
        S = S * w_t[:, :, None, :] + sa[..., None] * b_t[:, :, None, :] + v_t[..., None] * k_t[:, :, None, :]
        return S, jnp.einsum('bhij,bhj->bhi', S, r_t)

    xs = tuple(jnp.moveaxis(z, 1, 0) for z in (r, w, k, v, a, b))
    _, y = lax.scan(step, jnp.zeros((Bn, H, N, N), jnp.float32), xs, reverse=reverse)
    return jnp.moveaxis(y, 0, 1)


def rwkv7_mixer(z, w0, w2, a0, a2, g2, k_k, k_a, r_k, gn_g, gn_b):
    Bn, T, _ = z.shape
    H, N = RWKV_HEADS, RWKV_HEAD_DIM
    f32 = jnp.float32
    r, k, v, wd, ad, gd = jnp.split(z, [RWKV_WIDTH, 2 * RWKV_WIDTH, 3 * RWKV_WIDTH,
                                        3 * RWKV_WIDTH + 2 * DECAY_LORA,
                                        3 * RWKV_WIDTH + 2 * DECAY_LORA + 2 * AAA_LORA], axis=-1)
    wd = wd.reshape(Bn, T, 2, DECAY_LORA)
    ad = ad.reshape(Bn, T, 2, AAA_LORA)
    log_w = -RWKV_DECAY_SCALE * jax.nn.sigmoid((w0 + jnp.einsum('btzl,zlc->btzc', jnp.tanh(wd), w2)).astype(f32))
    decay = jnp.exp(log_w).reshape(Bn, T, 2, H, N)
    a = jax.nn.sigmoid((a0 + jnp.einsum('btzl,zlc->btzc', ad, a2)).astype(f32)).reshape(Bn, T, 2, H, N)
    g = jax.nn.sigmoid(gd) @ g2
    rh = r.reshape(Bn, T, H, N).astype(f32)
    kh = k.reshape(Bn, T, H, N).astype(f32)
    vh = v.reshape(Bn, T, H, N).astype(f32)
    kk = l2_normalize(kh * k_k.reshape(H, N))
    k_dir = kh[:, :, None] * (1.0 + (a - 1.0) * k_a.reshape(H, N))
    b_dir = kk[:, :, None] * a
    y = (rwkv7_scan(rh, decay[:, :, 0], k_dir[:, :, 0], vh, -kk, b_dir[:, :, 0], False)
         + rwkv7_scan(rh, decay[:, :, 1], k_dir[:, :, 1], vh, -kk, b_dir[:, :, 1], True))
    mu = jnp.mean(y, -1, keepdims=True)
    var = jnp.mean(jnp.square(y - mu), -1, keepdims=True)
    y = (y - mu) * lax.rsqrt(var + RWKV_GN_EPS) * gn_g.reshape(H, N) + gn_b.reshape(H, N)
    bonus = jnp.sum(rh[:, :, None] * k_dir * r_k, axis=(2, 4))[..., None] * vh
    return ((y + bonus).reshape(Bn, T, RWKV_WIDTH) * g).astype(z.dtype)


def even_mixer(h, positions, w_in, shift_mu, q_norm, w_uq, kv_norm, w_ukv, w0, w2, a0, a2, g2,
               k_k, k_a, r_k, gn_g, gn_b, w_out):
    p = h @ w_in
    p_mla, p_rwkv = p[..., :MLA_IN], p[..., MLA_IN:]
    p_rwkv = p_rwkv + (centred_shift(p_rwkv) - p_rwkv) * shift_mu
    y = jnp.concatenate([mla_mixer(p_mla, positions, q_norm, w_uq, kv_norm, w_ukv),
                         rwkv7_mixer(p_rwkv, w0, w2, a0, a2, g2, k_k, k_a, r_k, gn_g, gn_b)], axis=-1)
    return y @ w_out


def depthwise_conv(x, w):
    return lax.conv_general_dilated(x, w[:, None, :], window_strides=(1,), padding='SAME',
                                    dimension_numbers=('NWC', 'WIO', 'NWC'),
                                    feature_group_count=x.shape[-1])


def gated_delta_chunked(q, k, v, g, beta):
    in_dtype = v.dtype
    f32 = jnp.float32
    q, k, v, g, beta = (z.astype(f32) for z in (q, k, v, g, beta))
    Bn, H, T, DK = q.shape
    DV = v.shape[-1]
    n = T // CHUNK
    q = q.reshape(Bn, H, n, CHUNK, DK)
    k = k.reshape(Bn, H, n, CHUNK, DK)
    v = v.reshape(Bn, H, n, CHUNK, DV)
    g = jnp.cumsum(g.reshape(Bn, H, n, CHUNK), axis=-1)
    beta = beta.reshape(Bn, H, n, CHUNK)[..., None]
    k_beta = k * beta
    incl = jnp.tril(jnp.ones((CHUNK, CHUNK), bool))
    strict = jnp.tril(jnp.ones((CHUNK, CHUNK), bool), -1)
    decay = jnp.where(incl, jnp.exp(jnp.where(incl, g[..., :, None] - g[..., None, :], 0.0)), 0.0)
    eye = jnp.eye(CHUNK, dtype=f32)
    lower = jnp.where(strict, jnp.einsum('bhncd,bhnmd->bhncm', k_beta, k) * decay, 0.0)
    t_inv = lax.linalg.triangular_solve(eye + lower, jnp.broadcast_to(eye, lower.shape),
                                        left_side=True, lower=True, unit_diagonal=True)
    u = t_inv @ (v * beta)
    w = t_inv @ (k_beta * jnp.exp(g)[..., None])
    attn = jnp.where(incl, jnp.einsum('bhncd,bhnmd->bhncm', q, k) * decay, 0.0)
    q_g = q * jnp.exp(g)[..., None]
    k_tail = k * jnp.exp(g[..., -1:] - g)[..., None]
    g_last = jnp.exp(g[..., -1])[..., None, None]

    def step(S, inp):
        u_i, w_i, a_i, qg_i, kt_i, gl_i = inp
        v_new = u_i - w_i @ S
        o_i = qg_i @ S + a_i @ v_new
        S = S * gl_i + jnp.swapaxes(kt_i, -1, -2) @ v_new
        return S, o_i

    xs = tuple(jnp.moveaxis(z, 2, 0) for z in (u, w, attn, q_g, k_tail, g_last))
    _, o = lax.scan(step, jnp.zeros((Bn, H, DK, DV), f32), xs)
    return jnp.moveaxis(o, 0, 2).reshape(Bn, H, T, DV).astype(in_dtype)


def odd_mixer(h, w_in, conv_w, a_log, dt_bias, norm_g, w_out):
    Bn, T, _ = h.shape
    f32 = jnp.float32
    p = h @ w_in
    qkv, z, b, a = jnp.split(p, [GDN_QKV, GDN_QKV + GDN_VAL_WIDTH,
                                 GDN_QKV + GDN_VAL_WIDTH + 2 * GDN_V_HEADS], axis=-1)
    qkv = jax.nn.silu(depthwise_conv(qkv, conv_w))
    q, k, v = jnp.split(qkv, [GDN_KEY_WIDTH, 2 * GDN_KEY_WIDTH], axis=-1)
    rep = GDN_V_HEADS // GDN_K_HEADS
    q = jnp.repeat(l2_normalize(q.reshape(Bn, T, GDN_K_HEADS, GDN_DK)), rep, axis=2) * GDN_DK ** -0.5
    k = jnp.repeat(l2_normalize(k.reshape(Bn, T, GDN_K_HEADS, GDN_DK)), rep, axis=2)
    v = v.reshape(Bn, T, GDN_V_HEADS, GDN_DV)
    beta = jax.nn.sigmoid(b.astype(f32)).reshape(Bn, T, 2, GDN_V_HEADS)
    g = -jnp.exp(a_log) * jax.nn.softplus(a.astype(f32).reshape(Bn, T, 2, GDN_V_HEADS) + dt_bias)
    qh, kh, vh = (jnp.swapaxes(t, 1, 2) for t in (q, k, v))
    gh = jnp.moveaxis(g, 1, 3)
    bh = jnp.moveaxis(beta, 1, 3)

    def flip(t):
        return jnp.flip(t, axis=2)

    o_fwd = gated_delta_chunked(qh, kh, vh, gh[:, 0], bh[:, 0])
    o_bwd = flip(gated_delta_chunked(flip(qh), flip(kh), flip(vh), flip(gh[:, 1]), flip(bh[:, 1])))
    o = jnp.swapaxes(o_fwd + o_bwd, 1, 2)
    o = rms_norm(o, norm_g) * jax.nn.silu(z.reshape(Bn, T, GDN_V_HEADS, GDN_DV))
    return o.reshape(Bn, T, GDN_VAL_WIDTH) @ w_out


def expert_choice_ffn(h, w_router, w_gate, w_up, w_down):
    Bn, T, D = h.shape
    cap = CAPACITY_FACTOR * T // N_EXPERTS
    aff = jax.nn.softmax((h @ w_router).astype(jnp.float32), axis=-1)
    gate, idx = lax.top_k(jnp.swapaxes(aff, 1, 2), cap)
    xs = jax.vmap(lambda hb, ib: hb[ib])(h, idx)
    hid = jax.nn.silu(jnp.einsum('becd,edf->becf', xs, w_gate)) * jnp.einsum('becd,edf->becf', xs, w_up)
    ys = jnp.einsum('becf,efd->becd', hid, w_down) * gate[..., None].astype(h.dtype)
    return jax.vmap(lambda ib, yb: jnp.zeros((T, D), h.dtype).at[ib.reshape(-1)].add(yb.reshape(-1, D)))(idx, ys)


def setup_inputs(seed: int = 0) -> dict:
    key = jax.random.key(seed)
    ks = iter(jax.random.split(key, 48))

    def nrm(shape, scale):
        return scale * jax.random.normal(next(ks), shape, jnp.float32)

    def unif(shape, lo, hi):
        return jax.random.uniform(next(ks), shape, jnp.float32, lo, hi)

    D = D_MODEL
    x = nrm((BATCH, SEQ, D), 1.0)
    c = nrm((BATCH, D), 1.0)
    positions = jax.random.randint(next(ks), (BATCH, 1), 0, SEQ, jnp.int32) + jnp.arange(SEQ, dtype=jnp.int32)
    ada_w = nrm((DEPTH, D, 6 * D), 0.5 * D ** -0.5)
    ada_b = nrm((DEPTH, 6 * D), 0.01)
    ln_g = 1.0 + nrm((DEPTH, 2, D), 0.02)
    ln_b = nrm((DEPTH, 2, D), 0.02)
    e_w_in = nrm((N_EVEN, D, EVEN_IN), D ** -0.5)
    e_shift_mu = unif((N_EVEN, RWKV_IN), 0.0, 1.0)
    mla_q_norm = 1.0 + nrm((N_EVEN, Q_LORA), 0.02)
    mla_w_uq = nrm((N_EVEN, Q_LORA, MLA_HEADS * (NOPE_DIM + ROPE_DIM)), Q_LORA ** -0.5)
    mla_kv_norm = 1.0 + nrm((N_EVEN, KV_LORA), 0.02)
    mla_w_ukv = nrm((N_EVEN, KV_LORA, MLA_HEADS * (NOPE_DIM + V_DIM)), KV_LORA ** -0.5)
    rwkv_w0 = nrm((N_EVEN, 2, RWKV_WIDTH), 1.0)
    rwkv_w2 = nrm((N_EVEN, 2, DECAY_LORA, RWKV_WIDTH), 0.1 * DECAY_LORA ** -0.5)
    rwkv_a0 = nrm((N_EVEN, 2, RWKV_WIDTH), 0.5)
    rwkv_a2 = nrm((N_EVEN, 2, AAA_LORA, RWKV_WIDTH), 0.1 * AAA_LORA ** -0.5)
    rwkv_g2 = nrm((N_EVEN, GATE_LORA, RWKV_WIDTH), GATE_LORA ** -0.5)
    rwkv_k_k = 0.85 + nrm((N_EVEN, RWKV_WIDTH), 0.02)
    rwkv_k_a = 1.0 + nrm((N_EVEN, RWKV_WIDTH), 0.02)
    rwkv_r_k = nrm((N_EVEN, RWKV_HEADS, RWKV_HEAD_DIM), 0.1)
    rwkv_gn_g = 1.0 + nrm((N_EVEN, RWKV_WIDTH), 0.02)
    rwkv_gn_b = nrm((N_EVEN, RWKV_WIDTH), 0.02)
    e_w_out = nrm((N_EVEN, EVEN_MIX, D), DEEPNORM_BETA * EVEN_MIX ** -0.5)
    o_w_in = nrm((N_ODD, D, ODD_IN), D ** -0.5)
    gdn_conv = nrm((N_ODD, CONV_WIDTH, GDN_QKV), CONV_WIDTH ** -0.5)
    gdn_a_log = jnp.log(unif((N_ODD, 2, GDN_V_HEADS), 1.0, 16.0))
    dt = jnp.exp(unif((N_ODD, 2, GDN_V_HEADS), math.log(1e-3), math.log(0.1)))
    gdn_dt_bias = dt + jnp.log(-jnp.expm1(-dt))
    gdn_norm = 1.0 + nrm((N_ODD, GDN_DV), 0.02)
    o_w_out = nrm((N_ODD, GDN_VAL_WIDTH, D), DEEPNORM_BETA * GDN_VAL_WIDTH ** -0.5)
    moe_router = nrm((DEPTH, D, N_EXPERTS), D ** -0.5)
    moe_w_gate = nrm((DEPTH, N_EXPERTS, D, EXPERT_FF), D ** -0.5)
    moe_w_up = nrm((DEPTH, N_EXPERTS, D, EXPERT_FF), D ** -0.5)
    moe_w_down = nrm((DEPTH, N_EXPERTS, EXPERT_FF, D), DEEPNORM_BETA * EXPERT_FF ** -0.5)
    return {'x': x, 'c': c, 'positions': positions, 'ada_w': ada_w, 'ada_b': ada_b,
            'ln_g': ln_g, 'ln_b': ln_b, 'e_w_in': e_w_in, 'e_shift_mu': e_shift_mu,
            'mla_q_norm': mla_q_norm, 'mla_w_uq': mla_w_uq, 'mla_kv_norm': mla_kv_norm,
            'mla_w_ukv': mla_w_ukv, 'rwkv_w0': rwkv_w0, 'rwkv_w2': rwkv_w2, 'rwkv_a0': rwkv_a0,
            'rwkv_a2': rwkv_a2, 'rwkv_g2': rwkv_g2, 'rwkv_k_k': rwkv_k_k, 'rwkv_k_a': rwkv_k_a,
            'rwkv_r_k': rwkv_r_k, 'rwkv_gn_g': rwkv_gn_g, 'rwkv_gn_b': rwkv_gn_b, 'e_w_out': e_w_out,
            'o_w_in': o_w_in, 'gdn_conv': gdn_conv, 'gdn_a_log': gdn_a_log, 'gdn_dt_bias': gdn_dt_bias,
            'gdn_norm': gdn_norm, 'o_w_out': o_w_out, 'moe_router': moe_router,
            'moe_w_gate': moe_w_gate, 'moe_w_up': moe_w_up, 'moe_w_down': moe_w_down}


def reference(x, c, positions, ada_w, ada_b, ln_g, ln_b, e_w_in, e_shift_mu, mla_q_norm, mla_w_uq,
              mla_kv_norm, mla_w_ukv, rwkv_w0, rwkv_w2, rwkv_a0, rwkv_a2, rwkv_g2, rwkv_k_k, rwkv_k_a,
              rwkv_r_k, rwkv_gn_g, rwkv_gn_b, e_w_out, o_w_in, gdn_conv, gdn_a_log, gdn_dt_bias,
              gdn_norm, o_w_out, moe_router, moe_w_gate, moe_w_up, moe_w_down):
    cond = jax.nn.silu(c)
    for i in range(DEPTH):
        mod = (cond @ ada_w[i] + ada_b[i])[:, None, :]
        sh_m, sc_m, g_m, sh_f, sc_f, g_f = jnp.split(mod, 6, axis=-1)
        h = x * (1.0 + sc_m) + sh_m
        j = i // 2
        if i % 2 == 0:
            y = even_mixer(h, positions, e_w_in[j], e_shift_mu[j], mla_q_norm[j], mla_w_uq[j],
                           mla_kv_norm[j], mla_w_ukv[j], rwkv_w0[j], rwkv_w2[j], rwkv_a0[j], rwkv_a2[j],
                           rwkv_g2[j], rwkv_k_k[j], rwkv_k_a[j], rwkv_r_k[j], rwkv_gn_g[j], rwkv_gn_b[j],
                           e_w_out[j])
        else:
            y = odd_mixer(h, o_w_in[j], gdn_conv[j], gdn_a_log[j], gdn_dt_bias[j], gdn_norm[j], o_w_out[j])
        x = layer_norm(DEEPNORM_ALPHA * x + g_m * y, ln_g[i, 0], ln_b[i, 0])
        h = x * (1.0 + sc_f) + sh_f
        y = expert_choice_ffn(h, moe_router[i], moe_w_gate[i], moe_w_up[i], moe_w_down[i])
        x = layer_norm(DEEPNORM_ALPHA * x + g_f * y, ln_g[i, 1], ln_b[i, 1])
    return x
```

```python
import functools
import math

import jax
import jax.numpy as jnp
from jax import lax
from jax.experimental import pallas as pl
from jax.experimental.pallas import tpu as pltpu

F32 = jnp.float32
BF16 = jnp.bfloat16
I32 = jnp.int32

DEPTH = 4
DEEPNORM_ALPHA = (2 * DEPTH) ** 0.25
MLA_HEADS = 16
Q_LORA = 1536
KV_LORA = 512
NOPE_DIM = 128
ROPE_DIM = 64
V_DIM = 128
ROPE_THETA = 10000.0
MLA_IN = Q_LORA + KV_LORA + ROPE_DIM
RWKV_HEADS = 32
RWKV_N = 64
RWKV_WIDTH = RWKV_HEADS * RWKV_N
DECAY_LORA = 96
AAA_LORA = 96
GATE_LORA = 256
RWKV_DECAY_SCALE = 0.6065306597126334
RWKV_GN_EPS = 64e-5
GDN_K_HEADS = 16
GDN_V_HEADS = 32
GDN_D = 128
GDN_KEY_WIDTH = GDN_K_HEADS * GDN_D
GDN_VAL_WIDTH = GDN_V_HEADS * GDN_D
GDN_QKV = 2 * GDN_KEY_WIDTH + GDN_VAL_WIDTH
CONV_WIDTH = 5
N_EXPERTS = 16
CAPACITY_FACTOR = 2

LANES = 128
CHUNK = 64
INV_BLOCK = 16
LORA_PAD = 128
EVEN_COLS = 9216
VMEM_LIMIT = 56 * 1024 * 1024


def _params(sem, vmem=VMEM_LIMIT):
    return pltpu.CompilerParams(dimension_semantics=sem, vmem_limit_bytes=vmem)


def _dot(a, b):
    return jnp.dot(a.astype(BF16), b.astype(BF16), preferred_element_type=F32)


def _dot_nt(a, b):
    return lax.dot_general(a.astype(BF16), b.astype(BF16), (((1,), (1,)), ((), ())),
                           preferred_element_type=F32)


def _dot_tn(a, b):
    return lax.dot_general(a.astype(BF16), b.astype(BF16), (((0,), (0,)), ((), ())),
                           preferred_element_type=F32)


def _split3(x):
    hi = x.astype(BF16)
    r1 = x - hi.astype(F32)
    mid = r1.astype(BF16)
    lo = (r1 - mid.astype(F32)).astype(BF16)
    return hi, mid, lo


def _sigmoid(x):
    return 1.0 / (1.0 + jnp.exp(-x))


def _silu(x):
    return x * _sigmoid(x)


def _ada_kernel(c_ref, w_ref, b_ref, o_ref):
    c = c_ref[...]
    o_ref[0] = _dot(_silu(c), w_ref[0]) + b_ref[0]


def ada_modulation(c, ada_w, ada_b):
    L, D, N = ada_w.shape
    B = c.shape[0]
    rows = 8
    cp = jnp.zeros((rows, D), F32).at[:B].set(c)
    tn = 1024
    out = pl.pallas_call(
        _ada_kernel,
        grid=(L, N // tn),
        in_specs=[pl.BlockSpec((rows, D), lambda l, j: (0, 0)),
                  pl.BlockSpec((1, D, tn), lambda l, j: (l, 0, j)),
                  pl.BlockSpec((1, 1, tn), lambda l, j: (l, 0, j))],
        out_specs=pl.BlockSpec((1, rows, tn), lambda l, j: (l, 0, j)),
        out_shape=jax.ShapeDtypeStruct((L, rows, N), F32),
        compiler_params=_params(("parallel", "parallel")),
        name="ada_mod",
    )(cp, ada_w, ada_b.reshape(L, 1, N))
    return out[:, :B]


def _mm_mod_kernel(x_ref, sc_ref, sh_ref, w_ref, o_ref, h_scr):
    @pl.when(pl.program_id(1) == 0)
    def _():
        h_scr[...] = (x_ref[...] * (1.0 + sc_ref[0]) + sh_ref[0]).astype(BF16)

    o_ref[...] = jnp.dot(h_scr[...], w_ref[...], preferred_element_type=F32).astype(o_ref.dtype)


def mm_mod(x, sc, sh, w, T, tm, tn, out_dtype):
    M, D = x.shape
    N = w.shape[1]
    tm = min(tm, T)
    tn = min(tn, N)
    per_b = T // tm
    return pl.pallas_call(
        _mm_mod_kernel,
        grid=(M // tm, N // tn),
        in_specs=[pl.BlockSpec((tm, D), lambda i, j: (i, 0)),
                  pl.BlockSpec((1, 1, D), lambda i, j: (i // per_b, 0, 0)),
                  pl.BlockSpec((1, 1, D), lambda i, j: (i // per_b, 0, 0)),
                  pl.BlockSpec((D, tn), lambda i, j: (0, j))],
        out_specs=pl.BlockSpec((tm, tn), lambda i, j: (i, j)),
        out_shape=jax.ShapeDtypeStruct((M, N), out_dtype),
        scratch_shapes=[pltpu.VMEM((tm, D), BF16)],
        compiler_params=_params(("parallel", "arbitrary")),
        name="mm_mod",
    )(x, sc, sh, w)


def _layer_norm_rows(v, lg, lb):
    mu = jnp.mean(v, axis=-1, keepdims=True)
    d = v - mu
    var = jnp.mean(d * d, axis=-1, keepdims=True)
    return d * lax.rsqrt(var + 1e-5) * lg + lb


def _mm_ln_kernel(a0_ref, a1_ref, w_ref, x_ref, g_ref, lg_ref, lb_ref, o_ref, acc):
    k = pl.program_id(1)

    @pl.when(k == 0)
    def _():
        acc[...] = jnp.dot(a0_ref[...], w_ref[...], preferred_element_type=F32)

    @pl.when(k == 1)
    def _():
        y = acc[...] + jnp.dot(a1_ref[...], w_ref[...], preferred_element_type=F32)
        v = DEEPNORM_ALPHA * x_ref[...] + g_ref[0] * y
        o_ref[...] = _layer_norm_rows(v, lg_ref[...], lb_ref[...])


def mm_ln(a0, a0_blk, a1, a1_blk, w, x, g, lg, lb, T, tm):
    M, D = x.shape
    tm = min(tm, T)
    per_b = T // tm
    return pl.pallas_call(
        _mm_ln_kernel,
        grid=(M // tm, 2),
        in_specs=[pl.BlockSpec((tm, D), lambda i, k: (i, a0_blk)),
                  pl.BlockSpec((tm, D), lambda i, k: (i, a1_blk)),
                  pl.BlockSpec((D, D), lambda i, k: (k, 0)),
                  pl.BlockSpec((tm, D), lambda i, k: (i, 0)),
                  pl.BlockSpec((1, 1, D), lambda i, k: (i // per_b, 0, 0)),
                  pl.BlockSpec((1, D), lambda i, k: (0, 0)),
                  pl.BlockSpec((1, D), lambda i, k: (0, 0))],
        out_specs=pl.BlockSpec((tm, D), lambda i, k: (i, 0)),
        out_shape=jax.ShapeDtypeStruct((M, D), F32),
        scratch_shapes=[pltpu.VMEM((tm, D), F32)],
        compiler_params=_params(("parallel", "arbitrary")),
        name="mm_ln",
    )(a0, a1, w, x, g, lg, lb)


def _rope_kernel(pos_ref, inv_ref, cs_ref, sn_ref):
    ang = pos_ref[...].astype(F32) * inv_ref[...]
    cs_ref[...] = jnp.cos(ang)
    sn_ref[...] = jnp.sin(ang)


def rope_tables(positions):
    B, T = positions.shape
    M = B * T
    inv = ROPE_THETA ** (-jnp.arange(0, ROPE_DIM, 2, dtype=F32) / ROPE_DIM)
    inv2 = jnp.concatenate([inv, inv]).reshape(1, ROPE_DIM)
    tm = min(1024, T)
    return pl.pallas_call(
        _rope_kernel,
        grid=(M // tm,),
        in_specs=[pl.BlockSpec((tm, 1), lambda i: (i, 0)),
                  pl.BlockSpec((1, ROPE_DIM), lambda i: (0, 0))],
        out_specs=[pl.BlockSpec((tm, ROPE_DIM), lambda i: (i, 0)),
                   pl.BlockSpec((tm, ROPE_DIM), lambda i: (i, 0))],
        out_shape=[jax.ShapeDtypeStruct((M, ROPE_DIM), F32)] * 2,
        compiler_params=_params(("parallel",)),
        name="rope_tables",
    )(positions.reshape(M, 1), inv2)


def _rms_rows(x, g, eps=1e-6):
    return x * lax.rsqrt(jnp.mean(x * x, axis=-1, keepdims=True) + eps) * g


def _mla_proj_kernel(cq_ref, ckv_ref, kr_ref, cs_ref, sn_ref, qn_ref, kvn_ref, wq_ref, wkv_ref,
                     q_ref, k_ref, v_ref, cq_scr, ckv_scr, kr_scr):
    @pl.when(pl.program_id(1) == 0)
    def _():
        cq_scr[...] = _rms_rows(cq_ref[...], qn_ref[...]).astype(BF16)
        ckv_scr[...] = _rms_rows(ckv_ref[...], kvn_ref[...]).astype(BF16)
        kr = kr_ref[...]
        kr_scr[...] = kr[:, :ROPE_DIM] * cs_ref[...] + kr[:, ROPE_DIM:] * sn_ref[...]

    scale = (NOPE_DIM + ROPE_DIM) ** -0.5
    qf = jnp.dot(cq_scr[...], wq_ref[0], preferred_element_type=F32)
    q_rope = qf[:, NOPE_DIM:NOPE_DIM + ROPE_DIM] * cs_ref[...] + qf[:, NOPE_DIM + ROPE_DIM:] * sn_ref[...]
    q_ref[0, 0, :, :NOPE_DIM] = (qf[:, :NOPE_DIM] * scale).astype(BF16)
    q_ref[0, 0, :, NOPE_DIM:] = (q_rope * scale).astype(BF16)
    kvf = jnp.dot(ckv_scr[...], wkv_ref[0], preferred_element_type=F32)
    k_ref[0, 0, :, :NOPE_DIM] = kvf[:, :NOPE_DIM].astype(BF16)
    k_ref[0, 0, :, NOPE_DIM:] = kr_scr[...].astype(BF16)
    v_ref[0, 0] = kvf[:, NOPE_DIM:].astype(BF16)


def mla_proj(p, cs, sn, qn, kvn, wq, wkv, B, T):
    M = B * T
    H = MLA_HEADS
    tm = min(512, T)
    per_b = T // tm
    qk_dim = NOPE_DIM + ROPE_DIM
    out_map = lambda i, h: (i // per_b, h, i % per_b, 0)
    return pl.pallas_call(
        _mla_proj_kernel,
        grid=(M // tm, H),
        in_specs=[pl.BlockSpec((tm, Q_LORA), lambda i, h: (i, 6144 // Q_LORA)),
                  pl.BlockSpec((tm, KV_LORA), lambda i, h: (i, 7680 // KV_LORA)),
                  pl.BlockSpec((tm, 2 * ROPE_DIM), lambda i, h: (i, 8192 // (2 * ROPE_DIM))),
                  pl.BlockSpec((tm, ROPE_DIM), lambda i, h: (i, 0)),
                  pl.BlockSpec((tm, ROPE_DIM), lambda i, h: (i, 0)),
                  pl.BlockSpec((1, Q_LORA), lambda i, h: (0, 0)),
                  pl.BlockSpec((1, KV_LORA), lambda i, h: (0, 0)),
                  pl.BlockSpec((1, Q_LORA, 256), lambda i, h: (h, 0, 0)),
                  pl.BlockSpec((1, KV_LORA, 256), lambda i, h: (h, 0, 0))],
        out_specs=[pl.BlockSpec((1, 1, tm, qk_dim), out_map),
                   pl.BlockSpec((1, 1, tm, qk_dim), out_map),
                   pl.BlockSpec((1, 1, tm, V_DIM), out_map)],
        out_shape=[jax.ShapeDtypeStruct((B, H, T, qk_dim), BF16),
                   jax.ShapeDtypeStruct((B, H, T, qk_dim), BF16),
                   jax.ShapeDtypeStruct((B, H, T, V_DIM), BF16)],
        scratch_shapes=[pltpu.VMEM((tm, Q_LORA), BF16), pltpu.VMEM((tm, KV_LORA), BF16),
                        pltpu.VMEM((tm, ROPE_DIM), F32)],
        compiler_params=_params(("parallel", "arbitrary")),
        name="mla_proj",
    )(p, p, p, cs, sn, qn, kvn, wq, wkv)


def _flash_kernel(q_ref, k_ref, v_ref, o_ref, m_scr, l_scr, acc_scr):
    kv = pl.program_id(3)

    @pl.when(kv == 0)
    def _():
        m_scr[...] = jnp.full(m_scr.shape, -jnp.inf, F32)
        l_scr[...] = jnp.zeros(l_scr.shape, F32)
        acc_scr[...] = jnp.zeros(acc_scr.shape, F32)

    s = lax.dot_general(q_ref[0, 0], k_ref[0, 0], (((1,), (1,)), ((), ())), preferred_element_type=F32)
    m_prev = m_scr[...]
    m_new = jnp.maximum(m_prev, jnp.max(s, axis=1, keepdims=True))
    alpha = jnp.exp(m_prev - m_new)
    p = jnp.exp(s - m_new)
    l_scr[...] = alpha * l_scr[...] + jnp.sum(p, axis=1, keepdims=True)
    acc_scr[...] = alpha * acc_scr[...] + jnp.dot(p.astype(BF16), v_ref[0, 0], preferred_element_type=F32)
    m_scr[...] = m_new

    @pl.when(kv == pl.num_programs(3) - 1)
    def _():
        o_ref[0] = (acc_scr[...] / l_scr[...]).astype(o_ref.dtype)


def flash_attention(q, k, v):
    B, H, T, dqk = q.shape
    dv = v.shape[-1]
    tq = min(512, T)
    tk = min(512, T)
    return pl.pallas_call(
        _flash_kernel,
        grid=(B, H, T // tq, T // tk),
        in_specs=[pl.BlockSpec((1, 1, tq, dqk), lambda b, h, i, j: (b, h, i, 0)),
                  pl.BlockSpec((1, 1, tk, dqk), lambda b, h, i, j: (b, h, j, 0)),
                  pl.BlockSpec((1, 1, tk, dv), lambda b, h, i, j: (b, h, j, 0))],
        out_specs=pl.BlockSpec((1, tq, dv), lambda b, h, i, j: (b, i, h)),
        out_shape=jax.ShapeDtypeStruct((B, T, H * dv), BF16),
        scratch_shapes=[pltpu.VMEM((tq, 1), F32), pltpu.VMEM((tq, 1), F32), pltpu.VMEM((tq, dv), F32)],
        compiler_params=_params(("parallel", "parallel", "parallel", "arbitrary")),
        name="mla_flash",
    )(q, k, v)


def _shift_rows(cur, prev8, next8, d):
    tm = cur.shape[0]
    if d == 0:
        return cur
    rolled = pltpu.roll(cur, (-d) % tm, axis=0)
    row = lax.broadcasted_iota(I32, cur.shape, 0)
    if d < 0:
        for r in range(-d):
            src = prev8[8 + d + r:8 + d + r + 1, :]
            rolled = jnp.where(row == r, src, rolled)
    else:
        for r in range(d):
            src = next8[r:r + 1, :]
            rolled = jnp.where(row == tm - d + r, src, rolled)
    return rolled


def _halo_specs(tm, cw, per_b, nrow8, col_of):
    r8 = tm // 8
    cur = pl.BlockSpec((tm, cw), lambda i, j: (i, col_of(j)))
    prv = pl.BlockSpec((8, cw), lambda i, j: (jnp.maximum(i * r8 - 1, 0), col_of(j)))
    nxt = pl.BlockSpec((8, cw), lambda i, j: (jnp.minimum((i + 1) * r8, nrow8 - 1), col_of(j)))
    return cur, prv, nxt


def _halo_load(cur_ref, prev_ref, next_ref, per_b):
    ti = pl.program_id(0) % per_b
    cur = cur_ref[...].astype(F32)
    prev8 = prev_ref[...].astype(F32) * jnp.where(ti == 0, 0.0, 1.0)
    next8 = next_ref[...].astype(F32) * jnp.where(ti == per_b - 1, 0.0, 1.0)
    return cur, prev8, next8


def _shift_mix_kernel(cur_ref, prev_ref, next_ref, mu_ref, o_ref, *, per_b, lora):
    cur, prev8, next8 = _halo_load(cur_ref, prev_ref, next_ref, per_b)
    prv = _shift_rows(cur, prev8, next8, -1)
    nxt = _shift_rows(cur, prev8, next8, 1)
    z = cur + (0.5 * (prv + nxt) - cur) * mu_ref[...]
    if lora:
        col = lax.broadcasted_iota(I32, z.shape, 1)
        z = jnp.where(col < 2 * LORA_PAD, jnp.tanh(z), jnp.where(col >= 4 * LORA_PAD, _sigmoid(z), z))
    o_ref[...] = z.astype(o_ref.dtype)


def shift_mix(p, mu, T, col0, width, cw, lora):
    M = p.shape[0]
    tm = min(512, T)
    per_b = T // tm
    c0 = col0 // cw
    cur, prv, nxt = _halo_specs(tm, cw, per_b, M // 8, lambda j: c0 + j)
    return pl.pallas_call(
        functools.partial(_shift_mix_kernel, per_b=per_b, lora=lora),
        grid=(M // tm, width // cw),
        in_specs=[cur, prv, nxt, pl.BlockSpec((1, cw), lambda i, j: (0, j))],
        out_specs=pl.BlockSpec((tm, cw), lambda i, j: (i, j)),
        out_shape=jax.ShapeDtypeStruct((M, width), BF16),
        compiler_params=_params(("parallel", "parallel")),
        name="rwkv_shift_mix",
    )(p, p, p, mu)


def _tri_masks(n, rev):
    i = lax.broadcasted_iota(I32, (n, n), 0)
    j = lax.broadcasted_iota(I32, (n, n), 1)
    if rev:
        return i < j, i <= j, i, j
    return i > j, i >= j, i, j


def _unit_tri_inverse(nmat, eye, same_blk):
    n = nmat.shape[0]
    nd = jnp.where(same_blk, nmat, 0.0)
    noff = nmat - nd
    td = eye + nd
    pw = nd
    steps = int(math.log2(INV_BLOCK)) - 1
    for _ in range(steps):
        pw = _dot(pw, pw)
        td = td + _dot(td, pw)
    y = _dot(td, noff)
    z = eye + y
    pw = y
    steps = int(math.log2(n // INV_BLOCK)) - 1
    for _ in range(steps):
        pw = _dot(pw, pw)
        z = z + _dot(z, pw)
    return _dot(z, td)


def _row_to_col(row, eye_b):
    return jnp.sum(jnp.where(eye_b, row, 0.0), axis=1, keepdims=True)


def _rwkv_chunk(c, d, rev, S, r_ref, k_ref, v_ref, la_ref, w2_ref, a2_ref, w0_ref, a0_ref,
                kk_ref, ka_ref):
    C = CHUNK
    sl = pl.ds(pl.multiple_of(c * C, C), C)
    r = r_ref[0, sl, :].astype(F32)
    k = k_ref[0, sl, :].astype(F32)
    v = v_ref[0, sl, :].astype(F32)
    wd = la_ref[0, sl, d * LORA_PAD:(d + 1) * LORA_PAD]
    ad = la_ref[0, sl, (2 + d) * LORA_PAD:(3 + d) * LORA_PAD]
    lw = -RWKV_DECAY_SCALE * _sigmoid(w0_ref[d:d + 1, :] + jnp.dot(wd, w2_ref[d], preferred_element_type=F32))
    a_sig = _sigmoid(a0_ref[d:d + 1, :] + jnp.dot(ad, a2_ref[d], preferred_element_type=F32))

    lane = lax.broadcasted_iota(I32, (C, LANES), 1)
    head0 = lane < RWKV_N
    kx = k * kk_ref[...]
    sq = kx * kx
    ss = jnp.where(head0, jnp.sum(jnp.where(head0, sq, 0.0), axis=1, keepdims=True),
                   jnp.sum(jnp.where(head0, 0.0, sq), axis=1, keepdims=True))
    kk = kx * lax.rsqrt(ss + 1e-6)
    k_dir = k * (1.0 + (a_sig - 1.0) * ka_ref[...])
    b = kk * a_sig
    a = -kk

    strict, incl, ii, jj = _tri_masks(C, rev)
    eye = (ii == jj).astype(F32)
    same_blk = (ii // INV_BLOCK) == (jj // INV_BLOCK)
    tri01 = jnp.where(incl, 1.0, 0.0).astype(BF16)
    hi, mid, lo = _split3(lw)
    cum = (jnp.dot(tri01, hi, preferred_element_type=F32) + jnp.dot(tri01, mid, preferred_element_type=F32)
           + jnp.dot(tri01, lo, preferred_element_type=F32))
    end = 0 if rev else C - 1
    cum_end = cum[end:end + 1, :]
    e_pos = jnp.exp(cum)
    e_neg = jnp.exp(-cum)
    e_tail = jnp.exp(cum_end - cum)
    r_t = r * e_pos
    a_t = a * jnp.exp(cum - lw)
    k_t = k_dir * e_neg
    b_t = b * e_neg

    zero = jnp.zeros_like(a_t)
    lhs4 = jnp.concatenate([jnp.where(head0, a_t, zero), jnp.where(head0, r_t, zero),
                            jnp.where(head0, zero, a_t), jnp.where(head0, zero, r_t)], axis=0)
    gk = _dot_nt(lhs4, k_t)
    gb = _dot_nt(lhs4, b_t)
    ar = _dot_nt(jnp.concatenate([a_t, r_t], axis=0), S)
    rhs = ar[:C]
    y = ar[C:]
    akv = []
    tinv = []
    for h in range(2):
        hm = head0 if h == 0 else jnp.logical_not(head0)
        a_ak = jnp.where(strict, gk[2 * h * C:(2 * h + 1) * C], 0.0)
        a_rk = jnp.where(incl, gk[(2 * h + 1) * C:(2 * h + 2) * C], 0.0)
        a_ab = jnp.where(strict, gb[2 * h * C:(2 * h + 1) * C], 0.0)
        vm = jnp.where(hm, v, 0.0)
        both = _dot(jnp.concatenate([a_ak, a_rk], axis=0), vm)
        rhs = rhs + both[:C]
        y = y + both[C:]
        tinv.append(_unit_tri_inverse(a_ab, eye, same_blk))
    u = jnp.zeros_like(rhs)
    for h in range(2):
        hm = head0 if h == 0 else jnp.logical_not(head0)
        u = u + _dot(tinv[h], jnp.where(hm, rhs, 0.0))
    for h in range(2):
        hm = head0 if h == 0 else jnp.logical_not(head0)
        a_rb = jnp.where(incl, gb[(2 * h + 1) * C:(2 * h + 2) * C], 0.0)
        y = y + _dot(a_rb, jnp.where(hm, u, 0.0))
    upd = _dot_tn(jnp.concatenate([v, u], axis=0),
                  jnp.concatenate([k_dir * e_tail, b * e_tail], axis=0))
    li = lax.broadcasted_iota(I32, (LANES, LANES), 0)
    lj = lax.broadcasted_iota(I32, (LANES, LANES), 1)
    same_head = (li // RWKV_N) == (lj // RWKV_N)
    S_new = S * jnp.exp(cum_end) + jnp.where(same_head, upd, 0.0)
    bonus_part = r * k_dir
    return S_new, y, bonus_part


def _rwkv_scan_kernel(r_ref, k_ref, v_ref, la_ref, w2_ref, a2_ref, g2_ref, w0_ref, a0_ref,
                      kk_ref, ka_ref, rk_ref, gng_ref, gnb_ref, o_ref, yf_scr, yb_scr, bon_scr, *, nC):
    C = CHUNK

    def body(ci, carry):
        Sf, Sb = carry
        cb = nC - 1 - ci
        Sf, yf, bf = _rwkv_chunk(ci, 0, False, Sf, r_ref, k_ref, v_ref, la_ref, w2_ref, a2_ref,
                                 w0_ref, a0_ref, kk_ref, ka_ref)
        Sb, yb, bb = _rwkv_chunk(cb, 1, True, Sb, r_ref, k_ref, v_ref, la_ref, w2_ref, a2_ref,
                                 w0_ref, a0_ref, kk_ref, ka_ref)
        yf_scr[pl.ds(pl.multiple_of(ci * C, C), C), :] = yf
        yb_scr[pl.ds(pl.multiple_of(cb * C, C), C), :] = yb
        bon_scr[0, pl.ds(pl.multiple_of(ci * C, C), C), :] = bf
        bon_scr[1, pl.ds(pl.multiple_of(cb * C, C), C), :] = bb
        return Sf, Sb

    z = jnp.zeros((LANES, LANES), F32)
    lax.fori_loop(0, nC, body, (z, z))

    def fin(ci, _):
        sl = pl.ds(pl.multiple_of(ci * C, C), C)
        y = yf_scr[sl, :] + yb_scr[sl, :]
        lane = lax.broadcasted_iota(I32, (C, LANES), 1)
        head0 = lane < RWKV_N

        def seg_mean(x):
            s0 = jnp.sum(jnp.where(head0, x, 0.0), axis=1, keepdims=True)
            s1 = jnp.sum(jnp.where(head0, 0.0, x), axis=1, keepdims=True)
            return jnp.where(head0, s0, s1) * (1.0 / RWKV_N)

        mu = seg_mean(y)
        dlt = y - mu
        var = seg_mean(dlt * dlt)
        yn = dlt * lax.rsqrt(var + RWKV_GN_EPS) * gng_ref[...] + gnb_ref[...]
        bon = (bon_scr[0, sl, :] + bon_scr[1, sl, :]) * rk_ref[...]
        bsum = seg_mean(bon) * RWKV_N
        v = v_ref[0, sl, :].astype(F32)
        gate = jnp.dot(la_ref[0, sl, 4 * LORA_PAD:], g2_ref[...], preferred_element_type=F32)
        o_ref[0, sl, :] = ((yn + bsum * v) * gate).astype(o_ref.dtype)
        return 0

    lax.fori_loop(0, nC, fin, 0)


def rwkv_scan(z_rkv, la, w2, a2, g2, w0, a0, k_k, k_a, r_k, gn_g, gn_b, B, T):
    nC = T // CHUNK
    npair = RWKV_WIDTH // LANES
    nl = la.shape[-1]
    vec = lambda: pl.BlockSpec((1, LANES), lambda b, p: (0, p))
    return pl.pallas_call(
        functools.partial(_rwkv_scan_kernel, nC=nC),
        grid=(B, npair),
        in_specs=[pl.BlockSpec((1, T, LANES), lambda b, p: (b, 0, p)),
                  pl.BlockSpec((1, T, LANES), lambda b, p: (b, 0, npair + p)),
                  pl.BlockSpec((1, T, LANES), lambda b, p: (b, 0, 2 * npair + p)),
                  pl.BlockSpec((1, T, nl), lambda b, p: (b, 0, 0)),
                  pl.BlockSpec((2, LORA_PAD, LANES), lambda b, p: (0, 0, p)),
                  pl.BlockSpec((2, LORA_PAD, LANES), lambda b, p: (0, 0, p)),
                  pl.BlockSpec((GATE_LORA, LANES), lambda b, p: (0, p)),
                  pl.BlockSpec((2, LANES), lambda b, p: (0, p)),
                  pl.BlockSpec((2, LANES), lambda b, p: (0, p)),
                  vec(), vec(), vec(), vec(), vec()],
        out_specs=pl.BlockSpec((1, T, LANES), lambda b, p: (b, 0, p)),
        out_shape=jax.ShapeDtypeStruct((B, T, RWKV_WIDTH), BF16),
        scratch_shapes=[pltpu.VMEM((T, LANES), F32), pltpu.VMEM((T, LANES), F32),
                        pltpu.VMEM((2, T, LANES), F32)],
        compiler_params=_params(("parallel", "arbitrary")),
        name="rwkv_scan",
    )(z_rkv, z_rkv, z_rkv, la, w2, a2, g2, w0, a0, k_k, k_a, r_k, gn_g, gn_b)


def _gdn_conv_kernel(cur_ref, prev_ref, next_ref, w_ref, o_ref, *, per_b, q_blocks, k_blocks):
    cur, prev8, next8 = _halo_load(cur_ref, prev_ref, next_ref, per_b)
    half = CONV_WIDTH // 2
    acc = jnp.zeros_like(cur)
    for i in range(CONV_WIDTH):
        acc = acc + _shift_rows(cur, prev8, next8, i - half) * w_ref[i:i + 1, :]
    y = _silu(acc)
    j = pl.program_id(1)
    cw = y.shape[1]
    outs = []
    for s in range(cw // GDN_D):
        ys = y[:, s * GDN_D:(s + 1) * GDN_D]
        nrm = lax.rsqrt(jnp.sum(ys * ys, axis=1, keepdims=True) + 1e-6)
        outs.append(ys * nrm)
    yn = jnp.concatenate(outs, axis=1) if len(outs) > 1 else outs[0]
    res = jnp.where(j < q_blocks, yn * (GDN_D ** -0.5), jnp.where(j < k_blocks, yn, y))
    o_ref[...] = res.astype(o_ref.dtype)


def gdn_conv_op(p, conv_w, T):
    M = p.shape[0]
    tm = min(512, T)
    cw = 512
    per_b = T // tm
    cur, prv, nxt = _halo_specs(tm, cw, per_b, M // 8, lambda j: j)
    return pl.pallas_call(
        functools.partial(_gdn_conv_kernel, per_b=per_b, q_blocks=GDN_KEY_WIDTH // cw,
                          k_blocks=2 * GDN_KEY_WIDTH // cw),
        grid=(M // tm, GDN_QKV // cw),
        in_specs=[cur, prv, nxt, pl.BlockSpec((CONV_WIDTH, cw), lambda i, j: (0, j))],
        out_specs=pl.BlockSpec((tm, cw), lambda i, j: (i, j)),
        out_shape=jax.ShapeDtypeStruct((M, GDN_QKV), BF16),
        compiler_params=_params(("parallel", "parallel")),
        name="gdn_conv",
    )(p, p, p, conv_w)


def _gdn_gate_kernel(ba_ref, alog_ref, dtb_ref, o_ref):
    C = CHUNK
    x = ba_ref[...]
    tm = x.shape[0]
    nh = 2 * GDN_V_HEADS
    beta = _sigmoid(x)
    a = x + dtb_ref[...]
    sp = jnp.maximum(a, 0.0) + jnp.log1p(jnp.exp(-jnp.abs(a)))
    g = -jnp.exp(alog_ref[...]) * sp
    i = lax.broadcasted_iota(I32, (C, C), 0)
    j = lax.broadcasted_iota(I32, (C, C), 1)
    lower = jnp.where(i >= j, 1.0, 0.0).astype(BF16)
    upper = jnp.where(i <= j, 1.0, 0.0).astype(BF16)
    col = lax.broadcasted_iota(I32, (C, 2 * nh), 1)
    for c in range(tm // C):
        gc = g[c * C:(c + 1) * C]
        hi, mid, lo = _split3(gc)
        fw = (jnp.dot(lower, hi, preferred_element_type=F32) + jnp.dot(lower, mid, preferred_element_type=F32)
              + jnp.dot(lower, lo, preferred_element_type=F32))
        bw = (jnp.dot(upper, hi, preferred_element_type=F32) + jnp.dot(upper, mid, preferred_element_type=F32)
              + jnp.dot(upper, lo, preferred_element_type=F32))
        o_ref[c * C:(c + 1) * C, :] = jnp.where(col < nh, beta[c * C:(c + 1) * C],
                                                jnp.where(col < nh + GDN_V_HEADS, fw, bw))


def gdn_gates(ba, a_log, dt_bias, T):
    M = ba.shape[0]
    tm = min(512, T)
    nh = 2 * GDN_V_HEADS
    return pl.pallas_call(
        _gdn_gate_kernel,
        grid=(M // tm,),
        in_specs=[pl.BlockSpec((tm, 2 * nh), lambda i: (i, 0)),
                  pl.BlockSpec((1, 2 * nh), lambda i: (0, 0)),
                  pl.BlockSpec((1, 2 * nh), lambda i: (0, 0))],
        out_specs=pl.BlockSpec((tm, 2 * nh), lambda i: (i, 0)),
        out_shape=jax.ShapeDtypeStruct((M, 2 * nh), F32),
        compiler_params=_params(("parallel",)),
        name="gdn_gates",
    )(ba, a_log, dt_bias)


def _gdn_chunk(c, d, rev, S, h, q_ref, k_ref, v_ref, bg_ref):
    C = CHUNK
    sl = pl.ds(pl.multiple_of(c * C, C), C)
    q = q_ref[0, sl, :].astype(F32)
    k = k_ref[0, sl, :].astype(F32)
    v = v_ref[0, sl, :].astype(F32)
    beta_row = bg_ref[0, c, pl.ds(d * GDN_V_HEADS + h, 1), :]
    gc_row = bg_ref[0, c, pl.ds((2 + d) * GDN_V_HEADS + h, 1), :]
    strict, incl, ii, jj = _tri_masks(C, rev)
    eye_b = ii == jj
    eye = eye_b.astype(F32)
    same_blk = (ii // INV_BLOCK) == (jj // INV_BLOCK)
    beta_col = _row_to_col(beta_row, eye_b)
    gc_col = _row_to_col(gc_row, eye_b)
    diff = gc_col - gc_row
    decay = jnp.where(incl, jnp.exp(jnp.where(incl, diff, 0.0)), 0.0)
    k_beta = k * beta_col
    kq = _dot_nt(jnp.concatenate([k_beta, q], axis=0), k)
    lower = jnp.where(strict, kq[:C] * decay, 0.0)
    attn = jnp.where(incl, kq[C:] * decay, 0.0)
    t_inv = _unit_tri_inverse(-lower, eye, same_blk)
    eg = jnp.exp(gc_col)
    uw = _dot(t_inv, jnp.concatenate([v * beta_col, k_beta * eg], axis=1))
    u = uw[:, :GDN_D]
    w = uw[:, GDN_D:]
    end = 0 if rev else C - 1
    g_last = gc_col[end:end + 1, :]
    ws = _dot(jnp.concatenate([w, q * eg], axis=0), S)
    v_new = u - ws[:C]
    o = ws[C:] + _dot(attn, v_new)
    S_new = S * jnp.exp(g_last) + _dot_tn(k * jnp.exp(g_last - gc_col), v_new)
    return S_new, o


def _gdn_scan_kernel(q_ref, k_ref, v_ref, z_ref, bg_ref, ng_ref, o_ref, of_scr, ob_scr, *, nC):
    C = CHUNK
    h = pl.program_id(1)

    def body(ci, carry):
        Sf, Sb = carry
        cb = nC - 1 - ci
        Sf, of = _gdn_chunk(ci, 0, False, Sf, h, q_ref, k_ref, v_ref, bg_ref)
        Sb, ob = _gdn_chunk(cb, 1, True, Sb, h, q_ref, k_ref, v_ref, bg_ref)
        of_scr[pl.ds(pl.multiple_of(ci * C, C), C), :] = of
        ob_scr[pl.ds(pl.multiple_of(cb * C, C), C), :] = ob
        return Sf, Sb

    z0 = jnp.zeros((GDN_D, GDN_D), F32)
    lax.fori_loop(0, nC, body, (z0, z0))

    def fin(ci, _):
        sl = pl.ds(pl.multiple_of(ci * C, C), C)
        o = of_scr[sl, :] + ob_scr[sl, :]
        on = o * lax.rsqrt(jnp.mean(o * o, axis=-1, keepdims=True) + 1e-6) * ng_ref[...]
        o_ref[0, sl, :] = (on * _silu(z_ref[0, sl, :].astype(F32))).astype(o_ref.dtype)
        return 0

    lax.fori_loop(0, nC, fin, 0)


def gdn_scan(qkv, z_src, z_col0, bg, norm_g, B, T):
    nC = T // CHUNK
    kb = GDN_KEY_WIDTH // GDN_D
    rep = GDN_V_HEADS // GDN_K_HEADS
    return pl.pallas_call(
        functools.partial(_gdn_scan_kernel, nC=nC),
        grid=(B, GDN_V_HEADS),
        in_specs=[pl.BlockSpec((1, T, GDN_D), lambda b, h: (b, 0, h // rep)),
                  pl.BlockSpec((1, T, GDN_D), lambda b, h: (b, 0, kb + h // rep)),
                  pl.BlockSpec((1, T, GDN_D), lambda b, h: (b, 0, 2 * kb + h)),
                  pl.BlockSpec((1, T, GDN_D), lambda b, h: (b, 0, z_col0 + h)),
                  pl.BlockSpec((1, nC, 4 * GDN_V_HEADS, CHUNK), lambda b, h: (b, 0, 0, 0)),
                  pl.BlockSpec((1, GDN_D), lambda b, h: (0, 0))],
        out_specs=pl.BlockSpec((1, T, GDN_D), lambda b, h: (b, 0, h)),
        out_shape=jax.ShapeDtypeStruct((B, T, GDN_VAL_WIDTH), BF16),
        scratch_shapes=[pltpu.VMEM((T, GDN_D), F32), pltpu.VMEM((T, GDN_D), F32)],
        compiler_params=_params(("parallel", "arbitrary")),
        name="gdn_scan",
    )(qkv, qkv, qkv, z_src, bg, norm_g)


def _router_kernel(x_ref, sc_ref, sh_ref, wr_ref, h_ref, aff_ref):
    h = x_ref[...] * (1.0 + sc_ref[0]) + sh_ref[0]
    h_ref[...] = h.astype(BF16)
    hh, hm, hl = _split3(h)
    wh, wm, wl = _split3(wr_ref[...])
    nt = lambda a, b: lax.dot_general(a, b, (((1,), (1,)), ((), ())), preferred_element_type=F32)
    logits = (nt(wh, hh) + (nt(wh, hm) + nt(wm, hh)) + (nt(wh, hl) + nt(wl, hh) + nt(wm, hm)))
    m = jnp.max(logits, axis=0, keepdims=True)
    e = jnp.exp(logits - m)
    aff_ref[0] = e / jnp.sum(e, axis=0, keepdims=True)


def moe_router_op(x, sc, sh, wr_t, B, T):
    M, D = x.shape
    E = wr_t.shape[0]
    tm = min(512, T)
    per_b = T // tm
    return pl.pallas_call(
        _router_kernel,
        grid=(M // tm,),
        in_specs=[pl.BlockSpec((tm, D), lambda i: (i, 0)),
                  pl.BlockSpec((1, 1, D), lambda i: (i // per_b, 0, 0)),
                  pl.BlockSpec((1, 1, D), lambda i: (i // per_b, 0, 0)),
                  pl.BlockSpec((E, D), lambda i: (0, 0))],
        out_specs=[pl.BlockSpec((tm, D), lambda i: (i, 0)),
                   pl.BlockSpec((1, E, tm), lambda i: (i // per_b, 0, i % per_b))],
        out_shape=[jax.ShapeDtypeStruct((M, D), BF16), jax.ShapeDtypeStruct((B, E, T), F32)],
        compiler_params=_params(("parallel",)),
        name="moe_router",
    )(x, sc, sh, wr_t)


def _lane_cumsum(m01, blk):
    E, T = m01.shape
    i = lax.broadcasted_iota(I32, (blk, blk), 0)
    j = lax.broadcasted_iota(I32, (blk, blk), 1)
    upper = jnp.where(i <= j, 1.0, 0.0).astype(BF16)
    carry = jnp.zeros((E, 1), F32)
    parts = []
    for c in range(T // blk):
        seg = m01[:, c * blk:(c + 1) * blk]
        cs = jnp.dot(seg.astype(BF16), upper, preferred_element_type=F32) + carry
        parts.append(cs)
        carry = cs[:, blk - 1:blk]
    return jnp.concatenate(parts, axis=1) if len(parts) > 1 else parts[0]


def _topk_kernel(aff_ref, pos_ref, *, cap):
    a = aff_ref[0]
    E, T = a.shape
    bits = pltpu.bitcast(a, I32)

    def body(i, t):
        cand = t | lax.shift_left(jnp.int32(1), 30 - i)
        cnt = jnp.sum((bits >= cand).astype(I32), axis=1, keepdims=True)
        return jnp.where(cnt >= cap, cand, t)

    thr = lax.fori_loop(0, 31, body, jnp.zeros((E, 1), I32))
    gt = bits > thr
    eq = bits == thr
    need = cap - jnp.sum(gt.astype(I32), axis=1, keepdims=True)
    blk = min(512, T)
    eq01 = jnp.where(eq, 1.0, 0.0)
    eq_rank = _lane_cumsum(eq01, blk) - eq01
    sel = jnp.logical_or(gt, jnp.logical_and(eq, eq_rank < need.astype(F32)))
    sel01 = jnp.where(sel, 1.0, 0.0)
    slot = _lane_cumsum(sel01, blk) - sel01
    pos_ref[0] = jnp.where(sel, slot.astype(I32), -1)


def moe_topk(aff_t, cap):
    B, E, T = aff_t.shape
    return pl.pallas_call(
        functools.partial(_topk_kernel, cap=cap),
        grid=(B,),
        in_specs=[pl.BlockSpec((1, E, T), lambda b: (b, 0, 0))],
        out_specs=pl.BlockSpec((1, E, T), lambda b: (b, 0, 0)),
        out_shape=jax.ShapeDtypeStruct((B, E, T), I32),
        compiler_params=_params(("parallel",)),
        name="moe_topk",
    )(aff_t)


def _gather_kernel(pos_ref, h_ref, o_ref, acc, *, cap):
    c = pl.program_id(2)
    tc = h_ref.shape[0]
    slot = lax.broadcasted_iota(I32, (cap, tc), 0)
    onehot = jnp.where(slot == pos_ref[0, 0], 1.0, 0.0).astype(BF16)
    part = jnp.dot(onehot, h_ref[...], preferred_element_type=F32)

    @pl.when(c == 0)
    def _():
        acc[...] = part

    @pl.when(c > 0)
    def _():
        acc[...] += part

    @pl.when(c == pl.num_programs(2) - 1)
    def _():
        o_ref[0, 0] = acc[...].astype(o_ref.dtype)


def moe_gather(pos4, h, B, T, cap):
    E = pos4.shape[1]
    D = h.shape[1]
    tc = min(1024, T)
    nc = T // tc
    return pl.pallas_call(
        functools.partial(_gather_kernel, cap=cap),
        grid=(B, E, nc),
        in_specs=[pl.BlockSpec((1, 1, 1, tc), lambda b, e, c: (b, e, 0, c)),
                  pl.BlockSpec((tc, D), lambda b, e, c: (b * nc + c, 0))],
        out_specs=pl.BlockSpec((1, 1, cap, D), lambda b, e, c: (b, e, 0, 0)),
        out_shape=jax.ShapeDtypeStruct((B, E, cap, D), BF16),
        scratch_shapes=[pltpu.VMEM((cap, D), F32)],
        compiler_params=_params(("parallel", "parallel", "arbitrary")),
        name="moe_gather",
    )(pos4, h)


def _ffn_kernel(x_ref, wg_ref, wu_ref, wd_ref, o_ref):
    x = x_ref[0, 0]
    g = jnp.dot(x, wg_ref[0], preferred_element_type=F32)
    u = jnp.dot(x, wu_ref[0], preferred_element_type=F32)
    hid = (_silu(g) * u).astype(BF16)
    o_ref[0, 0] = jnp.dot(hid, wd_ref[0], preferred_element_type=F32).astype(o_ref.dtype)


def moe_ffn(xs, wg, wu, wd):
    B, E, cap, D = xs.shape
    Fh = wg.shape[-1]
    return pl.pallas_call(
        _ffn_kernel,
        grid=(E, B),
        in_specs=[pl.BlockSpec((1, 1, cap, D), lambda e, b: (b, e, 0, 0)),
                  pl.BlockSpec((1, D, Fh), lambda e, b: (e, 0, 0)),
                  pl.BlockSpec((1, D, Fh), lambda e, b: (e, 0, 0)),
                  pl.BlockSpec((1, Fh, D), lambda e, b: (e, 0, 0))],
        out_specs=pl.BlockSpec((1, 1, cap, D), lambda e, b: (b, e, 0, 0)),
        out_shape=jax.ShapeDtypeStruct((B, E, cap, D), BF16),
        compiler_params=_params(("parallel", "arbitrary")),
        name="moe_ffn",
    )(xs, wg, wu, wd)


def _combine_kernel(pos_ref, gate_ref, y_ref, x_ref, g_ref, lg_ref, lb_ref, o_ref, acc, *, cap):
    e = pl.program_id(2)
    pos = pos_ref[0].astype(F32)
    gate = gate_ref[0]
    tt, E = pos.shape
    lane = lax.broadcasted_iota(I32, (tt, E), 1)
    pcol = jnp.sum(jnp.where(lane == e, pos, 0.0), axis=1, keepdims=True)
    gcol = jnp.sum(jnp.where(lane == e, gate, 0.0), axis=1, keepdims=True)
    slot = lax.broadcasted_iota(I32, (tt, cap), 1).astype(F32)
    onehot = jnp.where(slot == pcol, 1.0, 0.0).astype(BF16)
    part = gcol * jnp.dot(onehot, y_ref[0, 0], preferred_element_type=F32)

    @pl.when(e == 0)
    def _():
        acc[...] = part

    @pl.when(e > 0)
    def _():
        acc[...] += part

    @pl.when(e == pl.num_programs(2) - 1)
    def _():
        v = DEEPNORM_ALPHA * x_ref[...] + g_ref[0] * acc[...]
        o_ref[...] = _layer_norm_rows(v, lg_ref[...], lb_ref[...])


def moe_combine(pos_tm, aff_tm, yd, x, g, lg, lb, B, T, cap):
    E = pos_tm.shape[-1]
    M, D = x.shape
    tt = min(512, T)
    per_b = T // tt
    return pl.pallas_call(
        functools.partial(_combine_kernel, cap=cap),
        grid=(B, per_b, E),
        in_specs=[pl.BlockSpec((1, tt, E), lambda b, i, e: (b, i, 0)),
                  pl.BlockSpec((1, tt, E), lambda b, i, e: (b, i, 0)),
                  pl.BlockSpec((1, 1, cap, D), lambda b, i, e: (b, e, 0, 0)),
                  pl.BlockSpec((tt, D), lambda b, i, e: (b * per_b + i, 0)),
                  pl.BlockSpec((1, 1, D), lambda b, i, e: (b, 0, 0)),
                  pl.BlockSpec((1, D), lambda b, i, e: (0, 0)),
                  pl.BlockSpec((1, D), lambda b, i, e: (0, 0))],
        out_specs=pl.BlockSpec((tt, D), lambda b, i, e: (b * per_b + i, 0)),
        out_shape=jax.ShapeDtypeStruct((M, D), F32),
        scratch_shapes=[pltpu.VMEM((tt, D), F32)],
        compiler_params=_params(("parallel", "parallel", "arbitrary")),
        name="moe_combine",
    )(pos_tm, aff_tm, yd, x, g, lg, lb)


def moe_layer(x, sc, sh, g, lg, lb, w_router, w_gate, w_up, w_down, B, T):
    cap = CAPACITY_FACTOR * T // N_EXPERTS
    h, aff_t = moe_router_op(x, sc, sh, w_router.T, B, T)
    pos = moe_topk(aff_t, cap)
    xs = moe_gather(pos[:, :, None, :], h, B, T, cap)
    yd = moe_ffn(xs, w_gate.astype(BF16), w_up.astype(BF16), w_down.astype(BF16))
    pos_tm = jnp.swapaxes(pos, 1, 2)
    aff_tm = jnp.swapaxes(aff_t, 1, 2)
    return moe_combine(pos_tm, aff_tm, yd, x, g, lg, lb, B, T, cap)


def _pad_cols(w, n):
    return jnp.pad(w, ((0, 0), (0, n - w.shape[1])))


def _rot_cols(w):
    half = w.shape[-1] // 2
    return jnp.concatenate([-w[..., half:], w[..., :half]], axis=-1)


def _even_w_in(w):
    D = w.shape[0]
    o = MLA_IN
    rkv = w[:, o:o + 3 * RWKV_WIDTH]
    cq = w[:, :Q_LORA]
    ckv = w[:, Q_LORA:Q_LORA + KV_LORA]
    kr = w[:, Q_LORA + KV_LORA:MLA_IN]
    l0 = o + 3 * RWKV_WIDTH
    lora = [_pad_cols(w[:, l0 + i * DECAY_LORA:l0 + (i + 1) * DECAY_LORA], LORA_PAD) for i in range(4)]
    gd = w[:, l0 + 4 * DECAY_LORA:]
    pad = jnp.zeros((D, LANES), w.dtype)
    out = jnp.concatenate([rkv, cq, ckv, kr, _rot_cols(kr), pad] + lora + [gd], axis=1)
    assert out.shape[1] == EVEN_COLS
    return out.astype(BF16)


def _even_mu(mu):
    rkv = mu[:3 * RWKV_WIDTH]
    l0 = 3 * RWKV_WIDTH
    lora = [jnp.pad(mu[l0 + i * DECAY_LORA:l0 + (i + 1) * DECAY_LORA], (0, LORA_PAD - DECAY_LORA)) for i in range(4)]
    gd = mu[l0 + 4 * DECAY_LORA:]
    return rkv.reshape(1, -1), jnp.concatenate(lora + [gd]).reshape(1, -1)


def _pad_rows(w, n):
    return jnp.pad(w, ((0, 0), (0, n - w.shape[1]), (0, 0)))


def even_mixer(x, sc, sh, positions_tables, w_in, shift_mu, q_norm, w_uq, kv_norm, w_ukv, w0, w2, a0, a2, g2,
               k_k, k_a, r_k, gn_g, gn_b, B, T):
    cs, sn = positions_tables
    H = MLA_HEADS
    p = mm_mod(x, sc, sh, _even_w_in(w_in), T, 512, 1024, F32)
    wq = w_uq.reshape(Q_LORA, H, NOPE_DIM + ROPE_DIM)
    wq = jnp.concatenate([wq, _rot_cols(wq[..., NOPE_DIM:])], axis=-1)
    wq = jnp.swapaxes(wq, 0, 1).astype(BF16)
    wkv = jnp.swapaxes(w_ukv.reshape(KV_LORA, H, NOPE_DIM + V_DIM), 0, 1).astype(BF16)
    q, k, v = mla_proj(p, cs, sn, q_norm.reshape(1, -1), kv_norm.reshape(1, -1), wq, wkv, B, T)
    attn = flash_attention(q, k, v)
    mu_rkv, mu_lora = _even_mu(shift_mu)
    z_rkv = shift_mix(p, mu_rkv, T, 0, 3 * RWKV_WIDTH, 1024, False)
    la = shift_mix(p, mu_lora, T, 8448, 768, 768, True)
    y_rwkv = rwkv_scan(z_rkv.reshape(B, T, -1), la.reshape(B, T, -1),
                       _pad_rows(w2, LORA_PAD).astype(BF16), _pad_rows(a2, LORA_PAD).astype(BF16),
                       g2.astype(BF16), w0, a0, k_k.reshape(1, -1), k_a.reshape(1, -1), r_k.reshape(1, -1),
                       gn_g.reshape(1, -1), gn_b.reshape(1, -1), B, T)
    return attn.reshape(B * T, -1), y_rwkv.reshape(B * T, -1)


def odd_mixer(x, sc, sh, w_in, conv_w, a_log, dt_bias, norm_g, B, T):
    M = B * T
    wb = w_in.astype(BF16)
    n_main = GDN_QKV + GDN_VAL_WIDTH
    p = mm_mod(x, sc, sh, wb[:, :n_main], T, 512, 1024, F32)
    ba = mm_mod(x, sc, sh, wb[:, n_main:], T, 512, 128, F32)
    qkv = gdn_conv_op(p, conv_w, T)
    nh = 2 * GDN_V_HEADS
    bg = gdn_gates(ba, jnp.pad(a_log.reshape(1, -1), ((0, 0), (nh, 0))),
                   jnp.pad(dt_bias.reshape(1, -1), ((0, 0), (nh, 0))), T)
    bg_t = jnp.swapaxes(bg.reshape(B, T // CHUNK, CHUNK, -1), 2, 3)
    o = gdn_scan(qkv.reshape(B, T, -1), p.reshape(B, T, -1), GDN_QKV // GDN_D, bg_t, norm_g.reshape(1, -1), B, T)
    return o.reshape(M, -1)


def kernel(x, c, positions, ada_w, ada_b, ln_g, ln_b, e_w_in, e_shift_mu, mla_q_norm, mla_w_uq, mla_kv_norm,
           mla_w_ukv, rwkv_w0, rwkv_w2, rwkv_a0, rwkv_a2, rwkv_g2, rwkv_k_k, rwkv_k_a, rwkv_r_k, rwkv_gn_g,
           rwkv_gn_b, e_w_out, o_w_in, gdn_conv, gdn_a_log, gdn_dt_bias, gdn_norm, o_w_out, moe_router,
           moe_w_gate, moe_w_up, moe_w_down):
    B, T, D = x.shape
    M = B * T
    depth = ada_w.shape[0]
    mod = ada_modulation(c, ada_w, ada_b)
    tables = rope_tables(positions)
    xf = x.reshape(M, D)
    for i in range(depth):
        m6 = mod[i].reshape(B, 6, 1, D)
        sh_m, sc_m, g_m, sh_f, sc_f, g_f = (m6[:, n] for n in range(6))
        j = i // 2
        lg0, lb0 = ln_g[i, 0].reshape(1, D), ln_b[i, 0].reshape(1, D)
        lg1, lb1 = ln_g[i, 1].reshape(1, D), ln_b[i, 1].reshape(1, D)
        if i % 2 == 0:
            a0, a1 = even_mixer(xf, sc_m, sh_m, tables, e_w_in[j], e_shift_mu[j], mla_q_norm[j], mla_w_uq[j],
                                mla_kv_norm[j], mla_w_ukv[j], rwkv_w0[j], rwkv_w2[j], rwkv_a0[j], rwkv_a2[j],
                                rwkv_g2[j], rwkv_k_k[j], rwkv_k_a[j], rwkv_r_k[j], rwkv_gn_g[j], rwkv_gn_b[j],
                                B, T)
            xf = mm_ln(a0, 0, a1, 0, e_w_out[j].astype(BF16), xf, g_m, lg0, lb0, T, 512)
        else:
            o = odd_mixer(xf, sc_m, sh_m, o_w_in[j], gdn_conv[j], gdn_a_log[j], gdn_dt_bias[j], gdn_norm[j], B, T)
            xf = mm_ln(o, 0, o, 1, o_w_out[j].astype(BF16), xf, g_m, lg0, lb0, T, 512)
        xf = moe_layer(xf, sc_f, sh_f, g_f, lg1, lb1, moe_router[i], moe_w_gate[i], moe_w_up[i], moe_w_down[i], B, T)
    return xf.reshape(B, T, D)
```

```python
import functools
import math

import jax
import jax.numpy as jnp
from jax import lax
from jax.experimental import pallas as pl
from jax.experimental.pallas import tpu as pltpu

F32 = jnp.float32
BF16 = jnp.bfloat16
I32 = jnp.int32

DEPTH = 4
DEEPNORM_ALPHA = (2 * DEPTH) ** 0.25
MLA_HEADS = 16
Q_LORA = 1536
KV_LORA = 512
NOPE_DIM = 128
ROPE_DIM = 64
V_DIM = 128
ROPE_THETA = 10000.0
MLA_IN = Q_LORA + KV_LORA + ROPE_DIM
RWKV_HEADS = 32
RWKV_N = 64
RWKV_WIDTH = RWKV_HEADS * RWKV_N
DECAY_LORA = 96
AAA_LORA = 96
GATE_LORA = 256
RWKV_DECAY_SCALE = 0.6065306597126334
RWKV_GN_EPS = 64e-5
GDN_K_HEADS = 16
GDN_V_HEADS = 32
GDN_D = 128
GDN_KEY_WIDTH = GDN_K_HEADS * GDN_D
GDN_VAL_WIDTH = GDN_V_HEADS * GDN_D
GDN_QKV = 2 * GDN_KEY_WIDTH + GDN_VAL_WIDTH
CONV_WIDTH = 5
N_EXPERTS = 16
CAPACITY_FACTOR = 2

LANES = 128
CHUNK = 64
LOG2_CHUNK = 6
PAIR = 2 * CHUNK
GROUP = 4
INV_BLOCK = 16
LOG2_INV_BLOCK = 4
LORA_PAD = 128
EVEN_COLS = 9216
VMEM_LIMIT = 56 * 1024 * 1024


def _params(sem, vmem=VMEM_LIMIT):
    return pltpu.CompilerParams(dimension_semantics=sem, vmem_limit_bytes=vmem)


def _dot(a, b):
    return jnp.dot(a.astype(BF16), b.astype(BF16), preferred_element_type=F32)


def _dot_nt(a, b):
    return lax.dot_general(a.astype(BF16), b.astype(BF16), (((1,), (1,)), ((), ())),
                           preferred_element_type=F32)


def _dot_tn(a, b):
    return lax.dot_general(a.astype(BF16), b.astype(BF16), (((0,), (0,)), ((), ())),
                           preferred_element_type=F32)


def _bmm(a, b):
    return jnp.einsum('gij,gjk->gik', a.astype(BF16), b.astype(BF16), preferred_element_type=F32)


def _bmm_nt(a, b):
    return jnp.einsum('gik,gjk->gij', a.astype(BF16), b.astype(BF16), preferred_element_type=F32)


def _split2(x):
    hi = x.astype(BF16)
    lo = (x - hi.astype(F32)).astype(BF16)
    return hi, lo


def _split3(x):
    hi = x.astype(BF16)
    r1 = x - hi.astype(F32)
    mid = r1.astype(BF16)
    lo = (r1 - mid.astype(F32)).astype(BF16)
    return hi, mid, lo


def _sigmoid(x):
    return 1.0 / (1.0 + jnp.exp(-x))


def _silu(x):
    return x * _sigmoid(x)


def _ada_kernel(c_ref, w_ref, b_ref, o_ref):
    c = c_ref[...]
    o_ref[0] = _dot(_silu(c), w_ref[0]) + b_ref[0]


def ada_modulation(c, ada_w, ada_b):
    L, D, N = ada_w.shape
    B = c.shape[0]
    rows = 8
    cp = jnp.zeros((rows, D), F32).at[:B].set(c)
    tn = 1024
    out = pl.pallas_call(
        _ada_kernel,
        grid=(L, N // tn),
        in_specs=[pl.BlockSpec((rows, D), lambda l, j: (0, 0)),
                  pl.BlockSpec((1, D, tn), lambda l, j: (l, 0, j)),
                  pl.BlockSpec((1, 1, tn), lambda l, j: (l, 0, j))],
        out_specs=pl.BlockSpec((1, rows, tn), lambda l, j: (l, 0, j)),
        out_shape=jax.ShapeDtypeStruct((L, rows, N), F32),
        compiler_params=_params(("parallel", "parallel")),
        name="ada_mod",
    )(cp, ada_w, ada_b.reshape(L, 1, N))
    return out[:, :B]


def _mm_mod_kernel(x_ref, sc_ref, sh_ref, w_ref, o_ref, h_scr):
    @pl.when(pl.program_id(1) == 0)
    def _():
        h_scr[...] = (x_ref[...] * (1.0 + sc_ref[0]) + sh_ref[0]).astype(BF16)

    o_ref[...] = jnp.dot(h_scr[...], w_ref[...], preferred_element_type=F32).astype(o_ref.dtype)


def mm_mod(x, sc, sh, w, T, tm, tn, out_dtype):
    M, D = x.shape
    N = w.shape[1]
    tm = min(tm, T)
    tn = min(tn, N)
    per_b = T // tm
    return pl.pallas_call(
        _mm_mod_kernel,
        grid=(M // tm, N // tn),
        in_specs=[pl.BlockSpec((tm, D), lambda i, j: (i, 0)),
                  pl.BlockSpec((1, 1, D), lambda i, j: (i // per_b, 0, 0)),
                  pl.BlockSpec((1, 1, D), lambda i, j: (i // per_b, 0, 0)),
                  pl.BlockSpec((D, tn), lambda i, j: (0, j))],
        out_specs=pl.BlockSpec((tm, tn), lambda i, j: (i, j)),
        out_shape=jax.ShapeDtypeStruct((M, N), out_dtype),
        scratch_shapes=[pltpu.VMEM((tm, D), BF16)],
        compiler_params=_params(("parallel", "arbitrary")),
        name="mm_mod",
    )(x, sc, sh, w)


def _layer_norm_rows(v, lg, lb):
    mu = jnp.mean(v, axis=-1, keepdims=True)
    d = v - mu
    var = jnp.mean(d * d, axis=-1, keepdims=True)
    return d * lax.rsqrt(var + 1e-5) * lg + lb


def _mm_ln_kernel(a0_ref, a1_ref, w_ref, x_ref, g_ref, lg_ref, lb_ref, o_ref, acc):
    k = pl.program_id(1)

    @pl.when(k == 0)
    def _():
        acc[...] = jnp.dot(a0_ref[...], w_ref[...], preferred_element_type=F32)

    @pl.when(k == 1)
    def _():
        y = acc[...] + jnp.dot(a1_ref[...], w_ref[...], preferred_element_type=F32)
        v = DEEPNORM_ALPHA * x_ref[...] + g_ref[0] * y
        o_ref[...] = _layer_norm_rows(v, lg_ref[...], lb_ref[...])


def mm_ln(a0, a0_blk, a1, a1_blk, w, x, g, lg, lb, T, tm):
    M, D = x.shape
    tm = min(tm, T)
    per_b = T // tm
    return pl.pallas_call(
        _mm_ln_kernel,
        grid=(M // tm, 2),
        in_specs=[pl.BlockSpec((tm, D), lambda i, k: (i, a0_blk)),
                  pl.BlockSpec((tm, D), lambda i, k: (i, a1_blk)),
                  pl.BlockSpec((D, D), lambda i, k: (k, 0)),
                  pl.BlockSpec((tm, D), lambda i, k: (i, 0)),
                  pl.BlockSpec((1, 1, D), lambda i, k: (i // per_b, 0, 0)),
                  pl.BlockSpec((1, D), lambda i, k: (0, 0)),
                  pl.BlockSpec((1, D), lambda i, k: (0, 0))],
        out_specs=pl.BlockSpec((tm, D), lambda i, k: (i, 0)),
        out_shape=jax.ShapeDtypeStruct((M, D), F32),
        scratch_shapes=[pltpu.VMEM((tm, D), F32)],
        compiler_params=_params(("parallel", "arbitrary")),
        name="mm_ln",
    )(a0, a1, w, x, g, lg, lb)


def _rope_kernel(pos_ref, inv_ref, cs_ref, sn_ref):
    ang = pos_ref[...].astype(F32) * inv_ref[...]
    cs_ref[...] = jnp.cos(ang)
    sn_ref[...] = jnp.sin(ang)


def rope_tables(positions):
    B, T = positions.shape
    M = B * T
    inv = ROPE_THETA ** (-jnp.arange(0, ROPE_DIM, 2, dtype=F32) / ROPE_DIM)
    inv2 = jnp.concatenate([inv, inv]).reshape(1, ROPE_DIM)
    tm = min(1024, T)
    return pl.pallas_call(
        _rope_kernel,
        grid=(M // tm,),
        in_specs=[pl.BlockSpec((tm, 1), lambda i: (i, 0)),
                  pl.BlockSpec((1, ROPE_DIM), lambda i: (0, 0))],
        out_specs=[pl.BlockSpec((tm, ROPE_DIM), lambda i: (i, 0)),
                   pl.BlockSpec((tm, ROPE_DIM), lambda i: (i, 0))],
        out_shape=[jax.ShapeDtypeStruct((M, ROPE_DIM), F32)] * 2,
        compiler_params=_params(("parallel",)),
        name="rope_tables",
    )(positions.reshape(M, 1), inv2)


def _rms_rows(x, g, eps=1e-6):
    return x * lax.rsqrt(jnp.mean(x * x, axis=-1, keepdims=True) + eps) * g


def _mla_proj_kernel(cq_ref, ckv_ref, kr_ref, cs_ref, sn_ref, qn_ref, kvn_ref, wq_ref, wk_ref, wvt_ref,
                     q_ref, k_ref, vt_ref, cq_scr, ckv_scr, kr_scr):
    @pl.when(pl.program_id(1) == 0)
    def _():
        cq_scr[...] = _rms_rows(cq_ref[...], qn_ref[...]).astype(BF16)
        ckv_scr[...] = _rms_rows(ckv_ref[...], kvn_ref[...]).astype(BF16)
        kr = kr_ref[...]
        kr_scr[...] = kr[:, :ROPE_DIM] * cs_ref[...] + kr[:, ROPE_DIM:] * sn_ref[...]

    scale = (NOPE_DIM + ROPE_DIM) ** -0.5
    qf = jnp.dot(cq_scr[...], wq_ref[0], preferred_element_type=F32)
    q_rope = qf[:, NOPE_DIM:NOPE_DIM + ROPE_DIM] * cs_ref[...] + qf[:, NOPE_DIM + ROPE_DIM:] * sn_ref[...]
    q_ref[0, 0, :, :NOPE_DIM] = (qf[:, :NOPE_DIM] * scale).astype(BF16)
    q_ref[0, 0, :, NOPE_DIM:] = (q_rope * scale).astype(BF16)
    ckv = ckv_scr[...]
    k_ref[0, 0, :, :NOPE_DIM] = jnp.dot(ckv, wk_ref[0], preferred_element_type=F32).astype(BF16)
    k_ref[0, 0, :, NOPE_DIM:] = kr_scr[...].astype(BF16)
    vt_ref[0, 0] = lax.dot_general(wvt_ref[0], ckv, (((1,), (1,)), ((), ())),
                                   preferred_element_type=F32).astype(BF16)


def mla_proj(p, cs, sn, qn, kvn, wq, wk, wvt, B, T):
    M = B * T
    H = MLA_HEADS
    tm = min(512, T)
    per_b = T // tm
    qk_dim = NOPE_DIM + ROPE_DIM
    out_map = lambda i, h: (i // per_b, h, i % per_b, 0)
    return pl.pallas_call(
        _mla_proj_kernel,
        grid=(M // tm, H),
        in_specs=[pl.BlockSpec((tm, Q_LORA), lambda i, h: (i, 6144 // Q_LORA)),
                  pl.BlockSpec((tm, KV_LORA), lambda i, h: (i, 7680 // KV_LORA)),
                  pl.BlockSpec((tm, 2 * ROPE_DIM), lambda i, h: (i, 8192 // (2 * ROPE_DIM))),
                  pl.BlockSpec((tm, ROPE_DIM), lambda i, h: (i, 0)),
                  pl.BlockSpec((tm, ROPE_DIM), lambda i, h: (i, 0)),
                  pl.BlockSpec((1, Q_LORA), lambda i, h: (0, 0)),
                  pl.BlockSpec((1, KV_LORA), lambda i, h: (0, 0)),
                  pl.BlockSpec((1, Q_LORA, 256), lambda i, h: (h, 0, 0)),
                  pl.BlockSpec((1, KV_LORA, NOPE_DIM), lambda i, h: (h, 0, 0)),
                  pl.BlockSpec((1, V_DIM, KV_LORA), lambda i, h: (h, 0, 0))],
        out_specs=[pl.BlockSpec((1, 1, tm, qk_dim), out_map),
                   pl.BlockSpec((1, 1, tm, qk_dim), out_map),
                   pl.BlockSpec((1, 1, V_DIM, tm), lambda i, h: (i // per_b, h, 0, i % per_b))],
        out_shape=[jax.ShapeDtypeStruct((B, H, T, qk_dim), BF16),
                   jax.ShapeDtypeStruct((B, H, T, qk_dim), BF16),
                   jax.ShapeDtypeStruct((B, H, V_DIM, T), BF16)],
        scratch_shapes=[pltpu.VMEM((tm, Q_LORA), BF16), pltpu.VMEM((tm, KV_LORA), BF16),
                        pltpu.VMEM((tm, ROPE_DIM), F32)],
        compiler_params=_params(("parallel", "arbitrary")),
        name="mla_proj",
    )(p, p, p, cs, sn, qn, kvn, wq, wk, wvt)


def _attn_kernel(q_ref, k_ref, vt_ref, o_ref, *, kc):
    q = q_ref[0, 0]
    T = k_ref.shape[2]
    nk = T // kc
    ss = [lax.dot_general(k_ref[0, 0, c * kc:(c + 1) * kc, :], q, (((1,), (1,)), ((), ())),
                          preferred_element_type=F32) for c in range(nk)]
    m = jnp.max(ss[0], axis=0, keepdims=True)
    for c in range(1, nk):
        m = jnp.maximum(m, jnp.max(ss[c], axis=0, keepdims=True))
    acc = None
    l = None
    for c in range(nk):
        p = jnp.exp(ss[c] - m)
        lc = jnp.sum(p, axis=0, keepdims=True)
        ac = jnp.dot(vt_ref[0, 0, :, c * kc:(c + 1) * kc], p.astype(BF16), preferred_element_type=F32)
        l = lc if l is None else l + lc
        acc = ac if acc is None else acc + ac
    o_ref[0] = (acc / l).T.astype(o_ref.dtype)


def mla_attention(q, k, vt):
    B, H, T, dqk = q.shape
    dv = vt.shape[2]
    tq = min(512, T)
    kc = min(1024, T)
    return pl.pallas_call(
        functools.partial(_attn_kernel, kc=kc),
        grid=(B, H, T // tq),
        in_specs=[pl.BlockSpec((1, 1, tq, dqk), lambda b, h, i: (b, h, i, 0)),
                  pl.BlockSpec((1, 1, T, dqk), lambda b, h, i: (b, h, 0, 0)),
                  pl.BlockSpec((1, 1, dv, T), lambda b, h, i: (b, h, 0, 0))],
        out_specs=pl.BlockSpec((1, tq, dv), lambda b, h, i: (b, i, h)),
        out_shape=jax.ShapeDtypeStruct((B, T, H * dv), BF16),
        compiler_params=_params(("parallel", "parallel", "parallel")),
        name="mla_attn",
    )(q, k, vt)


def _shift_rows(cur, prev8, next8, d):
    tm = cur.shape[0]
    if d == 0:
        return cur
    rolled = pltpu.roll(cur, (-d) % tm, axis=0)
    row = lax.broadcasted_iota(I32, cur.shape, 0)
    if d < 0:
        for r in range(-d):
            src = prev8[8 + d + r:8 + d + r + 1, :]
            rolled = jnp.where(row == r, src, rolled)
    else:
        for r in range(d):
            src = next8[r:r + 1, :]
            rolled = jnp.where(row == tm - d + r, src, rolled)
    return rolled


def _halo_specs(tm, cw, per_b, nrow8, col_of):
    r8 = tm // 8
    cur = pl.BlockSpec((tm, cw), lambda i, j: (i, col_of(j)))
    prv = pl.BlockSpec((8, cw), lambda i, j: (jnp.maximum(i * r8 - 1, 0), col_of(j)))
    nxt = pl.BlockSpec((8, cw), lambda i, j: (jnp.minimum((i + 1) * r8, nrow8 - 1), col_of(j)))
    return cur, prv, nxt


def _halo_load(cur_ref, prev_ref, next_ref, per_b):
    ti = pl.program_id(0) % per_b
    cur = cur_ref[...].astype(F32)
    prev8 = prev_ref[...].astype(F32) * jnp.where(ti == 0, 0.0, 1.0)
    next8 = next_ref[...].astype(F32) * jnp.where(ti == per_b - 1, 0.0, 1.0)
    return cur, prev8, next8


def _pair_masks(rev):
    i = lax.broadcasted_iota(I32, (PAIR, PAIR), 0)
    j = lax.broadcasted_iota(I32, (PAIR, PAIR), 1)
    same = lax.shift_right_logical(i, LOG2_CHUNK) == lax.shift_right_logical(j, LOG2_CHUNK)
    if rev:
        strict = jnp.logical_and(same, i < j)
        incl = jnp.logical_and(same, i <= j)
    else:
        strict = jnp.logical_and(same, i > j)
        incl = jnp.logical_and(same, i >= j)
    eye_b = i == j
    same_blk = lax.shift_right_logical(i, LOG2_INV_BLOCK) == lax.shift_right_logical(j, LOG2_INV_BLOCK)
    return strict, incl, eye_b, same_blk


def _unit_tri_inverse(nmat, eye, same_blk):
    n = nmat.shape[-1]
    nd = jnp.where(same_blk, nmat, 0.0)
    noff = nmat - nd
    x = eye + nd
    p = _bmm(nd, nd)
    for s in range(1, LOG2_INV_BLOCK):
        if s < LOG2_INV_BLOCK - 1:
            xp = _bmm(jnp.concatenate([x, p], axis=1), p)
            x = x + xp[:, :n]
            p = xp[:, n:]
        else:
            x = x + _bmm(x, p)
    y = _bmm(x, noff)
    acc = x
    for _ in range(CHUNK // INV_BLOCK - 1):
        acc = x + _bmm(y, acc)
    return acc


def _rows_to_cols(rows, eye_b):
    return jnp.sum(jnp.where(eye_b, rows, 0.0), axis=2, keepdims=True)


def _stack_heads(x, head0):
    return jnp.concatenate([jnp.where(head0, x, 0.0), jnp.where(head0, 0.0, x)], axis=1)


def _shift_mix_kernel(cur_ref, prev_ref, next_ref, mu_ref, o_ref, *, per_b, lora):
    cur, prev8, next8 = _halo_load(cur_ref, prev_ref, next_ref, per_b)
    prv = _shift_rows(cur, prev8, next8, -1)
    nxt = _shift_rows(cur, prev8, next8, 1)
    z = cur + (0.5 * (prv + nxt) - cur) * mu_ref[...]
    if lora:
        col = lax.broadcasted_iota(I32, z.shape, 1)
        z = jnp.where(col < 2 * LORA_PAD, jnp.tanh(z), jnp.where(col >= 4 * LORA_PAD, _sigmoid(z), z))
    o_ref[...] = z.astype(o_ref.dtype)


def shift_mix(p, mu, T, col0, width, cw, lora):
    M = p.shape[0]
    tm = min(512, T)
    per_b = T // tm
    c0 = col0 // cw
    cur, prv, nxt = _halo_specs(tm, cw, per_b, M // 8, lambda j: c0 + j)
    return pl.pallas_call(
        functools.partial(_shift_mix_kernel, per_b=per_b, lora=lora),
        grid=(M // tm, width // cw),
        in_specs=[cur, prv, nxt, pl.BlockSpec((1, cw), lambda i, j: (0, j))],
        out_specs=pl.BlockSpec((tm, cw), lambda i, j: (i, j)),
        out_shape=jax.ShapeDtypeStruct((M, width), BF16),
        compiler_params=_params(("parallel", "parallel")),
        name="rwkv_shift_mix",
    )(p, p, p, mu)


def _rwkv_prepare(g, G, d, rev, r_ref, k_ref, v_ref, la_ref, w2_ref, a2_ref, w0_ref, a0_ref, kk_ref, ka_ref):
    C = CHUNK
    rows = pl.ds(pl.multiple_of(g * (G * C), G * C), G * C)
    r = r_ref[0, rows, :].astype(F32)
    k = k_ref[0, rows, :].astype(F32)
    v = v_ref[0, rows, :].astype(F32)
    wd = la_ref[0, rows, d * LORA_PAD:(d + 1) * LORA_PAD]
    ad = la_ref[0, rows, (2 + d) * LORA_PAD:(3 + d) * LORA_PAD]
    lw = -RWKV_DECAY_SCALE * _sigmoid(w0_ref[d:d + 1, :] + jnp.dot(wd, w2_ref[d], preferred_element_type=F32))
    a_sig = _sigmoid(a0_ref[d:d + 1, :] + jnp.dot(ad, a2_ref[d], preferred_element_type=F32))

    lane = lax.broadcasted_iota(I32, (1, LANES), 1)
    head0 = lane < RWKV_N
    kx = k * kk_ref[...]
    sq = kx * kx
    ss = jnp.where(head0, jnp.sum(jnp.where(head0, sq, 0.0), axis=1, keepdims=True),
                   jnp.sum(jnp.where(head0, 0.0, sq), axis=1, keepdims=True))
    kk = kx * lax.rsqrt(ss + 1e-6)
    k_dir = k * (1.0 + (a_sig - 1.0) * ka_ref[...])
    b = kk * a_sig
    a = -kk
    bonus = r * k_dir

    g3 = lambda t: t.reshape(G, C, LANES)
    lw3, r3, a3, b3, kd3, v3 = g3(lw), g3(r), g3(a), g3(b), g3(k_dir), g3(v)
    ci = lax.broadcasted_iota(I32, (C, C), 0)
    cj = lax.broadcasted_iota(I32, (C, C), 1)
    tri = jnp.where((ci <= cj) if rev else (ci >= cj), 1.0, 0.0).astype(BF16)
    tri = jnp.broadcast_to(tri[None], (G, C, C))
    hi, lo = _split2(lw3)
    cum = (jnp.einsum('gij,gjk->gik', tri, hi, preferred_element_type=F32)
           + jnp.einsum('gij,gjk->gik', tri, lo, preferred_element_type=F32))
    end = 0 if rev else C - 1
    cum_end = cum[:, end:end + 1, :]
    e_neg = jnp.exp(-cum)
    e_tail = jnp.exp(cum_end - cum)
    h0 = head0.reshape(1, 1, LANES)
    a2s = _stack_heads(a3 * jnp.exp(cum - lw3), h0)
    r2s = _stack_heads(r3 * jnp.exp(cum), h0)
    k2s = _stack_heads(kd3 * e_neg, h0)
    b2s = _stack_heads(b3 * e_neg, h0)
    v2s = _stack_heads(v3, h0)
    kh2 = _stack_heads(kd3 * e_tail, h0).astype(BF16)
    bh2 = _stack_heads(b3 * e_tail, h0).astype(BF16)

    strict, incl, eye_b, same_blk = _pair_masks(rev)
    eye = jnp.where(eye_b, 1.0, 0.0)
    gram = _bmm_nt(jnp.concatenate([a2s, r2s], axis=1), jnp.concatenate([k2s, b2s], axis=1))
    a_ak = jnp.where(strict, gram[:, :PAIR, :PAIR], 0.0)
    a_ab = jnp.where(strict, gram[:, :PAIR, PAIR:], 0.0)
    a_rk = jnp.where(incl, gram[:, PAIR:, :PAIR], 0.0)
    a_rb = jnp.where(incl, gram[:, PAIR:, PAIR:], 0.0).astype(BF16)
    tinv = _unit_tri_inverse(a_ab, eye, same_blk)
    kv = _bmm(jnp.concatenate([a_ak, a_rk], axis=1), v2s)
    tt = _bmm(tinv, jnp.concatenate([a2s, kv[:, :PAIR]], axis=2))
    lhs = jnp.concatenate([tt[:, :, :LANES], r2s], axis=1).astype(BF16)
    u0 = tt[:, :, LANES:]
    y0 = kv[:, PAIR:]
    decay = jnp.exp(cum_end)
    return dict(lhs=lhs, u0=u0, y0=y0, a_rb=a_rb, bh2=bh2, kh2=kh2, v2=v2s.astype(BF16), decay=decay), bonus


def _rwkv_step(S, pa, i):
    C = CHUNK
    sr = _dot_nt(pa['lhs'][i], S)
    u2 = sr[:PAIR] + pa['u0'][i]
    y2 = sr[PAIR:] + pa['y0'][i] + _dot(pa['a_rb'][i], u2)
    y = y2[:C] + y2[C:]
    upd = _dot_tn(jnp.concatenate([pa['v2'][i], u2.astype(BF16)], axis=0),
                  jnp.concatenate([pa['kh2'][i], pa['bh2'][i]], axis=0))
    return S * pa['decay'][i] + upd, y


def _rwkv_scan_kernel(r_ref, k_ref, v_ref, la_ref, w2_ref, a2_ref, g2_ref, w0_ref, a0_ref,
                      kk_ref, ka_ref, rk_ref, gng_ref, gnb_ref, o_ref, yf_scr, yb_scr, bon_scr, *, nC, G):
    C = CHUNK
    nG = nC // G
    GC = G * C
    refs = (r_ref, k_ref, v_ref, la_ref, w2_ref, a2_ref, w0_ref, a0_ref, kk_ref, ka_ref)

    def body(g, carry):
        Sf, Sb = carry
        gb = nG - 1 - g
        pf, bonf = _rwkv_prepare(g, G, 0, False, *refs)
        pb, bonb = _rwkv_prepare(gb, G, 1, True, *refs)
        bon_scr[0, pl.ds(pl.multiple_of(g * GC, GC), GC), :] = bonf
        bon_scr[1, pl.ds(pl.multiple_of(gb * GC, GC), GC), :] = bonb
        for i in range(G):
            ib = G - 1 - i
            Sf, yf = _rwkv_step(Sf, pf, i)
            Sb, yb = _rwkv_step(Sb, pb, ib)
            yf_scr[pl.ds(pl.multiple_of(g * GC + i * C, C), C), :] = yf
            yb_scr[pl.ds(pl.multiple_of(gb * GC + ib * C, C), C), :] = yb
        return Sf, Sb

    z = jnp.zeros((LANES, LANES), F32)
    lax.fori_loop(0, nG, body, (z, z))

    def fin(g, _):
        sl = pl.ds(pl.multiple_of(g * GC, GC), GC)
        y = yf_scr[sl, :] + yb_scr[sl, :]
        lane = lax.broadcasted_iota(I32, (1, LANES), 1)
        head0 = lane < RWKV_N

        def seg_sum(x):
            s0 = jnp.sum(jnp.where(head0, x, 0.0), axis=1, keepdims=True)
            s1 = jnp.sum(jnp.where(head0, 0.0, x), axis=1, keepdims=True)
            return jnp.where(head0, s0, s1)

        mu = seg_sum(y) * (1.0 / RWKV_N)
        dlt = y - mu
        var = seg_sum(dlt * dlt) * (1.0 / RWKV_N)
        yn = dlt * lax.rsqrt(var + RWKV_GN_EPS) * gng_ref[...] + gnb_ref[...]
        bsum = seg_sum((bon_scr[0, sl, :] + bon_scr[1, sl, :]) * rk_ref[...])
        v = v_ref[0, sl, :].astype(F32)
        gate = jnp.dot(la_ref[0, sl, 4 * LORA_PAD:], g2_ref[...], preferred_element_type=F32)
        o_ref[0, sl, :] = ((yn + bsum * v) * gate).astype(o_ref.dtype)
        return 0

    lax.fori_loop(0, nG, fin, 0)


def rwkv_scan(z_rkv, la, w2, a2, g2, w0, a0, k_k, k_a, r_k, gn_g, gn_b, B, T):
    nC = T // CHUNK
    G = min(GROUP, nC)
    npair = RWKV_WIDTH // LANES
    nl = la.shape[-1]
    vec = lambda: pl.BlockSpec((1, LANES), lambda b, p: (0, p))
    return pl.pallas_call(
        functools.partial(_rwkv_scan_kernel, nC=nC, G=G),
        grid=(B, npair),
        in_specs=[pl.BlockSpec((1, T, LANES), lambda b, p: (b, 0, p)),
                  pl.BlockSpec((1, T, LANES), lambda b, p: (b, 0, npair + p)),
                  pl.BlockSpec((1, T, LANES), lambda b, p: (b, 0, 2 * npair + p)),
                  pl.BlockSpec((1, T, nl), lambda b, p: (b, 0, 0)),
                  pl.BlockSpec((2, LORA_PAD, LANES), lambda b, p: (0, 0, p)),
                  pl.BlockSpec((2, LORA_PAD, LANES), lambda b, p: (0, 0, p)),
                  pl.BlockSpec((GATE_LORA, LANES), lambda b, p: (0, p)),
                  pl.BlockSpec((2, LANES), lambda b, p: (0, p)),
                  pl.BlockSpec((2, LANES), lambda b, p: (0, p)),
                  vec(), vec(), vec(), vec(), vec()],
        out_specs=pl.BlockSpec((1, T, LANES), lambda b, p: (b, 0, p)),
        out_shape=jax.ShapeDtypeStruct((B, T, RWKV_WIDTH), BF16),
        scratch_shapes=[pltpu.VMEM((T, LANES), F32), pltpu.VMEM((T, LANES), F32),
                        pltpu.VMEM((2, T, LANES), F32)],
        compiler_params=_params(("parallel", "arbitrary")),
        name="rwkv_scan",
    )(z_rkv, z_rkv, z_rkv, la, w2, a2, g2, w0, a0, k_k, k_a, r_k, gn_g, gn_b)


def _gdn_conv_kernel(cur_ref, prev_ref, next_ref, w_ref, o_ref, *, per_b, q_blocks, k_blocks):
    cur, prev8, next8 = _halo_load(cur_ref, prev_ref, next_ref, per_b)
    half = CONV_WIDTH // 2
    acc = jnp.zeros_like(cur)
    for i in range(CONV_WIDTH):
        acc = acc + _shift_rows(cur, prev8, next8, i - half) * w_ref[i:i + 1, :]
    y = _silu(acc)
    j = pl.program_id(1)
    cw = y.shape[1]
    outs = []
    for s in range(cw // GDN_D):
        ys = y[:, s * GDN_D:(s + 1) * GDN_D]
        nrm = lax.rsqrt(jnp.sum(ys * ys, axis=1, keepdims=True) + 1e-6)
        outs.append(ys * nrm)
    yn = jnp.concatenate(outs, axis=1) if len(outs) > 1 else outs[0]
    res = jnp.where(j < q_blocks, yn * (GDN_D ** -0.5), jnp.where(j < k_blocks, yn, y))
    o_ref[...] = res.astype(o_ref.dtype)


def gdn_conv_op(p, conv_w, T):
    M = p.shape[0]
    tm = min(512, T)
    cw = 512
    per_b = T // tm
    cur, prv, nxt = _halo_specs(tm, cw, per_b, M // 8, lambda j: j)
    return pl.pallas_call(
        functools.partial(_gdn_conv_kernel, per_b=per_b, q_blocks=GDN_KEY_WIDTH // cw,
                          k_blocks=2 * GDN_KEY_WIDTH // cw),
        grid=(M // tm, GDN_QKV // cw),
        in_specs=[cur, prv, nxt, pl.BlockSpec((CONV_WIDTH, cw), lambda i, j: (0, j))],
        out_specs=pl.BlockSpec((tm, cw), lambda i, j: (i, j)),
        out_shape=jax.ShapeDtypeStruct((M, GDN_QKV), BF16),
        compiler_params=_params(("parallel", "parallel")),
        name="gdn_conv",
    )(p, p, p, conv_w)


def _gdn_gate_kernel(ba_ref, alog_ref, dtb_ref, o_ref):
    C = CHUNK
    x = ba_ref[...]
    tm = x.shape[0]
    nh = 2 * GDN_V_HEADS
    beta = _sigmoid(x)
    a = x + dtb_ref[...]
    sp = jnp.maximum(a, 0.0) + jnp.log1p(jnp.exp(-jnp.abs(a)))
    g = -jnp.exp(alog_ref[...]) * sp
    i = lax.broadcasted_iota(I32, (C, C), 0)
    j = lax.broadcasted_iota(I32, (C, C), 1)
    lower = jnp.where(i >= j, 1.0, 0.0).astype(BF16)
    upper = jnp.where(i <= j, 1.0, 0.0).astype(BF16)
    col = lax.broadcasted_iota(I32, (C, 2 * nh), 1)
    for c in range(tm // C):
        gc = g[c * C:(c + 1) * C]
        hi, mid, lo = _split3(gc)
        fw = (jnp.dot(lower, hi, preferred_element_type=F32) + jnp.dot(lower, mid, preferred_element_type=F32)
              + jnp.dot(lower, lo, preferred_element_type=F32))
        bw = (jnp.dot(upper, hi, preferred_element_type=F32) + jnp.dot(upper, mid, preferred_element_type=F32)
              + jnp.dot(upper, lo, preferred_element_type=F32))
        o_ref[c * C:(c + 1) * C, :] = jnp.where(col < nh, beta[c * C:(c + 1) * C],
                                                jnp.where(col < nh + GDN_V_HEADS, fw, bw))


def gdn_gates(ba, a_log, dt_bias, T):
    M = ba.shape[0]
    tm = min(512, T)
    nh = 2 * GDN_V_HEADS
    return pl.pallas_call(
        _gdn_gate_kernel,
        grid=(M // tm,),
        in_specs=[pl.BlockSpec((tm, 2 * nh), lambda i: (i, 0)),
                  pl.BlockSpec((1, 2 * nh), lambda i: (0, 0)),
                  pl.BlockSpec((1, 2 * nh), lambda i: (0, 0))],
        out_specs=pl.BlockSpec((tm, 2 * nh), lambda i: (i, 0)),
        out_shape=jax.ShapeDtypeStruct((M, 2 * nh), F32),
        compiler_params=_params(("parallel",)),
        name="gdn_gates",
    )(ba, a_log, dt_bias)


def _gdn_prepare(g, G, d, rev, kh, q_ref, k_ref, v_ref, bg_ref):
    C = CHUNK
    rows = pl.ds(pl.multiple_of(g * (G * C), G * C), G * C)
    q = q_ref[0, rows, :].astype(F32).reshape(G, C, GDN_D)
    k = k_ref[0, rows, :].astype(F32).reshape(G, C, GDN_D)
    v = v_ref[0, rows, :].astype(F32).reshape(G, C, 2 * GDN_D)
    v2 = jnp.concatenate([v[:, :, :GDN_D], v[:, :, GDN_D:]], axis=1)
    k2 = jnp.concatenate([k, k], axis=1)
    q2 = jnp.concatenate([q, q], axis=1)
    chunks = pl.ds(g * G, G)
    beta_row = bg_ref[0, chunks, pl.ds(d * GDN_K_HEADS + kh, 1), :]
    gc_row = bg_ref[0, chunks, pl.ds((2 + d) * GDN_K_HEADS + kh, 1), :]
    strict, incl, eye_b, same_blk = _pair_masks(rev)
    eye = jnp.where(eye_b, 1.0, 0.0)
    beta_col = _rows_to_cols(beta_row, eye_b)
    gc_col = _rows_to_cols(gc_row, eye_b)
    decay = jnp.where(incl, jnp.exp(jnp.where(incl, gc_col - gc_row, 0.0)), 0.0)
    gram = _bmm_nt(jnp.concatenate([k2, q2], axis=1), k2)
    lower = jnp.where(strict, gram[:, :PAIR] * beta_col * decay, 0.0)
    attn = (gram[:, PAIR:] * decay).astype(BF16)
    t_inv = _unit_tri_inverse(-lower, eye, same_blk)
    eg = jnp.exp(gc_col)
    kb = k2 * beta_col
    uw = _bmm(t_inv, jnp.concatenate([v2 * beta_col, kb * eg], axis=2))
    end = 0 if rev else C - 1
    gl0 = gc_row[:, :, end:end + 1]
    gl1 = gc_row[:, :, C + end:C + end + 1]
    rowi = lax.broadcasted_iota(I32, (1, PAIR, 1), 1)
    gl_col = jnp.where(rowi < C, gl0, gl1)
    kt = (k2 * jnp.exp(gl_col - gc_col)).astype(BF16)
    wq = jnp.concatenate([uw[:, :, GDN_D:], q2 * eg], axis=1).astype(BF16)
    lane2 = lax.broadcasted_iota(I32, (1, 1, 2 * GDN_D), 2)
    decay_s = jnp.where(lane2 < GDN_D, jnp.exp(gl0), jnp.exp(gl1))
    return dict(u=uw[:, :, :GDN_D], wq=wq, attn=attn, kt=kt, decay=decay_s)


def _gdn_step(S, pa, i):
    C = CHUNK
    ws = _dot(pa['wq'][i], S)
    head0_rows = lax.broadcasted_iota(I32, (PAIR, 1), 0) < C
    w_s = jnp.where(head0_rows, ws[:PAIR, :GDN_D], ws[:PAIR, GDN_D:])
    q_s = jnp.where(head0_rows, ws[PAIR:, :GDN_D], ws[PAIR:, GDN_D:])
    v_new = pa['u'][i] - w_s
    o2 = q_s + _dot(pa['attn'][i], v_new)
    vcat = jnp.concatenate([jnp.where(head0_rows, v_new, 0.0), jnp.where(head0_rows, 0.0, v_new)], axis=1)
    S_new = S * pa['decay'][i] + _dot_tn(pa['kt'][i], vcat)
    return S_new, jnp.concatenate([o2[:C], o2[C:]], axis=1)


def _gdn_scan_kernel(q_ref, k_ref, v_ref, z_ref, bg_ref, ng_ref, o_ref, of_scr, ob_scr, *, nC, G):
    C = CHUNK
    nG = nC // G
    GC = G * C
    kh = pl.program_id(1)

    def body(g, carry):
        Sf, Sb = carry
        gb = nG - 1 - g
        pf = _gdn_prepare(g, G, 0, False, kh, q_ref, k_ref, v_ref, bg_ref)
        pb = _gdn_prepare(gb, G, 1, True, kh, q_ref, k_ref, v_ref, bg_ref)
        for i in range(G):
            ib = G - 1 - i
            Sf, of = _gdn_step(Sf, pf, i)
            Sb, ob = _gdn_step(Sb, pb, ib)
            of_scr[pl.ds(pl.multiple_of(g * GC + i * C, C), C), :] = of
            ob_scr[pl.ds(pl.multiple_of(gb * GC + ib * C, C), C), :] = ob
        return Sf, Sb

    z0 = jnp.zeros((GDN_D, 2 * GDN_D), F32)
    lax.fori_loop(0, nG, body, (z0, z0))

    def fin(g, _):
        sl = pl.ds(pl.multiple_of(g * GC, GC), GC)
        o = of_scr[sl, :] + ob_scr[sl, :]
        z = z_ref[0, sl, :].astype(F32)
        for s in range(2):
            os_ = o[:, s * GDN_D:(s + 1) * GDN_D]
            on = os_ * lax.rsqrt(jnp.mean(os_ * os_, axis=-1, keepdims=True) + 1e-6) * ng_ref[...]
            o_ref[0, sl, s * GDN_D:(s + 1) * GDN_D] = (on * _silu(z[:, s * GDN_D:(s + 1) * GDN_D])).astype(o_ref.dtype)
        return 0

    lax.fori_loop(0, nG, fin, 0)


def gdn_scan(qkv, z_src, z_col0, bg, norm_g, B, T):
    nC = T // CHUNK
    G = min(GROUP, nC)
    kb = GDN_KEY_WIDTH // GDN_D
    vb = 2 * GDN_KEY_WIDTH // (2 * GDN_D)
    return pl.pallas_call(
        functools.partial(_gdn_scan_kernel, nC=nC, G=G),
        grid=(B, GDN_K_HEADS),
        in_specs=[pl.BlockSpec((1, T, GDN_D), lambda b, h: (b, 0, h)),
                  pl.BlockSpec((1, T, GDN_D), lambda b, h: (b, 0, kb + h)),
                  pl.BlockSpec((1, T, 2 * GDN_D), lambda b, h: (b, 0, vb + h)),
                  pl.BlockSpec((1, T, 2 * GDN_D), lambda b, h: (b, 0, z_col0 + h)),
                  pl.BlockSpec((1, nC, 4 * GDN_K_HEADS, PAIR), lambda b, h: (b, 0, 0, 0)),
                  pl.BlockSpec((1, GDN_D), lambda b, h: (0, 0))],
        out_specs=pl.BlockSpec((1, T, 2 * GDN_D), lambda b, h: (b, 0, h)),
        out_shape=jax.ShapeDtypeStruct((B, T, GDN_VAL_WIDTH), BF16),
        scratch_shapes=[pltpu.VMEM((T, 2 * GDN_D), F32), pltpu.VMEM((T, 2 * GDN_D), F32)],
        compiler_params=_params(("parallel", "arbitrary")),
        name="gdn_scan",
    )(qkv, qkv, qkv, z_src, bg, norm_g)


def _router_kernel(x_ref, sc_ref, sh_ref, wr_ref, h_ref, aff_ref):
    h = x_ref[...] * (1.0 + sc_ref[0]) + sh_ref[0]
    h_ref[...] = h.astype(BF16)
    hh, hm, hl = _split3(h)
    wh, wm, wl = _split3(wr_ref[...])
    nt = lambda a, b: lax.dot_general(a, b, (((1,), (1,)), ((), ())), preferred_element_type=F32)
    logits = (nt(wh, hh) + (nt(wh, hm) + nt(wm, hh)) + (nt(wh, hl) + nt(wl, hh) + nt(wm, hm)))
    m = jnp.max(logits, axis=0, keepdims=True)
    e = jnp.exp(logits - m)
    aff_ref[0] = e / jnp.sum(e, axis=0, keepdims=True)


def moe_router_op(x, sc, sh, wr_t, B, T):
    M, D = x.shape
    E = wr_t.shape[0]
    tm = min(512, T)
    per_b = T // tm
    return pl.pallas_call(
        _router_kernel,
        grid=(M // tm,),
        in_specs=[pl.BlockSpec((tm, D), lambda i: (i, 0)),
                  pl.BlockSpec((1, 1, D), lambda i: (i // per_b, 0, 0)),
                  pl.BlockSpec((1, 1, D), lambda i: (i // per_b, 0, 0)),
                  pl.BlockSpec((E, D), lambda i: (0, 0))],
        out_specs=[pl.BlockSpec((tm, D), lambda i: (i, 0)),
                   pl.BlockSpec((1, E, tm), lambda i: (i // per_b, 0, i % per_b))],
        out_shape=[jax.ShapeDtypeStruct((M, D), BF16), jax.ShapeDtypeStruct((B, E, T), F32)],
        compiler_params=_params(("parallel",)),
        name="moe_router",
    )(x, sc, sh, wr_t)


def _lane_cumsum(m01, blk):
    E, T = m01.shape
    i = lax.broadcasted_iota(I32, (blk, blk), 0)
    j = lax.broadcasted_iota(I32, (blk, blk), 1)
    upper = jnp.where(i <= j, 1.0, 0.0).astype(BF16)
    carry = jnp.zeros((E, 1), F32)
    parts = []
    for c in range(T // blk):
        seg = m01[:, c * blk:(c + 1) * blk]
        cs = jnp.dot(seg.astype(BF16), upper, preferred_element_type=F32) + carry
        parts.append(cs)
        carry = cs[:, blk - 1:blk]
    return jnp.concatenate(parts, axis=1) if len(parts) > 1 else parts[0]


def _topk_kernel(aff_ref, pos_ref, los_ref, *, cap, ts):
    a = aff_ref[0]
    E, T = a.shape
    bits = pltpu.bitcast(a, I32)

    def body(i, t):
        cand = t | lax.shift_left(jnp.int32(1), 30 - i)
        cnt = jnp.sum((bits >= cand).astype(I32), axis=1, keepdims=True)
        return jnp.where(cnt >= cap, cand, t)

    thr = lax.fori_loop(0, 31, body, jnp.zeros((E, 1), I32))
    gt = bits > thr
    eq = bits == thr
    need = cap - jnp.sum(gt.astype(I32), axis=1, keepdims=True)
    blk = min(512, T)
    eq01 = jnp.where(eq, 1.0, 0.0)
    eq_rank = _lane_cumsum(eq01, blk) - eq01
    sel = jnp.logical_or(gt, jnp.logical_and(eq, eq_rank < need.astype(F32)))
    sel01 = jnp.where(sel, 1.0, 0.0)
    slot = _lane_cumsum(sel01, blk) - sel01
    pos_ref[0] = jnp.where(sel, slot.astype(I32), -1)
    ti = lax.broadcasted_iota(I32, (T, LANES), 0)
    wi = lax.broadcasted_iota(I32, (T, LANES), 1)
    before = jnp.where(ti < wi * ts, 1.0, 0.0).astype(BF16)
    los_ref[0] = jnp.dot(sel01.astype(BF16), before, preferred_element_type=F32).astype(I32)


def moe_topk(aff_t, cap, ts):
    B, E, T = aff_t.shape
    return pl.pallas_call(
        functools.partial(_topk_kernel, cap=cap, ts=ts),
        grid=(B,),
        in_specs=[pl.BlockSpec((1, E, T), lambda b: (b, 0, 0))],
        out_specs=[pl.BlockSpec((1, E, T), lambda b: (b, 0, 0)),
                   pl.BlockSpec((1, E, LANES), lambda b: (b, 0, 0))],
        out_shape=[jax.ShapeDtypeStruct((B, E, T), I32), jax.ShapeDtypeStruct((B, E, LANES), I32)],
        compiler_params=_params(("parallel",)),
        name="moe_topk",
    )(aff_t)


def _slot_windows(cap):
    win = min(LANES, cap)
    return win, cap // win


def _gather_kernel(los_ref, pos_ref, h_ref, o_ref, acc, *, cap, nt):
    b, e, c = pl.program_id(0), pl.program_id(1), pl.program_id(2)
    base = (b * pl.num_programs(1) + e) * (nt + 1) + c
    lo = los_ref[base]
    hi = los_ref[base + 1]
    tc = h_ref.shape[0]
    win, nwin = _slot_windows(cap)

    @pl.when(c == 0)
    def _():
        acc[...] = jnp.zeros(acc.shape, F32)

    for w in range(nwin):
        @pl.when(jnp.logical_and(lo < (w + 1) * win, hi > w * win))
        def _():
            slot = lax.broadcasted_iota(I32, (win, tc), 0) + w * win
            onehot = jnp.where(slot == pos_ref[0, 0], 1.0, 0.0).astype(BF16)
            acc[w * win:(w + 1) * win, :] += jnp.dot(onehot, h_ref[...], preferred_element_type=F32)

    @pl.when(c == pl.num_programs(2) - 1)
    def _():
        o_ref[0, 0] = acc[...].astype(o_ref.dtype)


def moe_gather(los, pos4, h, B, T, cap, ts):
    E = pos4.shape[1]
    D = h.shape[1]
    nt = T // ts
    return pl.pallas_call(
        functools.partial(_gather_kernel, cap=cap, nt=nt),
        grid_spec=pltpu.PrefetchScalarGridSpec(
            num_scalar_prefetch=1,
            grid=(B, E, nt),
            in_specs=[pl.BlockSpec((1, 1, 1, ts), lambda b, e, c, s: (b, e, 0, c)),
                      pl.BlockSpec((ts, D), lambda b, e, c, s: (b * nt + c, 0))],
            out_specs=pl.BlockSpec((1, 1, cap, D), lambda b, e, c, s: (b, e, 0, 0)),
            scratch_shapes=[pltpu.VMEM((cap, D), F32)]),
        out_shape=jax.ShapeDtypeStruct((B, E, cap, D), BF16),
        compiler_params=_params(("parallel", "parallel", "arbitrary")),
        name="moe_gather",
    )(los, pos4, h)


def _ffn_kernel(x_ref, wg_ref, wu_ref, wd_ref, o_ref):
    x = x_ref[0, 0]
    g = jnp.dot(x, wg_ref[0], preferred_element_type=F32)
    u = jnp.dot(x, wu_ref[0], preferred_element_type=F32)
    hid = (_silu(g) * u).astype(BF16)
    o_ref[0, 0] = jnp.dot(hid, wd_ref[0], preferred_element_type=F32).astype(o_ref.dtype)


def moe_ffn(xs, wg, wu, wd):
    B, E, cap, D = xs.shape
    Fh = wg.shape[-1]
    return pl.pallas_call(
        _ffn_kernel,
        grid=(E, B),
        in_specs=[pl.BlockSpec((1, 1, cap, D), lambda e, b: (b, e, 0, 0)),
                  pl.BlockSpec((1, D, Fh), lambda e, b: (e, 0, 0)),
                  pl.BlockSpec((1, D, Fh), lambda e, b: (e, 0, 0)),
                  pl.BlockSpec((1, Fh, D), lambda e, b: (e, 0, 0))],
        out_specs=pl.BlockSpec((1, 1, cap, D), lambda e, b: (b, e, 0, 0)),
        out_shape=jax.ShapeDtypeStruct((B, E, cap, D), BF16),
        compiler_params=_params(("parallel", "arbitrary")),
        name="moe_ffn",
    )(xs, wg, wu, wd)


def _combine_kernel(los_ref, pos_ref, gate_ref, y_ref, x_ref, g_ref, lg_ref, lb_ref, o_ref, acc, *, cap, nt):
    b, i, e = pl.program_id(0), pl.program_id(1), pl.program_id(2)
    base = (b * pl.num_programs(2) + e) * (nt + 1) + i
    lo = los_ref[base]
    hi = los_ref[base + 1]
    pos = pos_ref[0].astype(F32)
    gate = gate_ref[0]
    tt, E = pos.shape
    lane = lax.broadcasted_iota(I32, (tt, E), 1)
    pcol = jnp.sum(jnp.where(lane == e, pos, 0.0), axis=1, keepdims=True)
    gcol = jnp.sum(jnp.where(lane == e, gate, 0.0), axis=1, keepdims=True)
    win, nwin = _slot_windows(cap)

    @pl.when(e == 0)
    def _():
        acc[...] = jnp.zeros(acc.shape, F32)

    for w in range(nwin):
        @pl.when(jnp.logical_and(lo < (w + 1) * win, hi > w * win))
        def _():
            slot = (lax.broadcasted_iota(I32, (tt, win), 1) + w * win).astype(F32)
            onehot = jnp.where(slot == pcol, 1.0, 0.0).astype(BF16)
            acc[...] += gcol * jnp.dot(onehot, y_ref[0, 0, w * win:(w + 1) * win, :], preferred_element_type=F32)

    @pl.when(e == pl.num_programs(2) - 1)
    def _():
        v = DEEPNORM_ALPHA * x_ref[...] + g_ref[0] * acc[...]
        o_ref[...] = _layer_norm_rows(v, lg_ref[...], lb_ref[...])


def moe_combine(los, pos_tm, aff_tm, yd, x, g, lg, lb, B, T, cap, ts):
    E = pos_tm.shape[-1]
    M, D = x.shape
    nt = T // ts
    return pl.pallas_call(
        functools.partial(_combine_kernel, cap=cap, nt=nt),
        grid_spec=pltpu.PrefetchScalarGridSpec(
            num_scalar_prefetch=1,
            grid=(B, nt, E),
            in_specs=[pl.BlockSpec((1, ts, E), lambda b, i, e, s: (b, i, 0)),
                      pl.BlockSpec((1, ts, E), lambda b, i, e, s: (b, i, 0)),
                      pl.BlockSpec((1, 1, cap, D), lambda b, i, e, s: (b, e, 0, 0)),
                      pl.BlockSpec((ts, D), lambda b, i, e, s: (b * nt + i, 0)),
                      pl.BlockSpec((1, 1, D), lambda b, i, e, s: (b, 0, 0)),
                      pl.BlockSpec((1, D), lambda b, i, e, s: (0, 0)),
                      pl.BlockSpec((1, D), lambda b, i, e, s: (0, 0))],
            out_specs=pl.BlockSpec((ts, D), lambda b, i, e, s: (b * nt + i, 0)),
            scratch_shapes=[pltpu.VMEM((ts, D), F32)]),
        out_shape=jax.ShapeDtypeStruct((M, D), F32),
        compiler_params=_params(("parallel", "parallel", "arbitrary")),
        name="moe_combine",
    )(los, pos_tm, aff_tm, yd, x, g, lg, lb)


def moe_layer(x, sc, sh, g, lg, lb, w_router, w_gate, w_up, w_down, B, T):
    cap = CAPACITY_FACTOR * T // N_EXPERTS
    ts = min(512, T)
    nt = T // ts
    h, aff_t = moe_router_op(x, sc, sh, w_router.T, B, T)
    pos, los = moe_topk(aff_t, cap, ts)
    los = los[:, :, :nt + 1].reshape(-1)
    xs = moe_gather(los, pos[:, :, None, :], h, B, T, cap, ts)
    yd = moe_ffn(xs, w_gate.astype(BF16), w_up.astype(BF16), w_down.astype(BF16))
    pos_tm = jnp.swapaxes(pos, 1, 2)
    aff_tm = jnp.swapaxes(aff_t, 1, 2)
    return moe_combine(los, pos_tm, aff_tm, yd, x, g, lg, lb, B, T, cap, ts)


def _pad_cols(w, n):
    return jnp.pad(w, ((0, 0), (0, n - w.shape[1])))


def _rot_cols(w):
    half = w.shape[-1] // 2
    return jnp.concatenate([-w[..., half:], w[..., :half]], axis=-1)


def _even_w_in(w):
    D = w.shape[0]
    o = MLA_IN
    rkv = w[:, o:o + 3 * RWKV_WIDTH]
    cq = w[:, :Q_LORA]
    ckv = w[:, Q_LORA:Q_LORA + KV_LORA]
    kr = w[:, Q_LORA + KV_LORA:MLA_IN]
    l0 = o + 3 * RWKV_WIDTH
    lora = [_pad_cols(w[:, l0 + i * DECAY_LORA:l0 + (i + 1) * DECAY_LORA], LORA_PAD) for i in range(4)]
    gd = w[:, l0 + 4 * DECAY_LORA:]
    pad = jnp.zeros((D, LANES), w.dtype)
    out = jnp.concatenate([rkv, cq, ckv, kr, _rot_cols(kr), pad] + lora + [gd], axis=1)
    assert out.shape[1] == EVEN_COLS
    return out.astype(BF16)


def _even_mu(mu):
    rkv = mu[:3 * RWKV_WIDTH]
    l0 = 3 * RWKV_WIDTH
    lora = [jnp.pad(mu[l0 + i * DECAY_LORA:l0 + (i + 1) * DECAY_LORA], (0, LORA_PAD - DECAY_LORA)) for i in range(4)]
    gd = mu[l0 + 4 * DECAY_LORA:]
    return rkv.reshape(1, -1), jnp.concatenate(lora + [gd]).reshape(1, -1)


def _pad_rows(w, n):
    return jnp.pad(w, ((0, 0), (0, n - w.shape[1]), (0, 0)))


def even_mixer(x, sc, sh, positions_tables, w_in, shift_mu, q_norm, w_uq, kv_norm, w_ukv, w0, w2, a0, a2, g2,
               k_k, k_a, r_k, gn_g, gn_b, B, T):
    cs, sn = positions_tables
    H = MLA_HEADS
    p = mm_mod(x, sc, sh, _even_w_in(w_in), T, 512, 1024, F32)
    wq = w_uq.reshape(Q_LORA, H, NOPE_DIM + ROPE_DIM)
    wq = jnp.concatenate([wq, _rot_cols(wq[..., NOPE_DIM:])], axis=-1)
    wq = jnp.swapaxes(wq, 0, 1).astype(BF16)
    wkv = w_ukv.reshape(KV_LORA, H, NOPE_DIM + V_DIM)
    wk = jnp.swapaxes(wkv[..., :NOPE_DIM], 0, 1).astype(BF16)
    wvt = jnp.transpose(wkv[..., NOPE_DIM:], (1, 2, 0)).astype(BF16)
    q, k, vt = mla_proj(p, cs, sn, q_norm.reshape(1, -1), kv_norm.reshape(1, -1), wq, wk, wvt, B, T)
    attn = mla_attention(q, k, vt)
    mu_rkv, mu_lora = _even_mu(shift_mu)
    z_rkv = shift_mix(p, mu_rkv, T, 0, 3 * RWKV_WIDTH, 1024, False)
    la = shift_mix(p, mu_lora, T, 8448, 768, 768, True)
    y_rwkv = rwkv_scan(z_rkv.reshape(B, T, -1), la.reshape(B, T, -1),
                       _pad_rows(w2, LORA_PAD).astype(BF16), _pad_rows(a2, LORA_PAD).astype(BF16),
                       g2.astype(BF16), w0, a0, k_k.reshape(1, -1), k_a.reshape(1, -1), r_k.reshape(1, -1),
                       gn_g.reshape(1, -1), gn_b.reshape(1, -1), B, T)
    return attn.reshape(B * T, -1), y_rwkv.reshape(B * T, -1)


def odd_mixer(x, sc, sh, w_in, conv_w, a_log, dt_bias, norm_g, B, T):
    M = B * T
    nC = T // CHUNK
    wb = w_in.astype(BF16)
    n_main = GDN_QKV + GDN_VAL_WIDTH
    p = mm_mod(x, sc, sh, wb[:, :n_main], T, 512, 1024, F32)
    ba = mm_mod(x, sc, sh, wb[:, n_main:], T, 512, 128, F32)
    qkv = gdn_conv_op(p, conv_w, T)
    nh = 2 * GDN_V_HEADS
    bg = gdn_gates(ba, jnp.pad(a_log.reshape(1, -1), ((0, 0), (nh, 0))),
                   jnp.pad(dt_bias.reshape(1, -1), ((0, 0), (nh, 0))), T)
    bg = bg.reshape(B, nC, CHUNK, 2, 2, GDN_K_HEADS, 2)
    bg = jnp.transpose(bg, (0, 1, 3, 4, 5, 6, 2)).reshape(B, nC, 4 * GDN_K_HEADS, PAIR)
    o = gdn_scan(qkv.reshape(B, T, -1), p.reshape(B, T, -1), GDN_QKV // (2 * GDN_D), bg, norm_g.reshape(1, -1), B, T)
    return o.reshape(M, -1)


def kernel(x, c, positions, ada_w, ada_b, ln_g, ln_b, e_w_in, e_shift_mu, mla_q_norm, mla_w_uq, mla_kv_norm,
           mla_w_ukv, rwkv_w0, rwkv_w2, rwkv_a0, rwkv_a2, rwkv_g2, rwkv_k_k, rwkv_k_a, rwkv_r_k, rwkv_gn_g,
           rwkv_gn_b, e_w_out, o_w_in, gdn_conv, gdn_a_log, gdn_dt_bias, gdn_norm, o_w_out, moe_router,
           moe_w_gate, moe_w_up, moe_w_down):
    B, T, D = x.shape
    M = B * T
    depth = ada_w.shape[0]
    mod = ada_modulation(c, ada_w, ada_b)
    tables = rope_tables(positions)
    xf = x.reshape(M, D)
    for i in range(depth):
        m6 = mod[i].reshape(B, 6, 1, D)
        sh_m, sc_m, g_m, sh_f, sc_f, g_f = (m6[:, n] for n in range(6))
        j = i // 2
        lg0, lb0 = ln_g[i, 0].reshape(1, D), ln_b[i, 0].reshape(1, D)
        lg1, lb1 = ln_g[i, 1].reshape(1, D), ln_b[i, 1].reshape(1, D)
        if i % 2 == 0:
            a0, a1 = even_mixer(xf, sc_m, sh_m, tables, e_w_in[j], e_shift_mu[j], mla_q_norm[j], mla_w_uq[j],
                                mla_kv_norm[j], mla_w_ukv[j], rwkv_w0[j], rwkv_w2[j], rwkv_a0[j], rwkv_a2[j],
                                rwkv_g2[j], rwkv_k_k[j], rwkv_k_a[j], rwkv_r_k[j], rwkv_gn_g[j], rwkv_gn_b[j],
                                B, T)
            xf = mm_ln(a0, 0, a1, 0, e_w_out[j].astype(BF16), xf, g_m, lg0, lb0, T, 512)
        else:
            o = odd_mixer(xf, sc_m, sh_m, o_w_in[j], gdn_conv[j], gdn_a_log[j], gdn_dt_bias[j], gdn_norm[j], B, T)
            xf = mm_ln(o, 0, o, 1, o_w_out[j].astype(BF16), xf, g_m, lg0, lb0, T, 512)
        xf = moe_layer(xf, sc_f, sh_f, g_f, lg1, lb1, moe_router[i], moe_w_gate[i], moe_w_up[i], moe_w_down[i], B, T)
    return xf.reshape(B, T, D)
```

```python
import functools
import math

import jax
import jax.numpy as jnp
from jax import lax
from jax.experimental import pallas as pl
from jax.experimental.pallas import tpu as pltpu

F32 = jnp.float32
BF16 = jnp.bfloat16
I32 = jnp.int32

DEPTH = 4
DEEPNORM_ALPHA = (2 * DEPTH) ** 0.25
MLA_HEADS = 16
Q_LORA = 1536
KV_LORA = 512
NOPE_DIM = 128
ROPE_DIM = 64
V_DIM = 128
ROPE_THETA = 10000.0
MLA_IN = Q_LORA + KV_LORA + ROPE_DIM
RWKV_HEADS = 32
RWKV_N = 64
RWKV_WIDTH = RWKV_HEADS * RWKV_N
DECAY_LORA = 96
AAA_LORA = 96
GATE_LORA = 256
RWKV_DECAY_SCALE = 0.6065306597126334
RWKV_GN_EPS = 64e-5
GDN_K_HEADS = 16
GDN_V_HEADS = 32
GDN_D = 128
GDN_KEY_WIDTH = GDN_K_HEADS * GDN_D
GDN_VAL_WIDTH = GDN_V_HEADS * GDN_D
GDN_QKV = 2 * GDN_KEY_WIDTH + GDN_VAL_WIDTH
CONV_WIDTH = 5
N_EXPERTS = 16
CAPACITY_FACTOR = 2

LANES = 128
CHUNK = 64
LOG2_CHUNK = 6
PAIR = 2 * CHUNK
GROUP = 8
INV_BLOCK = 16
LOG2_INV_BLOCK = 4
LORA_PAD = 128
EVEN_COLS = 9216
VMEM_LIMIT = 56 * 1024 * 1024


def _params(sem, vmem=VMEM_LIMIT):
    return pltpu.CompilerParams(dimension_semantics=sem, vmem_limit_bytes=vmem)


def _dot(a, b):
    return jnp.dot(a.astype(BF16), b.astype(BF16), preferred_element_type=F32)


def _dot_nt(a, b):
    return lax.dot_general(a.astype(BF16), b.astype(BF16), (((1,), (1,)), ((), ())),
                           preferred_element_type=F32)


def _dot_tn(a, b):
    return lax.dot_general(a.astype(BF16), b.astype(BF16), (((0,), (0,)), ((), ())),
                           preferred_element_type=F32)


def _bmm(a, b):
    return jnp.einsum('gij,gjk->gik', a.astype(BF16), b.astype(BF16), preferred_element_type=F32)


def _bmm_nt(a, b):
    return jnp.einsum('gik,gjk->gij', a.astype(BF16), b.astype(BF16), preferred_element_type=F32)


def _split2(x):
    hi = x.astype(BF16)
    lo = (x - hi.astype(F32)).astype(BF16)
    return hi, lo


def _split3(x):
    hi = x.astype(BF16)
    r1 = x - hi.astype(F32)
    mid = r1.astype(BF16)
    lo = (r1 - mid.astype(F32)).astype(BF16)
    return hi, mid, lo


def _sigmoid(x):
    return 1.0 / (1.0 + jnp.exp(-x))


def _silu(x):
    return x * _sigmoid(x)


def _ada_kernel(c_ref, w_ref, b_ref, o_ref):
    c = c_ref[...]
    o_ref[0] = _dot(_silu(c), w_ref[0]) + b_ref[0]


def ada_modulation(c, ada_w, ada_b):
    L, D, N = ada_w.shape
    B = c.shape[0]
    rows = 8
    cp = jnp.zeros((rows, D), F32).at[:B].set(c)
    tn = 1024
    out = pl.pallas_call(
        _ada_kernel,
        grid=(L, N // tn),
        in_specs=[pl.BlockSpec((rows, D), lambda l, j: (0, 0)),
                  pl.BlockSpec((1, D, tn), lambda l, j: (l, 0, j)),
                  pl.BlockSpec((1, 1, tn), lambda l, j: (l, 0, j))],
        out_specs=pl.BlockSpec((1, rows, tn), lambda l, j: (l, 0, j)),
        out_shape=jax.ShapeDtypeStruct((L, rows, N), F32),
        compiler_params=_params(("parallel", "parallel")),
        name="ada_mod",
    )(cp, ada_w, ada_b.reshape(L, 1, N))
    return out[:, :B]


def _mm_mod_kernel(x_ref, sc_ref, sh_ref, w_ref, o_ref, h_scr):
    @pl.when(pl.program_id(1) == 0)
    def _():
        h_scr[...] = (x_ref[...] * (1.0 + sc_ref[0]) + sh_ref[0]).astype(BF16)

    o_ref[...] = jnp.dot(h_scr[...], w_ref[...], preferred_element_type=F32).astype(o_ref.dtype)


def mm_mod(x, sc, sh, w, T, tm, tn, out_dtype):
    M, D = x.shape
    N = w.shape[1]
    tm = min(tm, T)
    tn = min(tn, N)
    per_b = T // tm
    return pl.pallas_call(
        _mm_mod_kernel,
        grid=(M // tm, N // tn),
        in_specs=[pl.BlockSpec((tm, D), lambda i, j: (i, 0)),
                  pl.BlockSpec((1, 1, D), lambda i, j: (i // per_b, 0, 0)),
                  pl.BlockSpec((1, 1, D), lambda i, j: (i // per_b, 0, 0)),
                  pl.BlockSpec((D, tn), lambda i, j: (0, j))],
        out_specs=pl.BlockSpec((tm, tn), lambda i, j: (i, j)),
        out_shape=jax.ShapeDtypeStruct((M, N), out_dtype),
        scratch_shapes=[pltpu.VMEM((tm, D), BF16)],
        compiler_params=_params(("parallel", "arbitrary")),
        name="mm_mod",
    )(x, sc, sh, w)


def _layer_norm_rows(v, lg, lb):
    mu = jnp.mean(v, axis=-1, keepdims=True)
    d = v - mu
    var = jnp.mean(d * d, axis=-1, keepdims=True)
    return d * lax.rsqrt(var + 1e-5) * lg + lb


def _mm_ln_kernel(a0_ref, a1_ref, w_ref, x_ref, g_ref, lg_ref, lb_ref, o_ref, acc):
    k = pl.program_id(1)

    @pl.when(k == 0)
    def _():
        acc[...] = jnp.dot(a0_ref[...], w_ref[...], preferred_element_type=F32)

    @pl.when(k == 1)
    def _():
        y = acc[...] + jnp.dot(a1_ref[...], w_ref[...], preferred_element_type=F32)
        v = DEEPNORM_ALPHA * x_ref[...] + g_ref[0] * y
        o_ref[...] = _layer_norm_rows(v, lg_ref[...], lb_ref[...])


def mm_ln(a0, a0_blk, a1, a1_blk, w, x, g, lg, lb, T, tm):
    M, D = x.shape
    tm = min(tm, T)
    per_b = T // tm
    return pl.pallas_call(
        _mm_ln_kernel,
        grid=(M // tm, 2),
        in_specs=[pl.BlockSpec((tm, D), lambda i, k: (i, a0_blk)),
                  pl.BlockSpec((tm, D), lambda i, k: (i, a1_blk)),
                  pl.BlockSpec((D, D), lambda i, k: (k, 0)),
                  pl.BlockSpec((tm, D), lambda i, k: (i, 0)),
                  pl.BlockSpec((1, 1, D), lambda i, k: (i // per_b, 0, 0)),
                  pl.BlockSpec((1, D), lambda i, k: (0, 0)),
                  pl.BlockSpec((1, D), lambda i, k: (0, 0))],
        out_specs=pl.BlockSpec((tm, D), lambda i, k: (i, 0)),
        out_shape=jax.ShapeDtypeStruct((M, D), F32),
        scratch_shapes=[pltpu.VMEM((tm, D), F32)],
        compiler_params=_params(("parallel", "arbitrary")),
        name="mm_ln",
    )(a0, a1, w, x, g, lg, lb)


def _rope_kernel(pos_ref, inv_ref, cs_ref, sn_ref):
    ang = pos_ref[...].astype(F32) * inv_ref[...]
    cs_ref[...] = jnp.cos(ang)
    sn_ref[...] = jnp.sin(ang)


def rope_tables(positions):
    B, T = positions.shape
    M = B * T
    inv = ROPE_THETA ** (-jnp.arange(0, ROPE_DIM, 2, dtype=F32) / ROPE_DIM)
    inv2 = jnp.concatenate([inv, inv]).reshape(1, ROPE_DIM)
    tm = min(1024, T)
    return pl.pallas_call(
        _rope_kernel,
        grid=(M // tm,),
        in_specs=[pl.BlockSpec((tm, 1), lambda i: (i, 0)),
                  pl.BlockSpec((1, ROPE_DIM), lambda i: (0, 0))],
        out_specs=[pl.BlockSpec((tm, ROPE_DIM), lambda i: (i, 0)),
                   pl.BlockSpec((tm, ROPE_DIM), lambda i: (i, 0))],
        out_shape=[jax.ShapeDtypeStruct((M, ROPE_DIM), F32)] * 2,
        compiler_params=_params(("parallel",)),
        name="rope_tables",
    )(positions.reshape(M, 1), inv2)


def _rms_rows(x, g, eps=1e-6):
    return x * lax.rsqrt(jnp.mean(x * x, axis=-1, keepdims=True) + eps) * g


def _mla_proj_kernel(cq_ref, ckv_ref, kr_ref, cs_ref, sn_ref, qn_ref, kvn_ref, wq_ref, wk_ref, wvt_ref,
                     q_ref, k_ref, vt_ref, cq_scr, ckv_scr, kr_scr):
    @pl.when(pl.program_id(1) == 0)
    def _():
        cq_scr[...] = _rms_rows(cq_ref[...], qn_ref[...]).astype(BF16)
        ckv_scr[...] = _rms_rows(ckv_ref[...], kvn_ref[...]).astype(BF16)
        kr = kr_ref[...]
        kr_scr[...] = kr[:, :ROPE_DIM] * cs_ref[...] + kr[:, ROPE_DIM:] * sn_ref[...]

    scale = (NOPE_DIM + ROPE_DIM) ** -0.5
    qf = jnp.dot(cq_scr[...], wq_ref[0], preferred_element_type=F32)
    q_rope = qf[:, NOPE_DIM:NOPE_DIM + ROPE_DIM] * cs_ref[...] + qf[:, NOPE_DIM + ROPE_DIM:] * sn_ref[...]
    q_ref[0, 0, :, :NOPE_DIM] = (qf[:, :NOPE_DIM] * scale).astype(BF16)
    q_ref[0, 0, :, NOPE_DIM:] = (q_rope * scale).astype(BF16)
    ckv = ckv_scr[...]
    k_ref[0, 0, :, :NOPE_DIM] = jnp.dot(ckv, wk_ref[0], preferred_element_type=F32).astype(BF16)
    k_ref[0, 0, :, NOPE_DIM:] = kr_scr[...].astype(BF16)
    vt_ref[0, 0] = lax.dot_general(wvt_ref[0], ckv, (((1,), (1,)), ((), ())),
                                   preferred_element_type=F32).astype(BF16)


def mla_proj(p, cs, sn, qn, kvn, wq, wk, wvt, B, T):
    M = B * T
    H = MLA_HEADS
    tm = min(512, T)
    per_b = T // tm
    qk_dim = NOPE_DIM + ROPE_DIM
    out_map = lambda i, h: (i // per_b, h, i % per_b, 0)
    return pl.pallas_call(
        _mla_proj_kernel,
        grid=(M // tm, H),
        in_specs=[pl.BlockSpec((tm, Q_LORA), lambda i, h: (i, 6144 // Q_LORA)),
                  pl.BlockSpec((tm, KV_LORA), lambda i, h: (i, 7680 // KV_LORA)),
                  pl.BlockSpec((tm, 2 * ROPE_DIM), lambda i, h: (i, 8192 // (2 * ROPE_DIM))),
                  pl.BlockSpec((tm, ROPE_DIM), lambda i, h: (i, 0)),
                  pl.BlockSpec((tm, ROPE_DIM), lambda i, h: (i, 0)),
                  pl.BlockSpec((1, Q_LORA), lambda i, h: (0, 0)),
                  pl.BlockSpec((1, KV_LORA), lambda i, h: (0, 0)),
                  pl.BlockSpec((1, Q_LORA, 256), lambda i, h: (h, 0, 0)),
                  pl.BlockSpec((1, KV_LORA, NOPE_DIM), lambda i, h: (h, 0, 0)),
                  pl.BlockSpec((1, V_DIM, KV_LORA), lambda i, h: (h, 0, 0))],
        out_specs=[pl.BlockSpec((1, 1, tm, qk_dim), out_map),
                   pl.BlockSpec((1, 1, tm, qk_dim), out_map),
                   pl.BlockSpec((1, 1, V_DIM, tm), lambda i, h: (i // per_b, h, 0, i % per_b))],
        out_shape=[jax.ShapeDtypeStruct((B, H, T, qk_dim), BF16),
                   jax.ShapeDtypeStruct((B, H, T, qk_dim), BF16),
                   jax.ShapeDtypeStruct((B, H, V_DIM, T), BF16)],
        scratch_shapes=[pltpu.VMEM((tm, Q_LORA), BF16), pltpu.VMEM((tm, KV_LORA), BF16),
                        pltpu.VMEM((tm, ROPE_DIM), F32)],
        compiler_params=_params(("parallel", "arbitrary")),
        name="mla_proj",
    )(p, p, p, cs, sn, qn, kvn, wq, wk, wvt)


def _attn_kernel(q_ref, k_ref, vt_ref, o_ref, *, kc):
    q = q_ref[0, 0]
    T = k_ref.shape[2]
    nk = T // kc
    ss = [lax.dot_general(k_ref[0, 0, c * kc:(c + 1) * kc, :], q, (((1,), (1,)), ((), ())),
                          preferred_element_type=F32) for c in range(nk)]
    m = jnp.max(ss[0], axis=0, keepdims=True)
    for c in range(1, nk):
        m = jnp.maximum(m, jnp.max(ss[c], axis=0, keepdims=True))
    acc = None
    l = None
    for c in range(nk):
        p = jnp.exp(ss[c] - m)
        lc = jnp.sum(p, axis=0, keepdims=True)
        ac = jnp.dot(vt_ref[0, 0, :, c * kc:(c + 1) * kc], p.astype(BF16), preferred_element_type=F32)
        l = lc if l is None else l + lc
        acc = ac if acc is None else acc + ac
    o_ref[0] = (acc / l).T.astype(o_ref.dtype)


def mla_attention(q, k, vt):
    B, H, T, dqk = q.shape
    dv = vt.shape[2]
    tq = min(512, T)
    kc = min(1024, T)
    return pl.pallas_call(
        functools.partial(_attn_kernel, kc=kc),
        grid=(B, H, T // tq),
        in_specs=[pl.BlockSpec((1, 1, tq, dqk), lambda b, h, i: (b, h, i, 0)),
                  pl.BlockSpec((1, 1, T, dqk), lambda b, h, i: (b, h, 0, 0)),
                  pl.BlockSpec((1, 1, dv, T), lambda b, h, i: (b, h, 0, 0))],
        out_specs=pl.BlockSpec((1, tq, dv), lambda b, h, i: (b, i, h)),
        out_shape=jax.ShapeDtypeStruct((B, T, H * dv), BF16),
        compiler_params=_params(("parallel", "parallel", "parallel")),
        name="mla_attn",
    )(q, k, vt)


def _shift_rows(cur, prev8, next8, d):
    tm = cur.shape[0]
    if d == 0:
        return cur
    rolled = pltpu.roll(cur, (-d) % tm, axis=0)
    row = lax.broadcasted_iota(I32, cur.shape, 0)
    if d < 0:
        for r in range(-d):
            src = prev8[8 + d + r:8 + d + r + 1, :]
            rolled = jnp.where(row == r, src, rolled)
    else:
        for r in range(d):
            src = next8[r:r + 1, :]
            rolled = jnp.where(row == tm - d + r, src, rolled)
    return rolled


def _halo_specs(tm, cw, per_b, nrow8, col_of):
    r8 = tm // 8
    cur = pl.BlockSpec((tm, cw), lambda i, j: (i, col_of(j)))
    prv = pl.BlockSpec((8, cw), lambda i, j: (jnp.maximum(i * r8 - 1, 0), col_of(j)))
    nxt = pl.BlockSpec((8, cw), lambda i, j: (jnp.minimum((i + 1) * r8, nrow8 - 1), col_of(j)))
    return cur, prv, nxt


def _halo_load(cur_ref, prev_ref, next_ref, per_b):
    ti = pl.program_id(0) % per_b
    cur = cur_ref[...].astype(F32)
    prev8 = prev_ref[...].astype(F32) * jnp.where(ti == 0, 0.0, 1.0)
    next8 = next_ref[...].astype(F32) * jnp.where(ti == per_b - 1, 0.0, 1.0)
    return cur, prev8, next8


def _pair_masks(rev):
    i = lax.broadcasted_iota(I32, (PAIR, PAIR), 0)
    j = lax.broadcasted_iota(I32, (PAIR, PAIR), 1)
    same = lax.shift_right_logical(i, LOG2_CHUNK) == lax.shift_right_logical(j, LOG2_CHUNK)
    if rev:
        strict = jnp.logical_and(same, i < j)
        incl = jnp.logical_and(same, i <= j)
    else:
        strict = jnp.logical_and(same, i > j)
        incl = jnp.logical_and(same, i >= j)
    eye_b = i == j
    same_blk = lax.shift_right_logical(i, LOG2_INV_BLOCK) == lax.shift_right_logical(j, LOG2_INV_BLOCK)
    return strict, incl, eye_b, same_blk


def _unit_tri_inverse(nmat, eye, same_blk):
    n = nmat.shape[-1]
    nd = jnp.where(same_blk, nmat, 0.0)
    noff = nmat - nd
    x = eye + nd
    p = _bmm(nd, nd)
    yield
    for s in range(1, LOG2_INV_BLOCK):
        if s < LOG2_INV_BLOCK - 1:
            xp = _bmm(jnp.concatenate([x, p], axis=1), p)
            x = x + xp[:, :n]
            p = xp[:, n:]
        else:
            x = x + _bmm(x, p)
        yield
    y = _bmm(x, noff)
    yield
    acc = x
    for _ in range(CHUNK // INV_BLOCK - 1):
        acc = x + _bmm(y, acc)
        yield
    return acc


def _interleave(n_steps, step_fn, gens, stages_per_step):
    for i in range(n_steps):
        step_fn(i)
        for _ in range(stages_per_step):
            for g in gens:
                next(g, None)
    for g in gens:
        for _ in g:
            pass


def _rows_to_cols(rows, eye_b):
    return jnp.sum(jnp.where(eye_b, rows, 0.0), axis=2, keepdims=True)


def _stack_heads(x, head0):
    return jnp.concatenate([jnp.where(head0, x, 0.0), jnp.where(head0, 0.0, x)], axis=1)


def _shift_mix_kernel(cur_ref, prev_ref, next_ref, mu_ref, o_ref, *, per_b, lora):
    cur, prev8, next8 = _halo_load(cur_ref, prev_ref, next_ref, per_b)
    prv = _shift_rows(cur, prev8, next8, -1)
    nxt = _shift_rows(cur, prev8, next8, 1)
    z = cur + (0.5 * (prv + nxt) - cur) * mu_ref[...]
    if lora:
        col = lax.broadcasted_iota(I32, z.shape, 1)
        z = jnp.where(col < 2 * LORA_PAD, jnp.tanh(z), jnp.where(col >= 4 * LORA_PAD, _sigmoid(z), z))
    o_ref[...] = z.astype(o_ref.dtype)


def shift_mix(p, mu, T, col0, width, cw, lora):
    M = p.shape[0]
    tm = min(512, T)
    per_b = T // tm
    c0 = col0 // cw
    cur, prv, nxt = _halo_specs(tm, cw, per_b, M // 8, lambda j: c0 + j)
    return pl.pallas_call(
        functools.partial(_shift_mix_kernel, per_b=per_b, lora=lora),
        grid=(M // tm, width // cw),
        in_specs=[cur, prv, nxt, pl.BlockSpec((1, cw), lambda i, j: (0, j))],
        out_specs=pl.BlockSpec((tm, cw), lambda i, j: (i, j)),
        out_shape=jax.ShapeDtypeStruct((M, width), BF16),
        compiler_params=_params(("parallel", "parallel")),
        name="rwkv_shift_mix",
    )(p, p, p, mu)


def _rwkv_prepare(g, G, d, rev, r_ref, k_ref, v_ref, la_ref, w2_ref, a2_ref, w0_ref, a0_ref, kk_ref, ka_ref,
                  p16_ref, p32_ref, bkt_ref, bon_ref):
    C = CHUNK
    rows = pl.ds(pl.multiple_of(g * (G * C), G * C), G * C)
    r = r_ref[0, rows, :].astype(F32)
    k = k_ref[0, rows, :].astype(F32)
    v = v_ref[0, rows, :].astype(F32)
    wd = la_ref[0, rows, d * LORA_PAD:(d + 1) * LORA_PAD]
    ad = la_ref[0, rows, (2 + d) * LORA_PAD:(3 + d) * LORA_PAD]
    lw = -RWKV_DECAY_SCALE * _sigmoid(w0_ref[d:d + 1, :] + jnp.dot(wd, w2_ref[d], preferred_element_type=F32))
    a_sig = _sigmoid(a0_ref[d:d + 1, :] + jnp.dot(ad, a2_ref[d], preferred_element_type=F32))

    lane = lax.broadcasted_iota(I32, (1, LANES), 1)
    head0 = lane < RWKV_N
    kx = k * kk_ref[...]
    sq = kx * kx
    ss = jnp.where(head0, jnp.sum(jnp.where(head0, sq, 0.0), axis=1, keepdims=True),
                   jnp.sum(jnp.where(head0, 0.0, sq), axis=1, keepdims=True))
    kk = kx * lax.rsqrt(ss + 1e-6)
    k_dir = k * (1.0 + (a_sig - 1.0) * ka_ref[...])
    b = kk * a_sig
    a = -kk
    bon_ref[d, rows, :] = r * k_dir
    yield

    g3 = lambda t: t.reshape(G, C, LANES)
    lw3, r3, a3, b3, kd3, v3 = g3(lw), g3(r), g3(a), g3(b), g3(k_dir), g3(v)
    ci = lax.broadcasted_iota(I32, (C, C), 0)
    cj = lax.broadcasted_iota(I32, (C, C), 1)
    tri = jnp.where((ci <= cj) if rev else (ci >= cj), 1.0, 0.0).astype(BF16)
    tri = jnp.broadcast_to(tri[None], (G, C, C))
    hi, lo = _split2(lw3)
    cum = (jnp.einsum('gij,gjk->gik', tri, hi, preferred_element_type=F32)
           + jnp.einsum('gij,gjk->gik', tri, lo, preferred_element_type=F32))
    yield
    end = 0 if rev else C - 1
    cum_end = cum[:, end:end + 1, :]
    e_neg = jnp.exp(-cum)
    e_tail = jnp.exp(cum_end - cum)
    h0 = head0.reshape(1, 1, LANES)
    a2s = _stack_heads(a3 * jnp.exp(cum - lw3), h0)
    r2s = _stack_heads(r3 * jnp.exp(cum), h0)
    k2s = _stack_heads(kd3 * e_neg, h0)
    b2s = _stack_heads(b3 * e_neg, h0)
    v2s = _stack_heads(v3, h0)
    kh2 = _stack_heads(kd3 * e_tail, h0)
    bh2 = _stack_heads(b3 * e_tail, h0)

    strict, incl, eye_b, same_blk = _pair_masks(rev)
    eye = jnp.where(eye_b, 1.0, 0.0)
    gram = _bmm_nt(jnp.concatenate([a2s, r2s], axis=1), jnp.concatenate([k2s, b2s], axis=1))
    a_ak = jnp.where(strict, gram[:, :PAIR, :PAIR], 0.0)
    a_ab = jnp.where(strict, gram[:, :PAIR, PAIR:], 0.0)
    a_rk = jnp.where(incl, gram[:, PAIR:, :PAIR], 0.0)
    a_rb = jnp.where(incl, gram[:, PAIR:, PAIR:], 0.0).astype(BF16)
    p16_ref[d, :, 2 * PAIR:4 * PAIR, :] = jnp.concatenate([v2s.astype(BF16), a_rb], axis=1)
    bkt_ref[d] = jnp.swapaxes(jnp.concatenate([bh2, kh2], axis=1), 1, 2).astype(BF16)
    p32_ref[d, :, 2 * PAIR:3 * PAIR, :] = jnp.broadcast_to(_rows_to_cols(jnp.exp(cum_end), eye_b), (G, PAIR, LANES))
    yield
    kv = _bmm(jnp.concatenate([a_ak, a_rk], axis=1), v2s)
    p32_ref[d, :, PAIR:2 * PAIR, :] = kv[:, PAIR:]
    yield
    tinv = yield from _unit_tri_inverse(a_ab, eye, same_blk)
    tt = _bmm(tinv, jnp.concatenate([a2s, kv[:, :PAIR]], axis=2))
    p16_ref[d, :, 0:2 * PAIR, :] = jnp.concatenate([tt[:, :, :LANES], r2s], axis=1).astype(BF16)
    p32_ref[d, :, 0:PAIR, :] = tt[:, :, LANES:]


RWKV_P16_ROWS = 4 * PAIR
RWKV_P32_ROWS = 3 * PAIR
RWKV_PREP_STAGES = 14


def _rwkv_step(H, p16_ref, p32_ref, bkt_ref, d, i):
    C = CHUNK
    lhs = p16_ref[d, i, 0:2 * PAIR, :]
    v2 = p16_ref[d, i, 2 * PAIR:3 * PAIR, :]
    a_rb = p16_ref[d, i, 3 * PAIR:4 * PAIR, :]
    sr = _dot(lhs, H)
    u2 = sr[:PAIR] + p32_ref[d, i, 0:PAIR, :]
    y2 = sr[PAIR:] + p32_ref[d, i, PAIR:2 * PAIR, :] + _dot(a_rb, u2)
    y = y2[:C] + y2[C:]
    upd = _dot(bkt_ref[d, i], jnp.concatenate([u2.astype(BF16), v2], axis=0))
    return H * p32_ref[d, i, 2 * PAIR:3 * PAIR, :] + upd, y


def _rwkv_scan_kernel(r_ref, k_ref, v_ref, la_ref, w2_ref, a2_ref, g2_ref, w0_ref, a0_ref,
                      kk_ref, ka_ref, rk_ref, gng_ref, gnb_ref, o_ref, yf_scr, yb_scr, bon_scr,
                      a16, a32, akt, b16, b32, bkt, *, nC, G):
    C = CHUNK
    nG = nC // G
    GC = G * C
    refs = (r_ref, k_ref, v_ref, la_ref, w2_ref, a2_ref, w0_ref, a0_ref, kk_ref, ka_ref)

    def prepare(g, bufs):
        return [_rwkv_prepare(g, G, 0, False, *refs, *bufs, bon_scr),
                _rwkv_prepare(nG - 1 - g, G, 1, True, *refs, *bufs, bon_scr)]

    def run(g, carry, cur, nxt_g, nxt):
        state = list(carry)
        gb = nG - 1 - g

        def step(i):
            ib = G - 1 - i
            state[0], yf = _rwkv_step(state[0], *cur, 0, i)
            state[1], yb = _rwkv_step(state[1], *cur, 1, ib)
            yf_scr[pl.ds(pl.multiple_of(g * GC + i * C, C), C), :] = yf
            yb_scr[pl.ds(pl.multiple_of(gb * GC + ib * C, C), C), :] = yb

        _interleave(G, step, prepare(nxt_g, nxt), RWKV_PREP_STAGES // G + 1)
        return tuple(state)

    bufs_a, bufs_b = (a16, a32, akt), (b16, b32, bkt)
    _interleave(0, None, prepare(0, bufs_a), 0)

    def body(h, carry):
        g = 2 * h
        carry = run(g, carry, bufs_a, g + 1, bufs_b)
        return run(g + 1, carry, bufs_b, jnp.minimum(g + 2, nG - 1), bufs_a)

    z = jnp.zeros((LANES, LANES), F32)
    lax.fori_loop(0, nG // 2, body, (z, z))

    def fin(g, _):
        sl = pl.ds(pl.multiple_of(g * GC, GC), GC)
        y = yf_scr[sl, :] + yb_scr[sl, :]
        lane = lax.broadcasted_iota(I32, (1, LANES), 1)
        head0 = lane < RWKV_N

        def seg_sum(x):
            s0 = jnp.sum(jnp.where(head0, x, 0.0), axis=1, keepdims=True)
            s1 = jnp.sum(jnp.where(head0, 0.0, x), axis=1, keepdims=True)
            return jnp.where(head0, s0, s1)

        mu = seg_sum(y) * (1.0 / RWKV_N)
        dlt = y - mu
        var = seg_sum(dlt * dlt) * (1.0 / RWKV_N)
        yn = dlt * lax.rsqrt(var + RWKV_GN_EPS) * gng_ref[...] + gnb_ref[...]
        bsum = seg_sum((bon_scr[0, sl, :] + bon_scr[1, sl, :]) * rk_ref[...])
        v = v_ref[0, sl, :].astype(F32)
        gate = jnp.dot(la_ref[0, sl, 4 * LORA_PAD:], g2_ref[...], preferred_element_type=F32)
        o_ref[0, sl, :] = ((yn + bsum * v) * gate).astype(o_ref.dtype)
        return 0

    lax.fori_loop(0, nG, fin, 0)


def rwkv_scan(z_rkv, la, w2, a2, g2, w0, a0, k_k, k_a, r_k, gn_g, gn_b, B, T):
    nC = T // CHUNK
    G = min(GROUP, nC // 2)
    assert (nC // G) % 2 == 0
    npair = RWKV_WIDTH // LANES
    nl = la.shape[-1]
    vec = lambda: pl.BlockSpec((1, LANES), lambda b, p: (0, p))
    return pl.pallas_call(
        functools.partial(_rwkv_scan_kernel, nC=nC, G=G),
        grid=(B, npair),
        in_specs=[pl.BlockSpec((1, T, LANES), lambda b, p: (b, 0, p)),
                  pl.BlockSpec((1, T, LANES), lambda b, p: (b, 0, npair + p)),
                  pl.BlockSpec((1, T, LANES), lambda b, p: (b, 0, 2 * npair + p)),
                  pl.BlockSpec((1, T, nl), lambda b, p: (b, 0, 0)),
                  pl.BlockSpec((2, LORA_PAD, LANES), lambda b, p: (0, 0, p)),
                  pl.BlockSpec((2, LORA_PAD, LANES), lambda b, p: (0, 0, p)),
                  pl.BlockSpec((GATE_LORA, LANES), lambda b, p: (0, p)),
                  pl.BlockSpec((2, LANES), lambda b, p: (0, p)),
                  pl.BlockSpec((2, LANES), lambda b, p: (0, p)),
                  vec(), vec(), vec(), vec(), vec()],
        out_specs=pl.BlockSpec((1, T, LANES), lambda b, p: (b, 0, p)),
        out_shape=jax.ShapeDtypeStruct((B, T, RWKV_WIDTH), BF16),
        scratch_shapes=[pltpu.VMEM((T, LANES), F32), pltpu.VMEM((T, LANES), F32),
                        pltpu.VMEM((2, T, LANES), F32)]
        + 2 * [pltpu.VMEM((2, G, RWKV_P16_ROWS, LANES), BF16), pltpu.VMEM((2, G, RWKV_P32_ROWS, LANES), F32),
               pltpu.VMEM((2, G, LANES, 2 * PAIR), BF16)],
        compiler_params=_params(("parallel", "arbitrary")),
        name="rwkv_scan",
    )(z_rkv, z_rkv, z_rkv, la, w2, a2, g2, w0, a0, k_k, k_a, r_k, gn_g, gn_b)


def _gdn_conv_kernel(cur_ref, prev_ref, next_ref, w_ref, o_ref, *, per_b, q_blocks, k_blocks):
    cur, prev8, next8 = _halo_load(cur_ref, prev_ref, next_ref, per_b)
    half = CONV_WIDTH // 2
    acc = jnp.zeros_like(cur)
    for i in range(CONV_WIDTH):
        acc = acc + _shift_rows(cur, prev8, next8, i - half) * w_ref[i:i + 1, :]
    y = _silu(acc)
    j = pl.program_id(1)
    cw = y.shape[1]
    outs = []
    for s in range(cw // GDN_D):
        ys = y[:, s * GDN_D:(s + 1) * GDN_D]
        nrm = lax.rsqrt(jnp.sum(ys * ys, axis=1, keepdims=True) + 1e-6)
        outs.append(ys * nrm)
    yn = jnp.concatenate(outs, axis=1) if len(outs) > 1 else outs[0]
    res = jnp.where(j < q_blocks, yn * (GDN_D ** -0.5), jnp.where(j < k_blocks, yn, y))
    o_ref[...] = res.astype(o_ref.dtype)


def gdn_conv_op(p, conv_w, T):
    M = p.shape[0]
    tm = min(512, T)
    cw = 512
    per_b = T // tm
    cur, prv, nxt = _halo_specs(tm, cw, per_b, M // 8, lambda j: j)
    return pl.pallas_call(
        functools.partial(_gdn_conv_kernel, per_b=per_b, q_blocks=GDN_KEY_WIDTH // cw,
                          k_blocks=2 * GDN_KEY_WIDTH // cw),
        grid=(M // tm, GDN_QKV // cw),
        in_specs=[cur, prv, nxt, pl.BlockSpec((CONV_WIDTH, cw), lambda i, j: (0, j))],
        out_specs=pl.BlockSpec((tm, cw), lambda i, j: (i, j)),
        out_shape=jax.ShapeDtypeStruct((M, GDN_QKV), BF16),
        compiler_params=_params(("parallel", "parallel")),
        name="gdn_conv",
    )(p, p, p, conv_w)


def _gdn_gate_kernel(ba_ref, alog_ref, dtb_ref, o_ref):
    C = CHUNK
    x = ba_ref[...]
    tm = x.shape[0]
    nh = 2 * GDN_V_HEADS
    beta = _sigmoid(x)
    a = x + dtb_ref[...]
    sp = jnp.maximum(a, 0.0) + jnp.log1p(jnp.exp(-jnp.abs(a)))
    g = -jnp.exp(alog_ref[...]) * sp
    i = lax.broadcasted_iota(I32, (C, C), 0)
    j = lax.broadcasted_iota(I32, (C, C), 1)
    lower = jnp.where(i >= j, 1.0, 0.0).astype(BF16)
    upper = jnp.where(i <= j, 1.0, 0.0).astype(BF16)
    col = lax.broadcasted_iota(I32, (C, 2 * nh), 1)
    for c in range(tm // C):
        gc = g[c * C:(c + 1) * C]
        hi, mid, lo = _split3(gc)
        fw = (jnp.dot(lower, hi, preferred_element_type=F32) + jnp.dot(lower, mid, preferred_element_type=F32)
              + jnp.dot(lower, lo, preferred_element_type=F32))
        bw = (jnp.dot(upper, hi, preferred_element_type=F32) + jnp.dot(upper, mid, preferred_element_type=F32)
              + jnp.dot(upper, lo, preferred_element_type=F32))
        o_ref[c * C:(c + 1) * C, :] = jnp.where(col < nh, beta[c * C:(c + 1) * C],
                                                jnp.where(col < nh + GDN_V_HEADS, fw, bw))


def gdn_gates(ba, a_log, dt_bias, T):
    M = ba.shape[0]
    tm = min(512, T)
    nh = 2 * GDN_V_HEADS
    return pl.pallas_call(
        _gdn_gate_kernel,
        grid=(M // tm,),
        in_specs=[pl.BlockSpec((tm, 2 * nh), lambda i: (i, 0)),
                  pl.BlockSpec((1, 2 * nh), lambda i: (0, 0)),
                  pl.BlockSpec((1, 2 * nh), lambda i: (0, 0))],
        out_specs=pl.BlockSpec((tm, 2 * nh), lambda i: (i, 0)),
        out_shape=jax.ShapeDtypeStruct((M, 2 * nh), F32),
        compiler_params=_params(("parallel",)),
        name="gdn_gates",
    )(ba, a_log, dt_bias)


def _gdn_prepare(g, G, d, rev, kh, q_ref, k_ref, v_ref, bg_ref, p16_ref, u_ref, dec_ref):
    C = CHUNK
    rows = pl.ds(pl.multiple_of(g * (G * C), G * C), G * C)
    q = q_ref[0, rows, :].astype(F32).reshape(G, C, GDN_D)
    k = k_ref[0, rows, :].astype(F32).reshape(G, C, GDN_D)
    v = v_ref[0, rows, :].astype(F32).reshape(G, C, 2 * GDN_D)
    v2 = jnp.concatenate([v[:, :, :GDN_D], v[:, :, GDN_D:]], axis=1)
    k2 = jnp.concatenate([k, k], axis=1)
    q2 = jnp.concatenate([q, q], axis=1)
    chunks = pl.ds(g * G, G)
    beta_row = bg_ref[0, chunks, pl.ds(d * GDN_K_HEADS + kh, 1), :]
    gc_row = bg_ref[0, chunks, pl.ds((2 + d) * GDN_K_HEADS + kh, 1), :]
    strict, incl, eye_b, same_blk = _pair_masks(rev)
    eye = jnp.where(eye_b, 1.0, 0.0)
    beta_col = _rows_to_cols(beta_row, eye_b)
    gc_col = _rows_to_cols(gc_row, eye_b)
    decay = jnp.where(incl, jnp.exp(jnp.where(incl, gc_col - gc_row, 0.0)), 0.0)
    gram = _bmm_nt(jnp.concatenate([k2, q2], axis=1), k2)
    lower = jnp.where(strict, gram[:, :PAIR] * beta_col * decay, 0.0)
    eg = jnp.exp(gc_col)
    end = 0 if rev else C - 1
    gl0 = gc_row[:, :, end:end + 1]
    gl1 = gc_row[:, :, C + end:C + end + 1]
    rowi = lax.broadcasted_iota(I32, (1, PAIR, 1), 1)
    gl_col = jnp.where(rowi < C, gl0, gl1)
    lane2 = lax.broadcasted_iota(I32, (1, 1, 2 * GDN_D), 2)
    decay_s = jnp.where(lane2 < GDN_D, jnp.exp(gl0), jnp.exp(gl1))
    p16_ref[d, :, PAIR:2 * PAIR, :] = (q2 * eg).astype(BF16)
    p16_ref[d, :, 2 * PAIR:3 * PAIR, :] = (gram[:, PAIR:] * decay).astype(BF16)
    p16_ref[d, :, 3 * PAIR:4 * PAIR, :] = jnp.swapaxes(k2 * jnp.exp(gl_col - gc_col), 1, 2).astype(BF16)
    dec_ref[d] = jnp.broadcast_to(decay_s, (G, 8, 2 * GDN_D))
    yield
    t_inv = yield from _unit_tri_inverse(-lower, eye, same_blk)
    kb = k2 * beta_col
    uw = _bmm(t_inv, jnp.concatenate([v2 * beta_col, kb * eg], axis=2))
    u_ref[d] = uw[:, :, :GDN_D]
    p16_ref[d, :, 0:PAIR, :] = uw[:, :, GDN_D:].astype(BF16)


GDN_P16_ROWS = 4 * PAIR
GDN_PREP_STAGES = 10


def _gdn_step(S, p16_ref, u_ref, dec_ref, d, i):
    C = CHUNK
    ws = _dot(p16_ref[d, i, 0:2 * PAIR, :], S)
    head0_rows = lax.broadcasted_iota(I32, (PAIR, 1), 0) < C
    w_s = jnp.where(head0_rows, ws[:PAIR, :GDN_D], ws[:PAIR, GDN_D:])
    q_s = jnp.where(head0_rows, ws[PAIR:, :GDN_D], ws[PAIR:, GDN_D:])
    v_new = u_ref[d, i] - w_s
    o2 = q_s + _dot(p16_ref[d, i, 2 * PAIR:3 * PAIR, :], v_new)
    vcat = jnp.concatenate([jnp.where(head0_rows, v_new, 0.0), jnp.where(head0_rows, 0.0, v_new)], axis=1)
    S_new = S * dec_ref[d, i, 0:1, :] + _dot(p16_ref[d, i, 3 * PAIR:4 * PAIR, :], vcat)
    return S_new, jnp.concatenate([o2[:C], o2[C:]], axis=1)


def _gdn_scan_kernel(q_ref, k_ref, v_ref, z_ref, bg_ref, ng_ref, o_ref, of_scr, ob_scr,
                     a16, au, ad, b16, bu, bd, *, nC, G):
    C = CHUNK
    nG = nC // G
    GC = G * C
    kh = pl.program_id(1)

    def prepare(g, bufs):
        return [_gdn_prepare(g, G, 0, False, kh, q_ref, k_ref, v_ref, bg_ref, *bufs),
                _gdn_prepare(nG - 1 - g, G, 1, True, kh, q_ref, k_ref, v_ref, bg_ref, *bufs)]

    def run(g, carry, cur, nxt_g, nxt):
        state = list(carry)
        gb = nG - 1 - g

        def step(i):
            ib = G - 1 - i
            state[0], of = _gdn_step(state[0], *cur, 0, i)
            state[1], ob = _gdn_step(state[1], *cur, 1, ib)
            of_scr[pl.ds(pl.multiple_of(g * GC + i * C, C), C), :] = of
            ob_scr[pl.ds(pl.multiple_of(gb * GC + ib * C, C), C), :] = ob

        _interleave(G, step, prepare(nxt_g, nxt), GDN_PREP_STAGES // G + 1)
        return tuple(state)

    bufs_a, bufs_b = (a16, au, ad), (b16, bu, bd)
    _interleave(0, None, prepare(0, bufs_a), 0)

    def body(h, carry):
        g = 2 * h
        carry = run(g, carry, bufs_a, g + 1, bufs_b)
        return run(g + 1, carry, bufs_b, jnp.minimum(g + 2, nG - 1), bufs_a)

    z0 = jnp.zeros((GDN_D, 2 * GDN_D), F32)
    lax.fori_loop(0, nG // 2, body, (z0, z0))

    def fin(g, _):
        sl = pl.ds(pl.multiple_of(g * GC, GC), GC)
        o = of_scr[sl, :] + ob_scr[sl, :]
        z = z_ref[0, sl, :].astype(F32)
        for s in range(2):
            os_ = o[:, s * GDN_D:(s + 1) * GDN_D]
            on = os_ * lax.rsqrt(jnp.mean(os_ * os_, axis=-1, keepdims=True) + 1e-6) * ng_ref[...]
            o_ref[0, sl, s * GDN_D:(s + 1) * GDN_D] = (on * _silu(z[:, s * GDN_D:(s + 1) * GDN_D])).astype(o_ref.dtype)
        return 0

    lax.fori_loop(0, nG, fin, 0)


def gdn_scan(qkv, z_src, z_col0, bg, norm_g, B, T):
    nC = T // CHUNK
    G = min(GROUP, nC // 2)
    assert (nC // G) % 2 == 0
    kb = GDN_KEY_WIDTH // GDN_D
    vb = 2 * GDN_KEY_WIDTH // (2 * GDN_D)
    return pl.pallas_call(
        functools.partial(_gdn_scan_kernel, nC=nC, G=G),
        grid=(B, GDN_K_HEADS),
        in_specs=[pl.BlockSpec((1, T, GDN_D), lambda b, h: (b, 0, h)),
                  pl.BlockSpec((1, T, GDN_D), lambda b, h: (b, 0, kb + h)),
                  pl.BlockSpec((1, T, 2 * GDN_D), lambda b, h: (b, 0, vb + h)),
                  pl.BlockSpec((1, T, 2 * GDN_D), lambda b, h: (b, 0, z_col0 + h)),
                  pl.BlockSpec((1, nC, 4 * GDN_K_HEADS, PAIR), lambda b, h: (b, 0, 0, 0)),
                  pl.BlockSpec((1, GDN_D), lambda b, h: (0, 0))],
        out_specs=pl.BlockSpec((1, T, 2 * GDN_D), lambda b, h: (b, 0, h)),
        out_shape=jax.ShapeDtypeStruct((B, T, GDN_VAL_WIDTH), BF16),
        scratch_shapes=[pltpu.VMEM((T, 2 * GDN_D), F32), pltpu.VMEM((T, 2 * GDN_D), F32)]
        + 2 * [pltpu.VMEM((2, G, GDN_P16_ROWS, GDN_D), BF16), pltpu.VMEM((2, G, PAIR, GDN_D), F32),
               pltpu.VMEM((2, G, 8, 2 * GDN_D), F32)],
        compiler_params=_params(("parallel", "arbitrary")),
        name="gdn_scan",
    )(qkv, qkv, qkv, z_src, bg, norm_g)


def _router_kernel(x_ref, sc_ref, sh_ref, wr_ref, h_ref, aff_ref):
    h = x_ref[...] * (1.0 + sc_ref[0]) + sh_ref[0]
    h_ref[...] = h.astype(BF16)
    hh, hm, hl = _split3(h)
    wh, wm, wl = _split3(wr_ref[...])
    nt = lambda a, b: lax.dot_general(a, b, (((1,), (1,)), ((), ())), preferred_element_type=F32)
    logits = (nt(wh, hh) + (nt(wh, hm) + nt(wm, hh)) + (nt(wh, hl) + nt(wl, hh) + nt(wm, hm)))
    m = jnp.max(logits, axis=0, keepdims=True)
    e = jnp.exp(logits - m)
    aff_ref[0] = e / jnp.sum(e, axis=0, keepdims=True)


def moe_router_op(x, sc, sh, wr_t, B, T):
    M, D = x.shape
    E = wr_t.shape[0]
    tm = min(512, T)
    per_b = T // tm
    return pl.pallas_call(
        _router_kernel,
        grid=(M // tm,),
        in_specs=[pl.BlockSpec((tm, D), lambda i: (i, 0)),
                  pl.BlockSpec((1, 1, D), lambda i: (i // per_b, 0, 0)),
                  pl.BlockSpec((1, 1, D), lambda i: (i // per_b, 0, 0)),
                  pl.BlockSpec((E, D), lambda i: (0, 0))],
        out_specs=[pl.BlockSpec((tm, D), lambda i: (i, 0)),
                   pl.BlockSpec((1, E, tm), lambda i: (i // per_b, 0, i % per_b))],
        out_shape=[jax.ShapeDtypeStruct((M, D), BF16), jax.ShapeDtypeStruct((B, E, T), F32)],
        compiler_params=_params(("parallel",)),
        name="moe_router",
    )(x, sc, sh, wr_t)


def _lane_cumsum(m01, blk):
    E, T = m01.shape
    i = lax.broadcasted_iota(I32, (blk, blk), 0)
    j = lax.broadcasted_iota(I32, (blk, blk), 1)
    upper = jnp.where(i <= j, 1.0, 0.0).astype(BF16)
    carry = jnp.zeros((E, 1), F32)
    parts = []
    for c in range(T // blk):
        seg = m01[:, c * blk:(c + 1) * blk]
        cs = jnp.dot(seg.astype(BF16), upper, preferred_element_type=F32) + carry
        parts.append(cs)
        carry = cs[:, blk - 1:blk]
    return jnp.concatenate(parts, axis=1) if len(parts) > 1 else parts[0]


def _topk_kernel(aff_ref, pos_ref, los_ref, *, cap, ts):
    a = aff_ref[0]
    E, T = a.shape
    bits = pltpu.bitcast(a, I32)

    def body(i, t):
        cand = t | lax.shift_left(jnp.int32(1), 30 - i)
        cnt = jnp.sum((bits >= cand).astype(I32), axis=1, keepdims=True)
        return jnp.where(cnt >= cap, cand, t)

    thr = lax.fori_loop(0, 31, body, jnp.zeros((E, 1), I32))
    gt = bits > thr
    eq = bits == thr
    need = cap - jnp.sum(gt.astype(I32), axis=1, keepdims=True)
    blk = min(512, T)
    eq01 = jnp.where(eq, 1.0, 0.0)
    eq_rank = _lane_cumsum(eq01, blk) - eq01
    sel = jnp.logical_or(gt, jnp.logical_and(eq, eq_rank < need.astype(F32)))
    sel01 = jnp.where(sel, 1.0, 0.0)
    slot = _lane_cumsum(sel01, blk) - sel01
    pos_ref[0] = jnp.where(sel, slot.astype(I32), -1)
    ti = lax.broadcasted_iota(I32, (T, LANES), 0)
    wi = lax.broadcasted_iota(I32, (T, LANES), 1)
    before = jnp.where(ti < wi * ts, 1.0, 0.0).astype(BF16)
    los_ref[0] = jnp.dot(sel01.astype(BF16), before, preferred_element_type=F32).astype(I32)


def moe_topk(aff_t, cap, ts):
    B, E, T = aff_t.shape
    return pl.pallas_call(
        functools.partial(_topk_kernel, cap=cap, ts=ts),
        grid=(B,),
        in_specs=[pl.BlockSpec((1, E, T), lambda b: (b, 0, 0))],
        out_specs=[pl.BlockSpec((1, E, T), lambda b: (b, 0, 0)),
                   pl.BlockSpec((1, E, LANES), lambda b: (b, 0, 0))],
        out_shape=[jax.ShapeDtypeStruct((B, E, T), I32), jax.ShapeDtypeStruct((B, E, LANES), I32)],
        compiler_params=_params(("parallel",)),
        name="moe_topk",
    )(aff_t)


MXU_DEPTH = 256


def _slot_windows(cap, win):
    win = min(win, cap)
    return win, cap // win


def _gather_kernel(los_ref, pos_ref, h_ref, o_ref, acc, *, cap, nt):
    b, e, c = pl.program_id(0), pl.program_id(1), pl.program_id(2)
    base = (b * pl.num_programs(1) + e) * (nt + 1) + c
    lo = los_ref[base]
    hi = los_ref[base + 1]
    tc = h_ref.shape[0]
    win, nwin = _slot_windows(cap, LANES)

    @pl.when(c == 0)
    def _():
        acc[...] = jnp.zeros(acc.shape, F32)

    for w in range(nwin):
        @pl.when(jnp.logical_and(lo < (w + 1) * win, hi > w * win))
        def _():
            slot = lax.broadcasted_iota(I32, (win, tc), 0) + w * win
            onehot = jnp.where(slot == pos_ref[0, 0], 1.0, 0.0).astype(BF16)
            acc[w * win:(w + 1) * win, :] += jnp.dot(onehot, h_ref[...], preferred_element_type=F32)

    @pl.when(c == pl.num_programs(2) - 1)
    def _():
        o_ref[0, 0] = acc[...].astype(o_ref.dtype)


def moe_gather(los, pos4, h, B, T, cap, ts):
    E = pos4.shape[1]
    D = h.shape[1]
    nt = T // ts
    return pl.pallas_call(
        functools.partial(_gather_kernel, cap=cap, nt=nt),
        grid_spec=pltpu.PrefetchScalarGridSpec(
            num_scalar_prefetch=1,
            grid=(B, E, nt),
            in_specs=[pl.BlockSpec((1, 1, 1, ts), lambda b, e, c, s: (b, e, 0, c)),
                      pl.BlockSpec((ts, D), lambda b, e, c, s: (b * nt + c, 0))],
            out_specs=pl.BlockSpec((1, 1, cap, D), lambda b, e, c, s: (b, e, 0, 0)),
            scratch_shapes=[pltpu.VMEM((cap, D), F32)]),
        out_shape=jax.ShapeDtypeStruct((B, E, cap, D), BF16),
        compiler_params=_params(("parallel", "parallel", "arbitrary")),
        name="moe_gather",
    )(los, pos4, h)


def _ffn_kernel(x_ref, wg_ref, wu_ref, wd_ref, o_ref):
    x = x_ref[0, 0]
    g = jnp.dot(x, wg_ref[0], preferred_element_type=F32)
    u = jnp.dot(x, wu_ref[0], preferred_element_type=F32)
    hid = (_silu(g) * u).astype(BF16)
    o_ref[0, 0] = jnp.dot(hid, wd_ref[0], preferred_element_type=F32).astype(o_ref.dtype)


def moe_ffn(xs, wg, wu, wd):
    B, E, cap, D = xs.shape
    Fh = wg.shape[-1]
    return pl.pallas_call(
        _ffn_kernel,
        grid=(E, B),
        in_specs=[pl.BlockSpec((1, 1, cap, D), lambda e, b: (b, e, 0, 0)),
                  pl.BlockSpec((1, D, Fh), lambda e, b: (e, 0, 0)),
                  pl.BlockSpec((1, D, Fh), lambda e, b: (e, 0, 0)),
                  pl.BlockSpec((1, Fh, D), lambda e, b: (e, 0, 0))],
        out_specs=pl.BlockSpec((1, 1, cap, D), lambda e, b: (b, e, 0, 0)),
        out_shape=jax.ShapeDtypeStruct((B, E, cap, D), BF16),
        compiler_params=_params(("parallel", "arbitrary")),
        name="moe_ffn",
    )(xs, wg, wu, wd)


def _combine_kernel(los_ref, pos_ref, gate_ref, y_ref, x_ref, g_ref, lg_ref, lb_ref, o_ref, acc, *, cap, nt):
    b, i, e = pl.program_id(0), pl.program_id(1), pl.program_id(2)
    base = (b * pl.num_programs(2) + e) * (nt + 1) + i
    lo = los_ref[base]
    hi = los_ref[base + 1]
    pos = pos_ref[0].astype(F32)
    gate = gate_ref[0]
    tt, E = pos.shape
    lane = lax.broadcasted_iota(I32, (tt, E), 1)
    pcol = jnp.sum(jnp.where(lane == e, pos, 0.0), axis=1, keepdims=True)
    gcol = jnp.sum(jnp.where(lane == e, gate, 0.0), axis=1, keepdims=True)
    win, nwin = _slot_windows(cap, MXU_DEPTH)

    @pl.when(e == 0)
    def _():
        acc[...] = jnp.zeros(acc.shape, F32)

    for w in range(nwin):
        @pl.when(jnp.logical_and(lo < (w + 1) * win, hi > w * win))
        def _():
            slot = (lax.broadcasted_iota(I32, (tt, win), 1) + w * win).astype(F32)
            onehot = jnp.where(slot == pcol, 1.0, 0.0).astype(BF16)
            acc[...] += gcol * jnp.dot(onehot, y_ref[0, 0, w * win:(w + 1) * win, :], preferred_element_type=F32)

    @pl.when(e == pl.num_programs(2) - 1)
    def _():
        v = DEEPNORM_ALPHA * x_ref[...] + g_ref[0] * acc[...]
        o_ref[...] = _layer_norm_rows(v, lg_ref[...], lb_ref[...])


def moe_combine(los, pos_tm, aff_tm, yd, x, g, lg, lb, B, T, cap, ts):
    E = pos_tm.shape[-1]
    M, D = x.shape
    nt = T // ts
    return pl.pallas_call(
        functools.partial(_combine_kernel, cap=cap, nt=nt),
        grid_spec=pltpu.PrefetchScalarGridSpec(
            num_scalar_prefetch=1,
            grid=(B, nt, E),
            in_specs=[pl.BlockSpec((1, ts, E), lambda b, i, e, s: (b, i, 0)),
                      pl.BlockSpec((1, ts, E), lambda b, i, e, s: (b, i, 0)),
                      pl.BlockSpec((1, 1, cap, D), lambda b, i, e, s: (b, e, 0, 0)),
                      pl.BlockSpec((ts, D), lambda b, i, e, s: (b * nt + i, 0)),
                      pl.BlockSpec((1, 1, D), lambda b, i, e, s: (b, 0, 0)),
                      pl.BlockSpec((1, D), lambda b, i, e, s: (0, 0)),
                      pl.BlockSpec((1, D), lambda b, i, e, s: (0, 0))],
            out_specs=pl.BlockSpec((ts, D), lambda b, i, e, s: (b * nt + i, 0)),
            scratch_shapes=[pltpu.VMEM((ts, D), F32)]),
        out_shape=jax.ShapeDtypeStruct((M, D), F32),
        compiler_params=_params(("parallel", "parallel", "arbitrary")),
        name="moe_combine",
    )(los, pos_tm, aff_tm, yd, x, g, lg, lb)


def moe_layer(x, sc, sh, g, lg, lb, w_router, w_gate, w_up, w_down, B, T):
    cap = CAPACITY_FACTOR * T // N_EXPERTS
    ts = min(512, T)
    nt = T // ts
    h, aff_t = moe_router_op(x, sc, sh, w_router.T, B, T)
    pos, los = moe_topk(aff_t, cap, ts)
    los = los[:, :, :nt + 1].reshape(-1)
    xs = moe_gather(los, pos[:, :, None, :], h, B, T, cap, ts)
    yd = moe_ffn(xs, w_gate.astype(BF16), w_up.astype(BF16), w_down.astype(BF16))
    pos_tm = jnp.swapaxes(pos, 1, 2)
    aff_tm = jnp.swapaxes(aff_t, 1, 2)
    return moe_combine(los, pos_tm, aff_tm, yd, x, g, lg, lb, B, T, cap, ts)


def _pad_cols(w, n):
    return jnp.pad(w, ((0, 0), (0, n - w.shape[1])))


def _rot_cols(w):
    half = w.shape[-1] // 2
    return jnp.concatenate([-w[..., half:], w[..., :half]], axis=-1)


def _even_w_in(w):
    D = w.shape[0]
    o = MLA_IN
    rkv = w[:, o:o + 3 * RWKV_WIDTH]
    cq = w[:, :Q_LORA]
    ckv = w[:, Q_LORA:Q_LORA + KV_LORA]
    kr = w[:, Q_LORA + KV_LORA:MLA_IN]
    l0 = o + 3 * RWKV_WIDTH
    lora = [_pad_cols(w[:, l0 + i * DECAY_LORA:l0 + (i + 1) * DECAY_LORA], LORA_PAD) for i in range(4)]
    gd = w[:, l0 + 4 * DECAY_LORA:]
    pad = jnp.zeros((D, LANES), w.dtype)
    out = jnp.concatenate([rkv, cq, ckv, kr, _rot_cols(kr), pad] + lora + [gd], axis=1)
    assert out.shape[1] == EVEN_COLS
    return out.astype(BF16)


def _even_mu(mu):
    rkv = mu[:3 * RWKV_WIDTH]
    l0 = 3 * RWKV_WIDTH
    lora = [jnp.pad(mu[l0 + i * DECAY_LORA:l0 + (i + 1) * DECAY_LORA], (0, LORA_PAD - DECAY_LORA)) for i in range(4)]
    gd = mu[l0 + 4 * DECAY_LORA:]
    return rkv.reshape(1, -1), jnp.concatenate(lora + [gd]).reshape(1, -1)


def _pad_rows(w, n):
    return jnp.pad(w, ((0, 0), (0, n - w.shape[1]), (0, 0)))


def even_mixer(x, sc, sh, positions_tables, w_in, shift_mu, q_norm, w_uq, kv_norm, w_ukv, w0, w2, a0, a2, g2,
               k_k, k_a, r_k, gn_g, gn_b, B, T):
    cs, sn = positions_tables
    H = MLA_HEADS
    p = mm_mod(x, sc, sh, _even_w_in(w_in), T, 512, 1024, F32)
    wq = w_uq.reshape(Q_LORA, H, NOPE_DIM + ROPE_DIM)
    wq = jnp.concatenate([wq, _rot_cols(wq[..., NOPE_DIM:])], axis=-1)
    wq = jnp.swapaxes(wq, 0, 1).astype(BF16)
    wkv = w_ukv.reshape(KV_LORA, H, NOPE_DIM + V_DIM)
    wk = jnp.swapaxes(wkv[..., :NOPE_DIM], 0, 1).astype(BF16)
    wvt = jnp.transpose(wkv[..., NOPE_DIM:], (1, 2, 0)).astype(BF16)
    q, k, vt = mla_proj(p, cs, sn, q_norm.reshape(1, -1), kv_norm.reshape(1, -1), wq, wk, wvt, B, T)
    attn = mla_attention(q, k, vt)
    mu_rkv, mu_lora = _even_mu(shift_mu)
    z_rkv = shift_mix(p, mu_rkv, T, 0, 3 * RWKV_WIDTH, 1024, False)
    la = shift_mix(p, mu_lora, T, 8448, 768, 768, True)
    y_rwkv = rwkv_scan(z_rkv.reshape(B, T, -1), la.reshape(B, T, -1),
                       _pad_rows(w2, LORA_PAD).astype(BF16), _pad_rows(a2, LORA_PAD).astype(BF16),
                       g2.astype(BF16), w0, a0, k_k.reshape(1, -1), k_a.reshape(1, -1), r_k.reshape(1, -1),
                       gn_g.reshape(1, -1), gn_b.reshape(1, -1), B, T)
    return attn.reshape(B * T, -1), y_rwkv.reshape(B * T, -1)


def odd_mixer(x, sc, sh, w_in, conv_w, a_log, dt_bias, norm_g, B, T):
    M = B * T
    nC = T // CHUNK
    wb = w_in.astype(BF16)
    n_main = GDN_QKV + GDN_VAL_WIDTH
    p = mm_mod(x, sc, sh, wb[:, :n_main], T, 512, 1024, F32)
    ba = mm_mod(x, sc, sh, wb[:, n_main:], T, 512, 128, F32)
    qkv = gdn_conv_op(p, conv_w, T)
    nh = 2 * GDN_V_HEADS
    bg = gdn_gates(ba, jnp.pad(a_log.reshape(1, -1), ((0, 0), (nh, 0))),
                   jnp.pad(dt_bias.reshape(1, -1), ((0, 0), (nh, 0))), T)
    bg = bg.reshape(B, nC, CHUNK, 2, 2, GDN_K_HEADS, 2)
    bg = jnp.transpose(bg, (0, 1, 3, 4, 5, 6, 2)).reshape(B, nC, 4 * GDN_K_HEADS, PAIR)
    o = gdn_scan(qkv.reshape(B, T, -1), p.reshape(B, T, -1), GDN_QKV // (2 * GDN_D), bg, norm_g.reshape(1, -1), B, T)
    return o.reshape(M, -1)


def kernel(x, c, positions, ada_w, ada_b, ln_g, ln_b, e_w_in, e_shift_mu, mla_q_norm, mla_w_uq, mla_kv_norm,
           mla_w_ukv, rwkv_w0, rwkv_w2, rwkv_a0, rwkv_a2, rwkv_g2, rwkv_k_k, rwkv_k_a, rwkv_r_k, rwkv_gn_g,
           rwkv_gn_b, e_w_out, o_w_in, gdn_conv, gdn_a_log, gdn_dt_bias, gdn_norm, o_w_out, moe_router,
           moe_w_gate, moe_w_up, moe_w_down):
    B, T, D = x.shape
    M = B * T
    depth = ada_w.shape[0]
    mod = ada_modulation(c, ada_w, ada_b)
    tables = rope_tables(positions)
    xf = x.reshape(M, D)
    for i in range(depth):
        m6 = mod[i].reshape(B, 6, 1, D)
        sh_m, sc_m, g_m, sh_f, sc_f, g_f = (m6[:, n] for n in range(6))
        j = i // 2
        lg0, lb0 = ln_g[i, 0].reshape(1, D), ln_b[i, 0].reshape(1, D)
        lg1, lb1 = ln_g[i, 1].reshape(1, D), ln_b[i, 1].reshape(1, D)
        if i % 2 == 0:
            a0, a1 = even_mixer(xf, sc_m, sh_m, tables, e_w_in[j], e_shift_mu[j], mla_q_norm[j], mla_w_uq[j],
                                mla_kv_norm[j], mla_w_ukv[j], rwkv_w0[j], rwkv_w2[j], rwkv_a0[j], rwkv_a2[j],
                                rwkv_g2[j], rwkv_k_k[j], rwkv_k_a[j], rwkv_r_k[j], rwkv_gn_g[j], rwkv_gn_b[j],
                                B, T)
            xf = mm_ln(a0, 0, a1, 0, e_w_out[j].astype(BF16), xf, g_m, lg0, lb0, T, 512)
        else:
            o = odd_mixer(xf, sc_m, sh_m, o_w_in[j], gdn_conv[j], gdn_a_log[j], gdn_dt_bias[j], gdn_norm[j], B, T)
            xf = mm_ln(o, 0, o, 1, o_w_out[j].astype(BF16), xf, g_m, lg0, lb0, T, 512)
        xf = moe_layer(xf, sc_f, sh_f, g_f, lg1, lb1, moe_router[i], moe_w_gate[i], moe_w_up[i], moe_w_down[i], B, T)
    return xf.reshape(B, T, D)
```

```python
import functools
import math

import jax
import jax.numpy as jnp
from jax import lax
from jax.experimental import pallas as pl
from jax.experimental.pallas import tpu as pltpu

F32 = jnp.float32
BF16 = jnp.bfloat16
I32 = jnp.int32

DEPTH = 4
DEEPNORM_ALPHA = (2 * DEPTH) ** 0.25
MLA_HEADS = 16
Q_LORA = 1536
KV_LORA = 512
NOPE_DIM = 128
ROPE_DIM = 64
V_DIM = 128
ROPE_THETA = 10000.0
MLA_IN = Q_LORA + KV_LORA + ROPE_DIM
RWKV_HEADS = 32
RWKV_N = 64
RWKV_WIDTH = RWKV_HEADS * RWKV_N
DECAY_LORA = 96
AAA_LORA = 96
GATE_LORA = 256
RWKV_DECAY_SCALE = 0.6065306597126334
RWKV_GN_EPS = 64e-5
GDN_K_HEADS = 16
GDN_V_HEADS = 32
GDN_D = 128
GDN_KEY_WIDTH = GDN_K_HEADS * GDN_D
GDN_VAL_WIDTH = GDN_V_HEADS * GDN_D
GDN_QKV = 2 * GDN_KEY_WIDTH + GDN_VAL_WIDTH
CONV_WIDTH = 5
N_EXPERTS = 16
CAPACITY_FACTOR = 2

LANES = 128
MXU_DEPTH = 256
CHUNK = 64
LOG2_CHUNK = 6
PAIR = 2 * CHUNK
GROUP = 8
INV_BLOCK = 16
LOG2_INV_BLOCK = 4
LORA_PAD = 128
EVEN_COLS = 9216
VMEM_LIMIT = 56 * 1024 * 1024


def _params(sem, vmem=VMEM_LIMIT):
    return pltpu.CompilerParams(dimension_semantics=sem, vmem_limit_bytes=vmem)


def _dot(a, b):
    return jnp.dot(a.astype(BF16), b.astype(BF16), preferred_element_type=F32)


def _dot_nt(a, b):
    return lax.dot_general(a.astype(BF16), b.astype(BF16), (((1,), (1,)), ((), ())),
                           preferred_element_type=F32)


def _dot_tn(a, b):
    return lax.dot_general(a.astype(BF16), b.astype(BF16), (((0,), (0,)), ((), ())),
                           preferred_element_type=F32)


def _bmm(a, b):
    return jnp.einsum('gij,gjk->gik', a.astype(BF16), b.astype(BF16), preferred_element_type=F32)


def _bmm_nt(a, b):
    return jnp.einsum('gik,gjk->gij', a.astype(BF16), b.astype(BF16), preferred_element_type=F32)


def _split2(x):
    hi = x.astype(BF16)
    lo = (x - hi.astype(F32)).astype(BF16)
    return hi, lo


def _split3(x):
    hi = x.astype(BF16)
    r1 = x - hi.astype(F32)
    mid = r1.astype(BF16)
    lo = (r1 - mid.astype(F32)).astype(BF16)
    return hi, mid, lo


def _sigmoid(x):
    return 1.0 / (1.0 + jnp.exp(-x))


def _silu(x):
    return x * _sigmoid(x)


def _ada_kernel(c_ref, w_ref, b_ref, o_ref):
    c = c_ref[...]
    o_ref[0] = _dot(_silu(c), w_ref[0]) + b_ref[0]


def ada_modulation(c, ada_w, ada_b):
    L, D, N = ada_w.shape
    B = c.shape[0]
    rows = 8
    cp = jnp.zeros((rows, D), F32).at[:B].set(c)
    tn = 1024
    out = pl.pallas_call(
        _ada_kernel,
        grid=(L, N // tn),
        in_specs=[pl.BlockSpec((rows, D), lambda l, j: (0, 0)),
                  pl.BlockSpec((1, D, tn), lambda l, j: (l, 0, j)),
                  pl.BlockSpec((1, 1, tn), lambda l, j: (l, 0, j))],
        out_specs=pl.BlockSpec((1, rows, tn), lambda l, j: (l, 0, j)),
        out_shape=jax.ShapeDtypeStruct((L, rows, N), F32),
        compiler_params=_params(("parallel", "parallel")),
        name="ada_mod",
    )(cp, ada_w, ada_b.reshape(L, 1, N))
    return out[:, :B]


def _mm_mod_kernel(x_ref, sc_ref, sh_ref, w_ref, o_ref, h_scr):
    @pl.when(pl.program_id(1) == 0)
    def _():
        h_scr[...] = (x_ref[...] * (1.0 + sc_ref[0]) + sh_ref[0]).astype(BF16)

    o_ref[...] = jnp.dot(h_scr[...], w_ref[...], preferred_element_type=F32).astype(o_ref.dtype)


def mm_mod(x, sc, sh, w, T, tm, tn, out_dtype):
    M, D = x.shape
    N = w.shape[1]
    tm = min(tm, T)
    tn = min(tn, N)
    per_b = T // tm
    return pl.pallas_call(
        _mm_mod_kernel,
        grid=(M // tm, N // tn),
        in_specs=[pl.BlockSpec((tm, D), lambda i, j: (i, 0)),
                  pl.BlockSpec((1, 1, D), lambda i, j: (i // per_b, 0, 0)),
                  pl.BlockSpec((1, 1, D), lambda i, j: (i // per_b, 0, 0)),
                  pl.BlockSpec((D, tn), lambda i, j: (0, j))],
        out_specs=pl.BlockSpec((tm, tn), lambda i, j: (i, j)),
        out_shape=jax.ShapeDtypeStruct((M, N), out_dtype),
        scratch_shapes=[pltpu.VMEM((tm, D), BF16)],
        compiler_params=_params(("parallel", "arbitrary")),
        name="mm_mod",
    )(x, sc, sh, w)


def _layer_norm_rows(v, lg, lb):
    mu = jnp.mean(v, axis=-1, keepdims=True)
    d = v - mu
    var = jnp.mean(d * d, axis=-1, keepdims=True)
    return d * lax.rsqrt(var + 1e-5) * lg + lb


def _mm_ln_kernel(a0_ref, a1_ref, w_ref, x_ref, g_ref, lg_ref, lb_ref, o_ref, acc):
    k = pl.program_id(1)

    @pl.when(k == 0)
    def _():
        acc[...] = jnp.dot(a0_ref[...], w_ref[...], preferred_element_type=F32)

    @pl.when(k == 1)
    def _():
        y = acc[...] + jnp.dot(a1_ref[...], w_ref[...], preferred_element_type=F32)
        v = DEEPNORM_ALPHA * x_ref[...] + g_ref[0] * y
        o_ref[...] = _layer_norm_rows(v, lg_ref[...], lb_ref[...])


def mm_ln(a0, a0_blk, a1, a1_blk, w, x, g, lg, lb, T, tm):
    M, D = x.shape
    tm = min(tm, T)
    per_b = T // tm
    return pl.pallas_call(
        _mm_ln_kernel,
        grid=(M // tm, 2),
        in_specs=[pl.BlockSpec((tm, D), lambda i, k: (i, a0_blk)),
                  pl.BlockSpec((tm, D), lambda i, k: (i, a1_blk)),
                  pl.BlockSpec((D, D), lambda i, k: (k, 0)),
                  pl.BlockSpec((tm, D), lambda i, k: (i, 0)),
                  pl.BlockSpec((1, 1, D), lambda i, k: (i // per_b, 0, 0)),
                  pl.BlockSpec((1, D), lambda i, k: (0, 0)),
                  pl.BlockSpec((1, D), lambda i, k: (0, 0))],
        out_specs=pl.BlockSpec((tm, D), lambda i, k: (i, 0)),
        out_shape=jax.ShapeDtypeStruct((M, D), F32),
        scratch_shapes=[pltpu.VMEM((tm, D), F32)],
        compiler_params=_params(("parallel", "arbitrary")),
        name="mm_ln",
    )(a0, a1, w, x, g, lg, lb)


def _rope_kernel(pos_ref, inv_ref, cs_ref, sn_ref):
    ang = pos_ref[...].astype(F32) * inv_ref[...]
    cs_ref[...] = jnp.cos(ang)
    sn_ref[...] = jnp.sin(ang)


def rope_tables(positions):
    B, T = positions.shape
    M = B * T
    inv = ROPE_THETA ** (-jnp.arange(0, ROPE_DIM, 2, dtype=F32) / ROPE_DIM)
    inv2 = jnp.concatenate([inv, inv]).reshape(1, ROPE_DIM)
    tm = min(1024, T)
    return pl.pallas_call(
        _rope_kernel,
        grid=(M // tm,),
        in_specs=[pl.BlockSpec((tm, 1), lambda i: (i, 0)),
                  pl.BlockSpec((1, ROPE_DIM), lambda i: (0, 0))],
        out_specs=[pl.BlockSpec((tm, ROPE_DIM), lambda i: (i, 0)),
                   pl.BlockSpec((tm, ROPE_DIM), lambda i: (i, 0))],
        out_shape=[jax.ShapeDtypeStruct((M, ROPE_DIM), F32)] * 2,
        compiler_params=_params(("parallel",)),
        name="rope_tables",
    )(positions.reshape(M, 1), inv2)


def _rms_rows(x, g, eps=1e-6):
    return x * lax.rsqrt(jnp.mean(x * x, axis=-1, keepdims=True) + eps) * g


def _mla_proj_kernel(cq_ref, ckv_ref, kr_ref, cs_ref, sn_ref, qn_ref, kvn_ref, wq_ref, wk_ref, wvt_ref,
                     q_ref, k_ref, vt_ref, cq_scr, ckv_scr, kr_scr):
    @pl.when(pl.program_id(1) == 0)
    def _():
        cq_scr[...] = _rms_rows(cq_ref[...], qn_ref[...]).astype(BF16)
        ckv_scr[...] = _rms_rows(ckv_ref[...], kvn_ref[...]).astype(BF16)
        kr = kr_ref[...]
        kr_scr[...] = kr[:, :ROPE_DIM] * cs_ref[...] + kr[:, ROPE_DIM:] * sn_ref[...]

    scale = (NOPE_DIM + ROPE_DIM) ** -0.5 * math.log2(math.e)
    qf = jnp.dot(cq_scr[...], wq_ref[0], preferred_element_type=F32)
    q_rope = qf[:, NOPE_DIM:NOPE_DIM + ROPE_DIM] * cs_ref[...] + qf[:, NOPE_DIM + ROPE_DIM:] * sn_ref[...]
    q_ref[0, 0, :, :NOPE_DIM] = (qf[:, :NOPE_DIM] * scale).astype(BF16)
    q_ref[0, 0, :, NOPE_DIM:] = (q_rope * scale).astype(BF16)
    ckv = ckv_scr[...]
    k_ref[0, 0, :, :NOPE_DIM] = jnp.dot(ckv, wk_ref[0], preferred_element_type=F32).astype(BF16)
    k_ref[0, 0, :, NOPE_DIM:] = kr_scr[...].astype(BF16)
    vt_ref[0, 0] = lax.dot_general(wvt_ref[0], ckv, (((1,), (1,)), ((), ())),
                                   preferred_element_type=F32).astype(BF16)


def mla_proj(p, cs, sn, qn, kvn, wq, wk, wvt, B, T):
    M = B * T
    H = MLA_HEADS
    tm = min(512, T)
    per_b = T // tm
    qk_dim = NOPE_DIM + ROPE_DIM
    out_map = lambda i, h: (i // per_b, h, i % per_b, 0)
    return pl.pallas_call(
        _mla_proj_kernel,
        grid=(M // tm, H),
        in_specs=[pl.BlockSpec((tm, Q_LORA), lambda i, h: (i, 6144 // Q_LORA)),
                  pl.BlockSpec((tm, KV_LORA), lambda i, h: (i, 7680 // KV_LORA)),
                  pl.BlockSpec((tm, 2 * ROPE_DIM), lambda i, h: (i, 8192 // (2 * ROPE_DIM))),
                  pl.BlockSpec((tm, ROPE_DIM), lambda i, h: (i, 0)),
                  pl.BlockSpec((tm, ROPE_DIM), lambda i, h: (i, 0)),
                  pl.BlockSpec((1, Q_LORA), lambda i, h: (0, 0)),
                  pl.BlockSpec((1, KV_LORA), lambda i, h: (0, 0)),
                  pl.BlockSpec((1, Q_LORA, 256), lambda i, h: (h, 0, 0)),
                  pl.BlockSpec((1, KV_LORA, NOPE_DIM), lambda i, h: (h, 0, 0)),
                  pl.BlockSpec((1, V_DIM, KV_LORA), lambda i, h: (h, 0, 0))],
        out_specs=[pl.BlockSpec((1, 1, tm, qk_dim), out_map),
                   pl.BlockSpec((1, 1, tm, qk_dim), out_map),
                   pl.BlockSpec((1, 1, V_DIM, tm), lambda i, h: (i // per_b, h, 0, i % per_b))],
        out_shape=[jax.ShapeDtypeStruct((B, H, T, qk_dim), BF16),
                   jax.ShapeDtypeStruct((B, H, T, qk_dim), BF16),
                   jax.ShapeDtypeStruct((B, H, V_DIM, T), BF16)],
        scratch_shapes=[pltpu.VMEM((tm, Q_LORA), BF16), pltpu.VMEM((tm, KV_LORA), BF16),
                        pltpu.VMEM((tm, ROPE_DIM), F32)],
        compiler_params=_params(("parallel", "arbitrary")),
        name="mla_proj",
    )(p, p, p, cs, sn, qn, kvn, wq, wk, wvt)


def _attn_kernel(q_ref, k_ref, vt_ref, o_ref, *, kc):
    q = q_ref[0, 0]
    T = k_ref.shape[2]
    nk = T // kc
    tq = q.shape[0]
    hq = min(MXU_DEPTH, tq)
    nh = tq // hq

    def scores(j):
        qj = q[j * hq:(j + 1) * hq]
        ss = [lax.dot_general(k_ref[0, 0, c * kc:(c + 1) * kc, :], qj, (((1,), (1,)), ((), ())),
                              preferred_element_type=F32) for c in range(nk)]
        m = jnp.max(ss[0], axis=0, keepdims=True)
        for c in range(1, nk):
            m = jnp.maximum(m, jnp.max(ss[c], axis=0, keepdims=True))
        return ss, m

    def finish(j, ss, m):
        acc = l = None
        for c in range(nk):
            p = jnp.exp2(ss[c] - m)
            lc = jnp.sum(p, axis=0, keepdims=True)
            ac = jnp.dot(vt_ref[0, 0, :, c * kc:(c + 1) * kc], p.astype(BF16), preferred_element_type=F32)
            l = lc if l is None else l + lc
            acc = ac if acc is None else acc + ac
        o_ref[0, j * hq:(j + 1) * hq, :] = (acc / l).T.astype(o_ref.dtype)

    prev = None
    for j in range(nh):
        cur = scores(j)
        if prev is not None:
            finish(j - 1, *prev)
        prev = cur
    finish(nh - 1, *prev)


def mla_attention(q, k, vt):
    B, H, T, dqk = q.shape
    dv = vt.shape[2]
    tq = min(1024, T)
    kc = min(1024, T)
    return pl.pallas_call(
        functools.partial(_attn_kernel, kc=kc),
        grid=(B, H, T // tq),
        in_specs=[pl.BlockSpec((1, 1, tq, dqk), lambda b, h, i: (b, h, i, 0)),
                  pl.BlockSpec((1, 1, T, dqk), lambda b, h, i: (b, h, 0, 0)),
                  pl.BlockSpec((1, 1, dv, T), lambda b, h, i: (b, h, 0, 0))],
        out_specs=pl.BlockSpec((1, tq, dv), lambda b, h, i: (b, i, h)),
        out_shape=jax.ShapeDtypeStruct((B, T, H * dv), BF16),
        compiler_params=_params(("parallel", "parallel", "parallel")),
        name="mla_attn",
    )(q, k, vt)


def _shift_rows(cur, prev8, next8, d):
    tm = cur.shape[0]
    if d == 0:
        return cur
    rolled = pltpu.roll(cur, (-d) % tm, axis=0)
    row = lax.broadcasted_iota(I32, cur.shape, 0)
    if d < 0:
        for r in range(-d):
            src = prev8[8 + d + r:8 + d + r + 1, :]
            rolled = jnp.where(row == r, src, rolled)
    else:
        for r in range(d):
            src = next8[r:r + 1, :]
            rolled = jnp.where(row == tm - d + r, src, rolled)
    return rolled


def _halo_specs(tm, cw, per_b, nrow8, col_of):
    r8 = tm // 8
    cur = pl.BlockSpec((tm, cw), lambda i, j: (i, col_of(j)))
    prv = pl.BlockSpec((8, cw), lambda i, j: (jnp.maximum(i * r8 - 1, 0), col_of(j)))
    nxt = pl.BlockSpec((8, cw), lambda i, j: (jnp.minimum((i + 1) * r8, nrow8 - 1), col_of(j)))
    return cur, prv, nxt


def _halo_load(cur_ref, prev_ref, next_ref, per_b):
    ti = pl.program_id(0) % per_b
    cur = cur_ref[...].astype(F32)
    prev8 = prev_ref[...].astype(F32) * jnp.where(ti == 0, 0.0, 1.0)
    next8 = next_ref[...].astype(F32) * jnp.where(ti == per_b - 1, 0.0, 1.0)
    return cur, prev8, next8


def _pair_masks(rev):
    i = lax.broadcasted_iota(I32, (PAIR, PAIR), 0)
    j = lax.broadcasted_iota(I32, (PAIR, PAIR), 1)
    same = lax.shift_right_logical(i, LOG2_CHUNK) == lax.shift_right_logical(j, LOG2_CHUNK)
    if rev:
        strict = jnp.logical_and(same, i < j)
        incl = jnp.logical_and(same, i <= j)
    else:
        strict = jnp.logical_and(same, i > j)
        incl = jnp.logical_and(same, i >= j)
    eye_b = i == j
    same_blk = lax.shift_right_logical(i, LOG2_INV_BLOCK) == lax.shift_right_logical(j, LOG2_INV_BLOCK)
    return strict, incl, eye_b, same_blk


def _unit_tri_inverse(nmat, eye, same_blk):
    n = nmat.shape[-1]
    nd = jnp.where(same_blk, nmat, 0.0)
    noff = nmat - nd
    x = eye + nd
    p = _bmm(nd, nd)
    yield
    for s in range(1, LOG2_INV_BLOCK):
        if s < LOG2_INV_BLOCK - 1:
            xp = _bmm(jnp.concatenate([x, p], axis=1), p)
            x = x + xp[:, :n]
            p = xp[:, n:]
        else:
            x = x + _bmm(x, p)
        yield
    y = _bmm(x, noff)
    yield
    acc = x
    for _ in range(CHUNK // INV_BLOCK - 1):
        acc = x + _bmm(y, acc)
        yield
    return acc


def _aligned(start, m):
    return start if isinstance(start, int) else pl.multiple_of(start, m)


def _interleave(n_steps, step_fn, gens, stages_per_step):
    for i in range(n_steps):
        step_fn(i)
        for _ in range(stages_per_step):
            for g in gens:
                next(g, None)
    for g in gens:
        for _ in g:
            pass


def _rows_to_cols(rows, eye_b):
    return jnp.sum(jnp.where(eye_b, rows, 0.0), axis=2, keepdims=True)


def _stack_heads(x, head0):
    return jnp.concatenate([jnp.where(head0, x, 0.0), jnp.where(head0, 0.0, x)], axis=1)


def _shift_mix_kernel(cur_ref, prev_ref, next_ref, mu_ref, o_ref, *, per_b, lora):
    cur, prev8, next8 = _halo_load(cur_ref, prev_ref, next_ref, per_b)
    prv = _shift_rows(cur, prev8, next8, -1)
    nxt = _shift_rows(cur, prev8, next8, 1)
    z = cur + (0.5 * (prv + nxt) - cur) * mu_ref[...]
    if lora:
        col = lax.broadcasted_iota(I32, z.shape, 1)
        z = jnp.where(col < 2 * LORA_PAD, jnp.tanh(z), jnp.where(col >= 4 * LORA_PAD, _sigmoid(z), z))
    o_ref[...] = z.astype(o_ref.dtype)


def shift_mix(p, mu, T, col0, width, cw, lora):
    M = p.shape[0]
    tm = min(512, T)
    per_b = T // tm
    c0 = col0 // cw
    cur, prv, nxt = _halo_specs(tm, cw, per_b, M // 8, lambda j: c0 + j)
    return pl.pallas_call(
        functools.partial(_shift_mix_kernel, per_b=per_b, lora=lora),
        grid=(M // tm, width // cw),
        in_specs=[cur, prv, nxt, pl.BlockSpec((1, cw), lambda i, j: (0, j))],
        out_specs=pl.BlockSpec((tm, cw), lambda i, j: (i, j)),
        out_shape=jax.ShapeDtypeStruct((M, width), BF16),
        compiler_params=_params(("parallel", "parallel")),
        name="rwkv_shift_mix",
    )(p, p, p, mu)


def _rwkv_prepare(g, G, d, rev, r_ref, k_ref, v_ref, la_ref, w2_ref, a2_ref, w0_ref, a0_ref, kk_ref, ka_ref,
                  p16_ref, p32_ref, bkt_ref, bon_ref):
    C = CHUNK
    rows = pl.ds(_aligned(g * (G * C), G * C), G * C)
    r = r_ref[0, rows, :].astype(F32)
    k = k_ref[0, rows, :].astype(F32)
    v = v_ref[0, rows, :].astype(F32)
    wd = la_ref[0, rows, d * LORA_PAD:(d + 1) * LORA_PAD]
    ad = la_ref[0, rows, (2 + d) * LORA_PAD:(3 + d) * LORA_PAD]
    lw = -RWKV_DECAY_SCALE * _sigmoid(w0_ref[d:d + 1, :] + jnp.dot(wd, w2_ref[d], preferred_element_type=F32))
    a_sig = _sigmoid(a0_ref[d:d + 1, :] + jnp.dot(ad, a2_ref[d], preferred_element_type=F32))

    lane = lax.broadcasted_iota(I32, (1, LANES), 1)
    head0 = lane < RWKV_N
    kx = k * kk_ref[...]
    sq = kx * kx
    ss = jnp.where(head0, jnp.sum(jnp.where(head0, sq, 0.0), axis=1, keepdims=True),
                   jnp.sum(jnp.where(head0, 0.0, sq), axis=1, keepdims=True))
    kk = kx * lax.rsqrt(ss + 1e-6)
    k_dir = k * (1.0 + (a_sig - 1.0) * ka_ref[...])
    b = kk * a_sig
    a = -kk
    bon_ref[d, rows, :] = r * k_dir
    yield

    g3 = lambda t: t.reshape(G, C, LANES)
    lw3, r3, a3, b3, kd3, v3 = g3(lw), g3(r), g3(a), g3(b), g3(k_dir), g3(v)
    ci = lax.broadcasted_iota(I32, (C, C), 0)
    cj = lax.broadcasted_iota(I32, (C, C), 1)
    tri = jnp.where((ci <= cj) if rev else (ci >= cj), 1.0, 0.0).astype(BF16)
    tri = jnp.broadcast_to(tri[None], (G, C, C))
    hi, lo = _split2(lw3)
    cum = (jnp.einsum('gij,gjk->gik', tri, hi, preferred_element_type=F32)
           + jnp.einsum('gij,gjk->gik', tri, lo, preferred_element_type=F32))
    yield
    end = 0 if rev else C - 1
    cum_end = cum[:, end:end + 1, :]
    e_neg = jnp.exp(-cum)
    e_tail = jnp.exp(cum_end - cum)
    h0 = head0.reshape(1, 1, LANES)
    a2s = _stack_heads(a3 * jnp.exp(cum - lw3), h0)
    r2s = _stack_heads(r3 * jnp.exp(cum), h0)
    k2s = _stack_heads(kd3 * e_neg, h0)
    b2s = _stack_heads(b3 * e_neg, h0)
    v2s = _stack_heads(v3, h0)
    kh2 = _stack_heads(kd3 * e_tail, h0)
    bh2 = _stack_heads(b3 * e_tail, h0)

    strict, incl, eye_b, same_blk = _pair_masks(rev)
    eye = jnp.where(eye_b, 1.0, 0.0)
    gram = _bmm_nt(jnp.concatenate([a2s, r2s], axis=1), jnp.concatenate([k2s, b2s], axis=1))
    a_ak = jnp.where(strict, gram[:, :PAIR, :PAIR], 0.0)
    a_ab = jnp.where(strict, gram[:, :PAIR, PAIR:], 0.0)
    a_rk = jnp.where(incl, gram[:, PAIR:, :PAIR], 0.0)
    a_rb = jnp.where(incl, gram[:, PAIR:, PAIR:], 0.0).astype(BF16)
    p16_ref[d, :, 2 * PAIR:4 * PAIR, :] = jnp.concatenate([v2s.astype(BF16), a_rb], axis=1)
    bkt_ref[d] = jnp.swapaxes(jnp.concatenate([bh2, kh2], axis=1), 1, 2).astype(BF16)
    p32_ref[d, :, 2 * PAIR:3 * PAIR, :] = jnp.broadcast_to(_rows_to_cols(jnp.exp(cum_end), eye_b), (G, PAIR, LANES))
    yield
    kv = _bmm(jnp.concatenate([a_ak, a_rk], axis=1), v2s)
    p32_ref[d, :, PAIR:2 * PAIR, :] = kv[:, PAIR:]
    yield
    tinv = yield from _unit_tri_inverse(a_ab, eye, same_blk)
    tt = _bmm(tinv, jnp.concatenate([a2s, kv[:, :PAIR]], axis=2))
    p16_ref[d, :, 0:2 * PAIR, :] = jnp.concatenate([tt[:, :, :LANES], r2s], axis=1).astype(BF16)
    p32_ref[d, :, 0:PAIR, :] = tt[:, :, LANES:]


RWKV_P16_ROWS = 4 * PAIR
RWKV_P32_ROWS = 3 * PAIR
RWKV_PREP_STAGES = 14


def _rwkv_step(H, p16_ref, p32_ref, bkt_ref, d, i):
    C = CHUNK
    lhs = p16_ref[d, i, 0:2 * PAIR, :]
    v2 = p16_ref[d, i, 2 * PAIR:3 * PAIR, :]
    a_rb = p16_ref[d, i, 3 * PAIR:4 * PAIR, :]
    sr = _dot(lhs, H)
    u2 = sr[:PAIR] + p32_ref[d, i, 0:PAIR, :]
    y2 = sr[PAIR:] + p32_ref[d, i, PAIR:2 * PAIR, :] + _dot(a_rb, u2)
    y = y2[:C] + y2[C:]
    upd = _dot(bkt_ref[d, i], jnp.concatenate([u2.astype(BF16), v2], axis=0))
    return H * p32_ref[d, i, 2 * PAIR:3 * PAIR, :] + upd, y


def _rwkv_scan_kernel(r_ref, k_ref, v_ref, la_ref, w2_ref, a2_ref, g2_ref, w0_ref, a0_ref,
                      kk_ref, ka_ref, rk_ref, gng_ref, gnb_ref, o_ref, yf_scr, yb_scr, bon_scr,
                      a16, a32, akt, b16, b32, bkt, *, nC, G):
    C = CHUNK
    nG = nC // G
    GC = G * C
    refs = (r_ref, k_ref, v_ref, la_ref, w2_ref, a2_ref, w0_ref, a0_ref, kk_ref, ka_ref)

    def prepare(g, bufs):
        return [_rwkv_prepare(g, G, 0, False, *refs, *bufs, bon_scr),
                _rwkv_prepare(nG - 1 - g, G, 1, True, *refs, *bufs, bon_scr)]

    def run(g, carry, cur, nxt_g, nxt):
        state = list(carry)
        gb = nG - 1 - g

        def step(i):
            ib = G - 1 - i
            state[0], yf = _rwkv_step(state[0], *cur, 0, i)
            state[1], yb = _rwkv_step(state[1], *cur, 1, ib)
            yf_scr[pl.ds(_aligned(g * GC + i * C, C), C), :] = yf
            yb_scr[pl.ds(_aligned(gb * GC + ib * C, C), C), :] = yb

        gens = [] if nxt_g is None else prepare(nxt_g, nxt)
        _interleave(G, step, gens, RWKV_PREP_STAGES // G + 1)
        return tuple(state)

    bufs_a, bufs_b = (a16, a32, akt), (b16, b32, bkt)
    _interleave(0, None, prepare(0, bufs_a), 0)

    def body(h, carry):
        g = 2 * h
        carry = run(g, carry, bufs_a, g + 1, bufs_b)
        return run(g + 1, carry, bufs_b, g + 2, bufs_a)

    z = jnp.zeros((LANES, LANES), F32)
    carry = lax.fori_loop(0, nG // 2 - 1, body, (z, z))
    carry = run(nG - 2, carry, bufs_a, nG - 1, bufs_b)
    run(nG - 1, carry, bufs_b, None, None)

    def fin(g, _):
        sl = pl.ds(_aligned(g * GC, GC), GC)
        y = yf_scr[sl, :] + yb_scr[sl, :]
        lane = lax.broadcasted_iota(I32, (1, LANES), 1)
        head0 = lane < RWKV_N

        def seg_sum(x):
            s0 = jnp.sum(jnp.where(head0, x, 0.0), axis=1, keepdims=True)
            s1 = jnp.sum(jnp.where(head0, 0.0, x), axis=1, keepdims=True)
            return jnp.where(head0, s0, s1)

        mu = seg_sum(y) * (1.0 / RWKV_N)
        dlt = y - mu
        var = seg_sum(dlt * dlt) * (1.0 / RWKV_N)
        yn = dlt * lax.rsqrt(var + RWKV_GN_EPS) * gng_ref[...] + gnb_ref[...]
        bsum = seg_sum((bon_scr[0, sl, :] + bon_scr[1, sl, :]) * rk_ref[...])
        v = v_ref[0, sl, :].astype(F32)
        gate = jnp.dot(la_ref[0, sl, 4 * LORA_PAD:], g2_ref[...], preferred_element_type=F32)
        o_ref[0, sl, :] = ((yn + bsum * v) * gate).astype(o_ref.dtype)
        return 0

    lax.fori_loop(0, nG, fin, 0)


def rwkv_scan(z_rkv, la, w2, a2, g2, w0, a0, k_k, k_a, r_k, gn_g, gn_b, B, T):
    nC = T // CHUNK
    G = min(GROUP, nC // 2)
    assert (nC // G) % 2 == 0
    npair = RWKV_WIDTH // LANES
    nl = la.shape[-1]
    vec = lambda: pl.BlockSpec((1, LANES), lambda b, p: (0, p))
    return pl.pallas_call(
        functools.partial(_rwkv_scan_kernel, nC=nC, G=G),
        grid=(B, npair),
        in_specs=[pl.BlockSpec((1, T, LANES), lambda b, p: (b, 0, p)),
                  pl.BlockSpec((1, T, LANES), lambda b, p: (b, 0, npair + p)),
                  pl.BlockSpec((1, T, LANES), lambda b, p: (b, 0, 2 * npair + p)),
                  pl.BlockSpec((1, T, nl), lambda b, p: (b, 0, 0)),
                  pl.BlockSpec((2, LORA_PAD, LANES), lambda b, p: (0, 0, p)),
                  pl.BlockSpec((2, LORA_PAD, LANES), lambda b, p: (0, 0, p)),
                  pl.BlockSpec((GATE_LORA, LANES), lambda b, p: (0, p)),
                  pl.BlockSpec((2, LANES), lambda b, p: (0, p)),
                  pl.BlockSpec((2, LANES), lambda b, p: (0, p)),
                  vec(), vec(), vec(), vec(), vec()],
        out_specs=pl.BlockSpec((1, T, LANES), lambda b, p: (b, 0, p)),
        out_shape=jax.ShapeDtypeStruct((B, T, RWKV_WIDTH), BF16),
        scratch_shapes=[pltpu.VMEM((T, LANES), F32), pltpu.VMEM((T, LANES), F32),
                        pltpu.VMEM((2, T, LANES), F32)]
        + 2 * [pltpu.VMEM((2, G, RWKV_P16_ROWS, LANES), BF16), pltpu.VMEM((2, G, RWKV_P32_ROWS, LANES), F32),
               pltpu.VMEM((2, G, LANES, 2 * PAIR), BF16)],
        compiler_params=_params(("parallel", "arbitrary")),
        name="rwkv_scan",
    )(z_rkv, z_rkv, z_rkv, la, w2, a2, g2, w0, a0, k_k, k_a, r_k, gn_g, gn_b)


def _gdn_conv_kernel(cur_ref, prev_ref, next_ref, w_ref, o_ref, *, per_b, q_blocks, k_blocks):
    cur, prev8, next8 = _halo_load(cur_ref, prev_ref, next_ref, per_b)
    half = CONV_WIDTH // 2
    acc = jnp.zeros_like(cur)
    for i in range(CONV_WIDTH):
        acc = acc + _shift_rows(cur, prev8, next8, i - half) * w_ref[i:i + 1, :]
    y = _silu(acc)
    j = pl.program_id(1)
    cw = y.shape[1]
    outs = []
    for s in range(cw // GDN_D):
        ys = y[:, s * GDN_D:(s + 1) * GDN_D]
        nrm = lax.rsqrt(jnp.sum(ys * ys, axis=1, keepdims=True) + 1e-6)
        outs.append(ys * nrm)
    yn = jnp.concatenate(outs, axis=1) if len(outs) > 1 else outs[0]
    res = jnp.where(j < q_blocks, yn * (GDN_D ** -0.5), jnp.where(j < k_blocks, yn, y))
    o_ref[...] = res.astype(o_ref.dtype)


def gdn_conv_op(p, conv_w, T):
    M = p.shape[0]
    tm = min(512, T)
    cw = 512
    per_b = T // tm
    cur, prv, nxt = _halo_specs(tm, cw, per_b, M // 8, lambda j: j)
    return pl.pallas_call(
        functools.partial(_gdn_conv_kernel, per_b=per_b, q_blocks=GDN_KEY_WIDTH // cw,
                          k_blocks=2 * GDN_KEY_WIDTH // cw),
        grid=(M // tm, GDN_QKV // cw),
        in_specs=[cur, prv, nxt, pl.BlockSpec((CONV_WIDTH, cw), lambda i, j: (0, j))],
        out_specs=pl.BlockSpec((tm, cw), lambda i, j: (i, j)),
        out_shape=jax.ShapeDtypeStruct((M, GDN_QKV), BF16),
        compiler_params=_params(("parallel", "parallel")),
        name="gdn_conv",
    )(p, p, p, conv_w)


def _gdn_gate_kernel(ba_ref, alog_ref, dtb_ref, o_ref):
    C = CHUNK
    x = ba_ref[...]
    tm = x.shape[0]
    nh = 2 * GDN_V_HEADS
    beta = _sigmoid(x)
    a = x + dtb_ref[...]
    sp = jnp.maximum(a, 0.0) + jnp.log1p(jnp.exp(-jnp.abs(a)))
    g = -jnp.exp(alog_ref[...]) * sp
    i = lax.broadcasted_iota(I32, (C, C), 0)
    j = lax.broadcasted_iota(I32, (C, C), 1)
    lower = jnp.where(i >= j, 1.0, 0.0).astype(BF16)
    upper = jnp.where(i <= j, 1.0, 0.0).astype(BF16)
    col = lax.broadcasted_iota(I32, (C, 2 * nh), 1)
    for c in range(tm // C):
        gc = g[c * C:(c + 1) * C]
        hi, mid, lo = _split3(gc)
        fw = (jnp.dot(lower, hi, preferred_element_type=F32) + jnp.dot(lower, mid, preferred_element_type=F32)
              + jnp.dot(lower, lo, preferred_element_type=F32))
        bw = (jnp.dot(upper, hi, preferred_element_type=F32) + jnp.dot(upper, mid, preferred_element_type=F32)
              + jnp.dot(upper, lo, preferred_element_type=F32))
        o_ref[c * C:(c + 1) * C, :] = jnp.where(col < nh, beta[c * C:(c + 1) * C],
                                                jnp.where(col < nh + GDN_V_HEADS, fw, bw))


def gdn_gates(ba, a_log, dt_bias, T):
    M = ba.shape[0]
    tm = min(512, T)
    nh = 2 * GDN_V_HEADS
    return pl.pallas_call(
        _gdn_gate_kernel,
        grid=(M // tm,),
        in_specs=[pl.BlockSpec((tm, 2 * nh), lambda i: (i, 0)),
                  pl.BlockSpec((1, 2 * nh), lambda i: (0, 0)),
                  pl.BlockSpec((1, 2 * nh), lambda i: (0, 0))],
        out_specs=pl.BlockSpec((tm, 2 * nh), lambda i: (i, 0)),
        out_shape=jax.ShapeDtypeStruct((M, 2 * nh), F32),
        compiler_params=_params(("parallel",)),
        name="gdn_gates",
    )(ba, a_log, dt_bias)


def _gdn_prepare(g, G, d, rev, kh, q_ref, k_ref, v_ref, bg_ref, p16_ref, u_ref, dec_ref):
    C = CHUNK
    rows = pl.ds(_aligned(g * (G * C), G * C), G * C)
    q = q_ref[0, rows, :].astype(F32).reshape(G, C, GDN_D)
    k = k_ref[0, rows, :].astype(F32).reshape(G, C, GDN_D)
    v = v_ref[0, rows, :].astype(F32).reshape(G, C, 2 * GDN_D)
    v2 = jnp.concatenate([v[:, :, :GDN_D], v[:, :, GDN_D:]], axis=1)
    k2 = jnp.concatenate([k, k], axis=1)
    q2 = jnp.concatenate([q, q], axis=1)
    chunks = pl.ds(g * G, G)
    beta_row = bg_ref[0, chunks, pl.ds(d * GDN_K_HEADS + kh, 1), :]
    gc_row = bg_ref[0, chunks, pl.ds((2 + d) * GDN_K_HEADS + kh, 1), :]
    strict, incl, eye_b, same_blk = _pair_masks(rev)
    eye = jnp.where(eye_b, 1.0, 0.0)
    beta_col = _rows_to_cols(beta_row, eye_b)
    gc_col = _rows_to_cols(gc_row, eye_b)
    decay = jnp.where(incl, jnp.exp(jnp.where(incl, gc_col - gc_row, 0.0)), 0.0)
    gram = _bmm_nt(jnp.concatenate([k2, q2], axis=1), k2)
    lower = jnp.where(strict, gram[:, :PAIR] * beta_col * decay, 0.0)
    eg = jnp.exp(gc_col)
    end = 0 if rev else C - 1
    gl0 = gc_row[:, :, end:end + 1]
    gl1 = gc_row[:, :, C + end:C + end + 1]
    rowi = lax.broadcasted_iota(I32, (1, PAIR, 1), 1)
    gl_col = jnp.where(rowi < C, gl0, gl1)
    lane2 = lax.broadcasted_iota(I32, (1, 1, 2 * GDN_D), 2)
    decay_s = jnp.where(lane2 < GDN_D, jnp.exp(gl0), jnp.exp(gl1))
    p16_ref[d, :, PAIR:2 * PAIR, :] = (q2 * eg).astype(BF16)
    p16_ref[d, :, 2 * PAIR:3 * PAIR, :] = (gram[:, PAIR:] * decay).astype(BF16)
    p16_ref[d, :, 3 * PAIR:4 * PAIR, :] = jnp.swapaxes(k2 * jnp.exp(gl_col - gc_col), 1, 2).astype(BF16)
    dec_ref[d] = jnp.broadcast_to(decay_s, (G, 8, 2 * GDN_D))
    yield
    t_inv = yield from _unit_tri_inverse(-lower, eye, same_blk)
    kb = k2 * beta_col
    uw = _bmm(t_inv, jnp.concatenate([v2 * beta_col, kb * eg], axis=2))
    u_ref[d] = uw[:, :, :GDN_D]
    p16_ref[d, :, 0:PAIR, :] = uw[:, :, GDN_D:].astype(BF16)


GDN_P16_ROWS = 4 * PAIR
GDN_PREP_STAGES = 10


def _gdn_step(S, p16_ref, u_ref, dec_ref, d, i):
    C = CHUNK
    ws = _dot(p16_ref[d, i, 0:2 * PAIR, :], S)
    head0_rows = lax.broadcasted_iota(I32, (PAIR, 1), 0) < C
    w_s = jnp.where(head0_rows, ws[:PAIR, :GDN_D], ws[:PAIR, GDN_D:])
    q_s = jnp.where(head0_rows, ws[PAIR:, :GDN_D], ws[PAIR:, GDN_D:])
    v_new = u_ref[d, i] - w_s
    o2 = q_s + _dot(p16_ref[d, i, 2 * PAIR:3 * PAIR, :], v_new)
    vcat = jnp.concatenate([jnp.where(head0_rows, v_new, 0.0), jnp.where(head0_rows, 0.0, v_new)], axis=1)
    S_new = S * dec_ref[d, i, 0:1, :] + _dot(p16_ref[d, i, 3 * PAIR:4 * PAIR, :], vcat)
    return S_new, jnp.concatenate([o2[:C], o2[C:]], axis=1)


def _gdn_scan_kernel(q_ref, k_ref, v_ref, z_ref, bg_ref, ng_ref, o_ref, of_scr, ob_scr,
                     a16, au, ad, b16, bu, bd, *, nC, G):
    C = CHUNK
    nG = nC // G
    GC = G * C
    kh = pl.program_id(1)

    def prepare(g, bufs):
        return [_gdn_prepare(g, G, 0, False, kh, q_ref, k_ref, v_ref, bg_ref, *bufs),
                _gdn_prepare(nG - 1 - g, G, 1, True, kh, q_ref, k_ref, v_ref, bg_ref, *bufs)]

    def run(g, carry, cur, nxt_g, nxt):
        state = list(carry)
        gb = nG - 1 - g

        def step(i):
            ib = G - 1 - i
            state[0], of = _gdn_step(state[0], *cur, 0, i)
            state[1], ob = _gdn_step(state[1], *cur, 1, ib)
            of_scr[pl.ds(_aligned(g * GC + i * C, C), C), :] = of
            ob_scr[pl.ds(_aligned(gb * GC + ib * C, C), C), :] = ob

        gens = [] if nxt_g is None else prepare(nxt_g, nxt)
        _interleave(G, step, gens, GDN_PREP_STAGES // G + 1)
        return tuple(state)

    bufs_a, bufs_b = (a16, au, ad), (b16, bu, bd)
    _interleave(0, None, prepare(0, bufs_a), 0)

    def body(h, carry):
        g = 2 * h
        carry = run(g, carry, bufs_a, g + 1, bufs_b)
        return run(g + 1, carry, bufs_b, g + 2, bufs_a)

    z0 = jnp.zeros((GDN_D, 2 * GDN_D), F32)
    carry = lax.fori_loop(0, nG // 2 - 1, body, (z0, z0))
    carry = run(nG - 2, carry, bufs_a, nG - 1, bufs_b)
    run(nG - 1, carry, bufs_b, None, None)

    def fin(g, _):
        sl = pl.ds(_aligned(g * GC, GC), GC)
        o = of_scr[sl, :] + ob_scr[sl, :]
        z = z_ref[0, sl, :].astype(F32)
        for s in range(2):
            os_ = o[:, s * GDN_D:(s + 1) * GDN_D]
            on = os_ * lax.rsqrt(jnp.mean(os_ * os_, axis=-1, keepdims=True) + 1e-6) * ng_ref[...]
            o_ref[0, sl, s * GDN_D:(s + 1) * GDN_D] = (on * _silu(z[:, s * GDN_D:(s + 1) * GDN_D])).astype(o_ref.dtype)
        return 0

    lax.fori_loop(0, nG, fin, 0)


def gdn_scan(qkv, z_src, z_col0, bg, norm_g, B, T):
    nC = T // CHUNK
    G = min(GROUP, nC // 2)
    assert (nC // G) % 2 == 0
    kb = GDN_KEY_WIDTH // GDN_D
    vb = 2 * GDN_KEY_WIDTH // (2 * GDN_D)
    return pl.pallas_call(
        functools.partial(_gdn_scan_kernel, nC=nC, G=G),
        grid=(B, GDN_K_HEADS),
        in_specs=[pl.BlockSpec((1, T, GDN_D), lambda b, h: (b, 0, h)),
                  pl.BlockSpec((1, T, GDN_D), lambda b, h: (b, 0, kb + h)),
                  pl.BlockSpec((1, T, 2 * GDN_D), lambda b, h: (b, 0, vb + h)),
                  pl.BlockSpec((1, T, 2 * GDN_D), lambda b, h: (b, 0, z_col0 + h)),
                  pl.BlockSpec((1, nC, 4 * GDN_K_HEADS, PAIR), lambda b, h: (b, 0, 0, 0)),
                  pl.BlockSpec((1, GDN_D), lambda b, h: (0, 0))],
        out_specs=pl.BlockSpec((1, T, 2 * GDN_D), lambda b, h: (b, 0, h)),
        out_shape=jax.ShapeDtypeStruct((B, T, GDN_VAL_WIDTH), BF16),
        scratch_shapes=[pltpu.VMEM((T, 2 * GDN_D), F32), pltpu.VMEM((T, 2 * GDN_D), F32)]
        + 2 * [pltpu.VMEM((2, G, GDN_P16_ROWS, GDN_D), BF16), pltpu.VMEM((2, G, PAIR, GDN_D), F32),
               pltpu.VMEM((2, G, 8, 2 * GDN_D), F32)],
        compiler_params=_params(("parallel", "arbitrary")),
        name="gdn_scan",
    )(qkv, qkv, qkv, z_src, bg, norm_g)


def _router_kernel(x_ref, sc_ref, sh_ref, wr_ref, h_ref, aff_ref):
    h = x_ref[...] * (1.0 + sc_ref[0]) + sh_ref[0]
    h_ref[...] = h.astype(BF16)
    hh, hm, hl = _split3(h)
    wh, wm, wl = _split3(wr_ref[...])
    nt = lambda a, b: lax.dot_general(a, b, (((1,), (1,)), ((), ())), preferred_element_type=F32)
    logits = (nt(wh, hh) + (nt(wh, hm) + nt(wm, hh)) + (nt(wh, hl) + nt(wl, hh) + nt(wm, hm)))
    m = jnp.max(logits, axis=0, keepdims=True)
    e = jnp.exp(logits - m)
    aff_ref[0] = e / jnp.sum(e, axis=0, keepdims=True)


def moe_router_op(x, sc, sh, wr_t, B, T):
    M, D = x.shape
    E = wr_t.shape[0]
    tm = min(512, T)
    per_b = T // tm
    return pl.pallas_call(
        _router_kernel,
        grid=(M // tm,),
        in_specs=[pl.BlockSpec((tm, D), lambda i: (i, 0)),
                  pl.BlockSpec((1, 1, D), lambda i: (i // per_b, 0, 0)),
                  pl.BlockSpec((1, 1, D), lambda i: (i // per_b, 0, 0)),
                  pl.BlockSpec((E, D), lambda i: (0, 0))],
        out_specs=[pl.BlockSpec((tm, D), lambda i: (i, 0)),
                   pl.BlockSpec((1, E, tm), lambda i: (i // per_b, 0, i % per_b))],
        out_shape=[jax.ShapeDtypeStruct((M, D), BF16), jax.ShapeDtypeStruct((B, E, T), F32)],
        compiler_params=_params(("parallel",)),
        name="moe_router",
    )(x, sc, sh, wr_t)


def _lane_cumsum(m01, blk):
    E, T = m01.shape
    i = lax.broadcasted_iota(I32, (blk, blk), 0)
    j = lax.broadcasted_iota(I32, (blk, blk), 1)
    upper = jnp.where(i <= j, 1.0, 0.0).astype(BF16)
    carry = jnp.zeros((E, 1), F32)
    parts = []
    for c in range(T // blk):
        seg = m01[:, c * blk:(c + 1) * blk]
        cs = jnp.dot(seg.astype(BF16), upper, preferred_element_type=F32) + carry
        parts.append(cs)
        carry = cs[:, blk - 1:blk]
    return jnp.concatenate(parts, axis=1) if len(parts) > 1 else parts[0]


def _topk_kernel(aff_ref, pos_ref, los_ref, *, cap, ts):
    a = aff_ref[0]
    E, T = a.shape
    bits = pltpu.bitcast(a, I32)

    def body(i, t):
        cand = t | lax.shift_left(jnp.int32(1), 30 - i)
        cnt = jnp.sum((bits >= cand).astype(I32), axis=1, keepdims=True)
        return jnp.where(cnt >= cap, cand, t)

    thr = lax.fori_loop(0, 31, body, jnp.zeros((E, 1), I32))
    gt = bits > thr
    eq = bits == thr
    need = cap - jnp.sum(gt.astype(I32), axis=1, keepdims=True)
    blk = min(512, T)
    eq01 = jnp.where(eq, 1.0, 0.0)
    eq_rank = _lane_cumsum(eq01, blk) - eq01
    sel = jnp.logical_or(gt, jnp.logical_and(eq, eq_rank < need.astype(F32)))
    sel01 = jnp.where(sel, 1.0, 0.0)
    slot = _lane_cumsum(sel01, blk) - sel01
    pos_ref[0] = jnp.where(sel, slot.astype(I32), -1)
    ti = lax.broadcasted_iota(I32, (T, LANES), 0)
    wi = lax.broadcasted_iota(I32, (T, LANES), 1)
    before = jnp.where(ti < wi * ts, 1.0, 0.0).astype(BF16)
    los_ref[0] = jnp.dot(sel01.astype(BF16), before, preferred_element_type=F32).astype(I32)


def moe_topk(aff_t, cap, ts):
    B, E, T = aff_t.shape
    return pl.pallas_call(
        functools.partial(_topk_kernel, cap=cap, ts=ts),
        grid=(B,),
        in_specs=[pl.BlockSpec((1, E, T), lambda b: (b, 0, 0))],
        out_specs=[pl.BlockSpec((1, E, T), lambda b: (b, 0, 0)),
                   pl.BlockSpec((1, E, LANES), lambda b: (b, 0, 0))],
        out_shape=[jax.ShapeDtypeStruct((B, E, T), I32), jax.ShapeDtypeStruct((B, E, LANES), I32)],
        compiler_params=_params(("parallel",)),
        name="moe_topk",
    )(aff_t)


def _slot_windows(cap, win):
    win = min(win, cap)
    return win, cap // win


def _gather_kernel(los_ref, pos_ref, h_ref, o_ref, acc, *, cap, nt):
    b, e, c = pl.program_id(0), pl.program_id(1), pl.program_id(2)
    base = (b * pl.num_programs(1) + e) * (nt + 1) + c
    lo = los_ref[base]
    hi = los_ref[base + 1]
    tc = h_ref.shape[0]
    win, nwin = _slot_windows(cap, LANES)

    @pl.when(c == 0)
    def _():
        acc[...] = jnp.zeros(acc.shape, F32)

    for w in range(nwin):
        @pl.when(jnp.logical_and(lo < (w + 1) * win, hi > w * win))
        def _():
            slot = lax.broadcasted_iota(I32, (win, tc), 0) + w * win
            onehot = jnp.where(slot == pos_ref[0, 0], 1.0, 0.0).astype(BF16)
            acc[w * win:(w + 1) * win, :] += jnp.dot(onehot, h_ref[...], preferred_element_type=F32)

    @pl.when(c == pl.num_programs(2) - 1)
    def _():
        o_ref[0, 0] = acc[...].astype(o_ref.dtype)


def moe_gather(los, pos4, h, B, T, cap, ts):
    E = pos4.shape[1]
    D = h.shape[1]
    nt = T // ts
    return pl.pallas_call(
        functools.partial(_gather_kernel, cap=cap, nt=nt),
        grid_spec=pltpu.PrefetchScalarGridSpec(
            num_scalar_prefetch=1,
            grid=(B, E, nt),
            in_specs=[pl.BlockSpec((1, 1, 1, ts), lambda b, e, c, s: (b, e, 0, c)),
                      pl.BlockSpec((ts, D), lambda b, e, c, s: (b * nt + c, 0))],
            out_specs=pl.BlockSpec((1, 1, cap, D), lambda b, e, c, s: (b, e, 0, 0)),
            scratch_shapes=[pltpu.VMEM((cap, D), F32)]),
        out_shape=jax.ShapeDtypeStruct((B, E, cap, D), BF16),
        compiler_params=_params(("parallel", "parallel", "arbitrary")),
        name="moe_gather",
    )(los, pos4, h)


def _ffn_kernel(x_ref, wg_ref, wu_ref, wd_ref, o_ref):
    x = x_ref[0, 0]
    g = jnp.dot(x, wg_ref[0], preferred_element_type=F32)
    u = jnp.dot(x, wu_ref[0], preferred_element_type=F32)
    hid = (_silu(g) * u).astype(BF16)
    o_ref[0, 0] = jnp.dot(hid, wd_ref[0], preferred_element_type=F32).astype(o_ref.dtype)


def moe_ffn(xs, wg, wu, wd, layer):
    B, E, cap, D = xs.shape
    Fh = wg.shape[-1]
    e0 = layer * E
    return pl.pallas_call(
        _ffn_kernel,
        grid=(E, B),
        in_specs=[pl.BlockSpec((1, 1, cap, D), lambda e, b: (b, e, 0, 0)),
                  pl.BlockSpec((1, D, Fh), lambda e, b: (e0 + e, 0, 0)),
                  pl.BlockSpec((1, D, Fh), lambda e, b: (e0 + e, 0, 0)),
                  pl.BlockSpec((1, Fh, D), lambda e, b: (e0 + e, 0, 0))],
        out_specs=pl.BlockSpec((1, 1, cap, D), lambda e, b: (b, e, 0, 0)),
        out_shape=jax.ShapeDtypeStruct((B, E, cap, D), BF16),
        compiler_params=_params(("parallel", "arbitrary")),
        name="moe_ffn",
    )(xs, wg, wu, wd)


def _combine_kernel(los_ref, pos_ref, gate_ref, y_ref, x_ref, g_ref, lg_ref, lb_ref, o_ref, acc, *, cap, nt):
    b, i, e = pl.program_id(0), pl.program_id(1), pl.program_id(2)
    base = (b * pl.num_programs(2) + e) * (nt + 1) + i
    lo = los_ref[base]
    hi = los_ref[base + 1]
    pos = pos_ref[0].astype(F32)
    gate = gate_ref[0]
    tt, E = pos.shape
    lane = lax.broadcasted_iota(I32, (tt, E), 1)
    pcol = jnp.sum(jnp.where(lane == e, pos, 0.0), axis=1, keepdims=True)
    gcol = jnp.sum(jnp.where(lane == e, gate, 0.0), axis=1, keepdims=True)
    win, nwin = _slot_windows(cap, MXU_DEPTH)

    @pl.when(e == 0)
    def _():
        acc[...] = jnp.zeros(acc.shape, F32)

    for w in range(nwin):
        @pl.when(jnp.logical_and(lo < (w + 1) * win, hi > w * win))
        def _():
            slot = (lax.broadcasted_iota(I32, (tt, win), 1) + w * win).astype(F32)
            onehot = jnp.where(slot == pcol, 1.0, 0.0).astype(BF16)
            acc[...] += gcol * jnp.dot(onehot, y_ref[0, 0, w * win:(w + 1) * win, :], preferred_element_type=F32)

    @pl.when(e == pl.num_programs(2) - 1)
    def _():
        v = DEEPNORM_ALPHA * x_ref[...] + g_ref[0] * acc[...]
        o_ref[...] = _layer_norm_rows(v, lg_ref[...], lb_ref[...])


def moe_combine(los, pos_tm, aff_tm, yd, x, g, lg, lb, B, T, cap, ts):
    E = pos_tm.shape[-1]
    M, D = x.shape
    nt = T // ts
    return pl.pallas_call(
        functools.partial(_combine_kernel, cap=cap, nt=nt),
        grid_spec=pltpu.PrefetchScalarGridSpec(
            num_scalar_prefetch=1,
            grid=(B, nt, E),
            in_specs=[pl.BlockSpec((1, ts, E), lambda b, i, e, s: (b, i, 0)),
                      pl.BlockSpec((1, ts, E), lambda b, i, e, s: (b, i, 0)),
                      pl.BlockSpec((1, 1, cap, D), lambda b, i, e, s: (b, e, 0, 0)),
                      pl.BlockSpec((ts, D), lambda b, i, e, s: (b * nt + i, 0)),
                      pl.BlockSpec((1, 1, D), lambda b, i, e, s: (b, 0, 0)),
                      pl.BlockSpec((1, D), lambda b, i, e, s: (0, 0)),
                      pl.BlockSpec((1, D), lambda b, i, e, s: (0, 0))],
            out_specs=pl.BlockSpec((ts, D), lambda b, i, e, s: (b * nt + i, 0)),
            scratch_shapes=[pltpu.VMEM((ts, D), F32)]),
        out_shape=jax.ShapeDtypeStruct((M, D), F32),
        compiler_params=_params(("parallel", "parallel", "arbitrary")),
        name="moe_combine",
    )(los, pos_tm, aff_tm, yd, x, g, lg, lb)


def moe_layer(x, sc, sh, g, lg, lb, w_router, w_gate, w_up, w_down, layer, B, T):
    cap = CAPACITY_FACTOR * T // N_EXPERTS
    ts = min(512, T)
    nt = T // ts
    h, aff_t = moe_router_op(x, sc, sh, w_router.T, B, T)
    pos, los = moe_topk(aff_t, cap, ts)
    los = los[:, :, :nt + 1].reshape(-1)
    xs = moe_gather(los, pos[:, :, None, :], h, B, T, cap, ts)
    yd = moe_ffn(xs, w_gate, w_up, w_down, layer)
    pos_tm = jnp.swapaxes(pos, 1, 2)
    aff_tm = jnp.swapaxes(aff_t, 1, 2)
    return moe_combine(los, pos_tm, aff_tm, yd, x, g, lg, lb, B, T, cap, ts)


def _pad_cols(w, n):
    return jnp.pad(w, ((0, 0), (0, n - w.shape[1])))


def _rot_cols(w):
    half = w.shape[-1] // 2
    return jnp.concatenate([-w[..., half:], w[..., :half]], axis=-1)


def _even_w_in(w):
    D = w.shape[0]
    o = MLA_IN
    rkv = w[:, o:o + 3 * RWKV_WIDTH]
    cq = w[:, :Q_LORA]
    ckv = w[:, Q_LORA:Q_LORA + KV_LORA]
    kr = w[:, Q_LORA + KV_LORA:MLA_IN]
    l0 = o + 3 * RWKV_WIDTH
    lora = [_pad_cols(w[:, l0 + i * DECAY_LORA:l0 + (i + 1) * DECAY_LORA], LORA_PAD) for i in range(4)]
    gd = w[:, l0 + 4 * DECAY_LORA:]
    pad = jnp.zeros((D, LANES), w.dtype)
    out = jnp.concatenate([rkv, cq, ckv, kr, _rot_cols(kr), pad] + lora + [gd], axis=1)
    assert out.shape[1] == EVEN_COLS
    return out.astype(BF16)


def _even_mu(mu):
    rkv = mu[:3 * RWKV_WIDTH]
    l0 = 3 * RWKV_WIDTH
    lora = [jnp.pad(mu[l0 + i * DECAY_LORA:l0 + (i + 1) * DECAY_LORA], (0, LORA_PAD - DECAY_LORA)) for i in range(4)]
    gd = mu[l0 + 4 * DECAY_LORA:]
    return rkv.reshape(1, -1), jnp.concatenate(lora + [gd]).reshape(1, -1)


def _pad_rows(w, n):
    return jnp.pad(w, ((0, 0), (0, n - w.shape[1]), (0, 0)))


def even_mixer(x, sc, sh, positions_tables, w_in, shift_mu, q_norm, w_uq, kv_norm, w_ukv, w0, w2, a0, a2, g2,
               k_k, k_a, r_k, gn_g, gn_b, B, T):
    cs, sn = positions_tables
    H = MLA_HEADS
    p = mm_mod(x, sc, sh, _even_w_in(w_in), T, 512, 1024, F32)
    wq = w_uq.reshape(Q_LORA, H, NOPE_DIM + ROPE_DIM)
    wq = jnp.concatenate([wq, _rot_cols(wq[..., NOPE_DIM:])], axis=-1)
    wq = jnp.swapaxes(wq, 0, 1).astype(BF16)
    wkv = w_ukv.reshape(KV_LORA, H, NOPE_DIM + V_DIM)
    wk = jnp.swapaxes(wkv[..., :NOPE_DIM], 0, 1).astype(BF16)
    wvt = jnp.transpose(wkv[..., NOPE_DIM:], (1, 2, 0)).astype(BF16)
    q, k, vt = mla_proj(p, cs, sn, q_norm.reshape(1, -1), kv_norm.reshape(1, -1), wq, wk, wvt, B, T)
    attn = mla_attention(q, k, vt)
    mu_rkv, mu_lora = _even_mu(shift_mu)
    z_rkv = shift_mix(p, mu_rkv, T, 0, 3 * RWKV_WIDTH, 1024, False)
    la = shift_mix(p, mu_lora, T, 8448, 768, 768, True)
    y_rwkv = rwkv_scan(z_rkv.reshape(B, T, -1), la.reshape(B, T, -1),
                       _pad_rows(w2, LORA_PAD).astype(BF16), _pad_rows(a2, LORA_PAD).astype(BF16),
                       g2.astype(BF16), w0, a0, k_k.reshape(1, -1), k_a.reshape(1, -1), r_k.reshape(1, -1),
                       gn_g.reshape(1, -1), gn_b.reshape(1, -1), B, T)
    return attn.reshape(B * T, -1), y_rwkv.reshape(B * T, -1)


def odd_mixer(x, sc, sh, w_in, conv_w, a_log, dt_bias, norm_g, B, T):
    M = B * T
    nC = T // CHUNK
    wb = w_in.astype(BF16)
    n_main = GDN_QKV + GDN_VAL_WIDTH
    p = mm_mod(x, sc, sh, wb[:, :n_main], T, 512, 1024, F32)
    ba = mm_mod(x, sc, sh, wb[:, n_main:], T, 512, 128, F32)
    qkv = gdn_conv_op(p, conv_w, T)
    nh = 2 * GDN_V_HEADS
    bg = gdn_gates(ba, jnp.pad(a_log.reshape(1, -1), ((0, 0), (nh, 0))),
                   jnp.pad(dt_bias.reshape(1, -1), ((0, 0), (nh, 0))), T)
    bg = bg.reshape(B, nC, CHUNK, 2, 2, GDN_K_HEADS, 2)
    bg = jnp.transpose(bg, (0, 1, 3, 4, 5, 6, 2)).reshape(B, nC, 4 * GDN_K_HEADS, PAIR)
    o = gdn_scan(qkv.reshape(B, T, -1), p.reshape(B, T, -1), GDN_QKV // (2 * GDN_D), bg, norm_g.reshape(1, -1), B, T)
    return o.reshape(M, -1)


def kernel(x, c, positions, ada_w, ada_b, ln_g, ln_b, e_w_in, e_shift_mu, mla_q_norm, mla_w_uq, mla_kv_norm,
           mla_w_ukv, rwkv_w0, rwkv_w2, rwkv_a0, rwkv_a2, rwkv_g2, rwkv_k_k, rwkv_k_a, rwkv_r_k, rwkv_gn_g,
           rwkv_gn_b, e_w_out, o_w_in, gdn_conv, gdn_a_log, gdn_dt_bias, gdn_norm, o_w_out, moe_router,
           moe_w_gate, moe_w_up, moe_w_down):
    B, T, D = x.shape
    M = B * T
    depth = ada_w.shape[0]
    mod = ada_modulation(c, ada_w, ada_b)
    tables = rope_tables(positions)
    xf = x.reshape(M, D)
    n_exp = moe_w_gate.shape[1]
    wg_all = moe_w_gate.astype(BF16).reshape(depth * n_exp, *moe_w_gate.shape[2:])
    wu_all = moe_w_up.astype(BF16).reshape(depth * n_exp, *moe_w_up.shape[2:])
    wd_all = moe_w_down.astype(BF16).reshape(depth * n_exp, *moe_w_down.shape[2:])
    for i in range(depth):
        m6 = mod[i].reshape(B, 6, 1, D)
        sh_m, sc_m, g_m, sh_f, sc_f, g_f = (m6[:, n] for n in range(6))
        j = i // 2
        lg0, lb0 = ln_g[i, 0].reshape(1, D), ln_b[i, 0].reshape(1, D)
        lg1, lb1 = ln_g[i, 1].reshape(1, D), ln_b[i, 1].reshape(1, D)
        if i % 2 == 0:
            a0, a1 = even_mixer(xf, sc_m, sh_m, tables, e_w_in[j], e_shift_mu[j], mla_q_norm[j], mla_w_uq[j],
                                mla_kv_norm[j], mla_w_ukv[j], rwkv_w0[j], rwkv_w2[j], rwkv_a0[j], rwkv_a2[j],
                                rwkv_g2[j], rwkv_k_k[j], rwkv_k_a[j], rwkv_r_k[j], rwkv_gn_g[j], rwkv_gn_b[j],
                                B, T)
            xf = mm_ln(a0, 0, a1, 0, e_w_out[j].astype(BF16), xf, g_m, lg0, lb0, T, 512)
        else:
            o = odd_mixer(xf, sc_m, sh_m, o_w_in[j], gdn_conv[j], gdn_a_log[j], gdn_dt_bias[j], gdn_norm[j], B, T)
            xf = mm_ln(o, 0, o, 1, o_w_out[j].astype(BF16), xf, g_m, lg0, lb0, T, 512)
        xf = moe_layer(xf, sc_f, sh_f, g_f, lg1, lb1, moe_router[i], wg_all, wu_all, wd_all, i, B, T)
    return xf.reshape(B, T, D)
```

```python
import functools
import math

import jax
import jax.numpy as jnp
from jax import lax
from jax.experimental import pallas as pl
from jax.experimental.pallas import tpu as pltpu

F32 = jnp.float32
BF16 = jnp.bfloat16
I32 = jnp.int32

DEPTH = 4
DEEPNORM_ALPHA = (2 * DEPTH) ** 0.25
MLA_HEADS = 16
Q_LORA = 1536
KV_LORA = 512
NOPE_DIM = 128
ROPE_DIM = 64
V_DIM = 128
ROPE_THETA = 10000.0
MLA_IN = Q_LORA + KV_LORA + ROPE_DIM
RWKV_HEADS = 32
RWKV_N = 64
RWKV_WIDTH = RWKV_HEADS * RWKV_N
DECAY_LORA = 96
AAA_LORA = 96
GATE_LORA = 256
RWKV_DECAY_SCALE = 0.6065306597126334
RWKV_GN_EPS = 64e-5
GDN_K_HEADS = 16
GDN_V_HEADS = 32
GDN_D = 128
GDN_KEY_WIDTH = GDN_K_HEADS * GDN_D
GDN_VAL_WIDTH = GDN_V_HEADS * GDN_D
GDN_QKV = 2 * GDN_KEY_WIDTH + GDN_VAL_WIDTH
CONV_WIDTH = 5
N_EXPERTS = 16
CAPACITY_FACTOR = 2

LANES = 128
MXU_DEPTH = 256
CHUNK = 64
LOG2_CHUNK = 6
PAIR = 2 * CHUNK
GROUP = 8
INV_BLOCK = 16
LOG2_INV_BLOCK = 4
LORA_PAD = 128
EVEN_COLS = 9216
VMEM_LIMIT = 56 * 1024 * 1024


def _params(sem, vmem=VMEM_LIMIT):
    return pltpu.CompilerParams(dimension_semantics=sem, vmem_limit_bytes=vmem)


def _dot(a, b):
    return jnp.dot(a.astype(BF16), b.astype(BF16), preferred_element_type=F32)


def _dot_nt(a, b):
    return lax.dot_general(a.astype(BF16), b.astype(BF16), (((1,), (1,)), ((), ())),
                           preferred_element_type=F32)


def _dot_tn(a, b):
    return lax.dot_general(a.astype(BF16), b.astype(BF16), (((0,), (0,)), ((), ())),
                           preferred_element_type=F32)


def _bmm(a, b):
    return jnp.einsum('gij,gjk->gik', a.astype(BF16), b.astype(BF16), preferred_element_type=F32)


def _bmm_nt(a, b):
    return jnp.einsum('gik,gjk->gij', a.astype(BF16), b.astype(BF16), preferred_element_type=F32)


def _split2(x):
    hi = x.astype(BF16)
    lo = (x - hi.astype(F32)).astype(BF16)
    return hi, lo


def _split3(x):
    hi = x.astype(BF16)
    r1 = x - hi.astype(F32)
    mid = r1.astype(BF16)
    lo = (r1 - mid.astype(F32)).astype(BF16)
    return hi, mid, lo


def _sigmoid(x):
    return 1.0 / (1.0 + jnp.exp(-x))


def _silu(x):
    return x * _sigmoid(x)


def _ada_kernel(c_ref, w_ref, b_ref, o_ref):
    c = c_ref[...]
    o_ref[0] = _dot(_silu(c), w_ref[0]) + b_ref[0]


def ada_modulation(c, ada_w, ada_b):
    L, D, N = ada_w.shape
    B = c.shape[0]
    rows = 8
    cp = jnp.zeros((rows, D), F32).at[:B].set(c)
    tn = 1024
    out = pl.pallas_call(
        _ada_kernel,
        grid=(L, N // tn),
        in_specs=[pl.BlockSpec((rows, D), lambda l, j: (0, 0)),
                  pl.BlockSpec((1, D, tn), lambda l, j: (l, 0, j)),
                  pl.BlockSpec((1, 1, tn), lambda l, j: (l, 0, j))],
        out_specs=pl.BlockSpec((1, rows, tn), lambda l, j: (l, 0, j)),
        out_shape=jax.ShapeDtypeStruct((L, rows, N), F32),
        compiler_params=_params(("parallel", "parallel")),
        name="ada_mod",
    )(cp, ada_w, ada_b.reshape(L, 1, N))
    return out[:, :B]


def _mm_mod_kernel(x_ref, sc_ref, sh_ref, w_ref, o_ref, h_scr):
    @pl.when(pl.program_id(1) == 0)
    def _():
        h_scr[...] = (x_ref[...] * (1.0 + sc_ref[0]) + sh_ref[0]).astype(BF16)

    o_ref[...] = jnp.dot(h_scr[...], w_ref[...], preferred_element_type=F32).astype(o_ref.dtype)


def mm_mod(x, sc, sh, w, T, tm, tn, out_dtype):
    M, D = x.shape
    N = w.shape[1]
    tm = min(tm, T)
    tn = min(tn, N)
    per_b = T // tm
    return pl.pallas_call(
        _mm_mod_kernel,
        grid=(M // tm, N // tn),
        in_specs=[pl.BlockSpec((tm, D), lambda i, j: (i, 0)),
                  pl.BlockSpec((1, 1, D), lambda i, j: (i // per_b, 0, 0)),
                  pl.BlockSpec((1, 1, D), lambda i, j: (i // per_b, 0, 0)),
                  pl.BlockSpec((D, tn), lambda i, j: (0, j))],
        out_specs=pl.BlockSpec((tm, tn), lambda i, j: (i, j)),
        out_shape=jax.ShapeDtypeStruct((M, N), out_dtype),
        scratch_shapes=[pltpu.VMEM((tm, D), BF16)],
        compiler_params=_params(("parallel", "arbitrary")),
        name="mm_mod",
    )(x, sc, sh, w)


def _layer_norm_rows(v, lg, lb):
    mu = jnp.mean(v, axis=-1, keepdims=True)
    d = v - mu
    var = jnp.mean(d * d, axis=-1, keepdims=True)
    return d * lax.rsqrt(var + 1e-5) * lg + lb


def _mm_ln_kernel(a0_ref, a1_ref, w_ref, x_ref, g_ref, lg_ref, lb_ref, o_ref, acc):
    k = pl.program_id(1)

    @pl.when(k == 0)
    def _():
        acc[...] = jnp.dot(a0_ref[...], w_ref[...], preferred_element_type=F32)

    @pl.when(k == 1)
    def _():
        y = acc[...] + jnp.dot(a1_ref[...], w_ref[...], preferred_element_type=F32)
        v = DEEPNORM_ALPHA * x_ref[...] + g_ref[0] * y
        o_ref[...] = _layer_norm_rows(v, lg_ref[...], lb_ref[...])


def mm_ln(a0, a0_blk, a1, a1_blk, w, x, g, lg, lb, T, tm):
    M, D = x.shape
    tm = min(tm, T)
    per_b = T // tm
    return pl.pallas_call(
        _mm_ln_kernel,
        grid=(M // tm, 2),
        in_specs=[pl.BlockSpec((tm, D), lambda i, k: (i, a0_blk)),
                  pl.BlockSpec((tm, D), lambda i, k: (i, a1_blk)),
                  pl.BlockSpec((D, D), lambda i, k: (k, 0)),
                  pl.BlockSpec((tm, D), lambda i, k: (i, 0)),
                  pl.BlockSpec((1, 1, D), lambda i, k: (i // per_b, 0, 0)),
                  pl.BlockSpec((1, D), lambda i, k: (0, 0)),
                  pl.BlockSpec((1, D), lambda i, k: (0, 0))],
        out_specs=pl.BlockSpec((tm, D), lambda i, k: (i, 0)),
        out_shape=jax.ShapeDtypeStruct((M, D), F32),
        scratch_shapes=[pltpu.VMEM((tm, D), F32)],
        compiler_params=_params(("parallel", "arbitrary")),
        name="mm_ln",
    )(a0, a1, w, x, g, lg, lb)


def _rope_kernel(pos_ref, inv_ref, cs_ref, sn_ref):
    ang = pos_ref[...].astype(F32) * inv_ref[...]
    cs_ref[...] = jnp.cos(ang)
    sn_ref[...] = jnp.sin(ang)


def rope_tables(positions):
    B, T = positions.shape
    M = B * T
    inv = ROPE_THETA ** (-jnp.arange(0, ROPE_DIM, 2, dtype=F32) / ROPE_DIM)
    inv2 = jnp.concatenate([inv, inv]).reshape(1, ROPE_DIM)
    tm = min(1024, T)
    return pl.pallas_call(
        _rope_kernel,
        grid=(M // tm,),
        in_specs=[pl.BlockSpec((tm, 1), lambda i: (i, 0)),
                  pl.BlockSpec((1, ROPE_DIM), lambda i: (0, 0))],
        out_specs=[pl.BlockSpec((tm, ROPE_DIM), lambda i: (i, 0)),
                   pl.BlockSpec((tm, ROPE_DIM), lambda i: (i, 0))],
        out_shape=[jax.ShapeDtypeStruct((M, ROPE_DIM), F32)] * 2,
        compiler_params=_params(("parallel",)),
        name="rope_tables",
    )(positions.reshape(M, 1), inv2)


def _rms_rows(x, g, eps=1e-6):
    return x * lax.rsqrt(jnp.mean(x * x, axis=-1, keepdims=True) + eps) * g


def _mla_proj_kernel(cq_ref, ckv_ref, kr_ref, cs_ref, sn_ref, qn_ref, kvn_ref, wq_ref, wk_ref, wvt_ref,
                     q_ref, k_ref, vt_ref, cq_scr, ckv_scr, kr_scr):
    @pl.when(pl.program_id(1) == 0)
    def _():
        cq_scr[...] = _rms_rows(cq_ref[...], qn_ref[...]).astype(BF16)
        ckv_scr[...] = _rms_rows(ckv_ref[...], kvn_ref[...]).astype(BF16)
        kr = kr_ref[...]
        kr_scr[...] = kr[:, :ROPE_DIM] * cs_ref[...] + kr[:, ROPE_DIM:] * sn_ref[...]

    scale = (NOPE_DIM + ROPE_DIM) ** -0.5 * math.log2(math.e)
    qf = jnp.dot(cq_scr[...], wq_ref[0], preferred_element_type=F32)
    q_rope = qf[:, NOPE_DIM:NOPE_DIM + ROPE_DIM] * cs_ref[...] + qf[:, NOPE_DIM + ROPE_DIM:] * sn_ref[...]
    q_ref[0, 0, :, :NOPE_DIM] = (qf[:, :NOPE_DIM] * scale).astype(BF16)
    q_ref[0, 0, :, NOPE_DIM:] = (q_rope * scale).astype(BF16)
    ckv = ckv_scr[...]
    k_ref[0, 0, :, :NOPE_DIM] = jnp.dot(ckv, wk_ref[0], preferred_element_type=F32).astype(BF16)
    k_ref[0, 0, :, NOPE_DIM:] = kr_scr[...].astype(BF16)
    vt_ref[0, 0] = lax.dot_general(wvt_ref[0], ckv, (((1,), (1,)), ((), ())),
                                   preferred_element_type=F32).astype(BF16)


def mla_proj(p, cs, sn, qn, kvn, wq, wk, wvt, B, T):
    M = B * T
    H = MLA_HEADS
    tm = min(1024, T)
    per_b = T // tm
    qk_dim = NOPE_DIM + ROPE_DIM
    out_map = lambda i, h: (i // per_b, h, i % per_b, 0)
    return pl.pallas_call(
        _mla_proj_kernel,
        grid=(M // tm, H),
        in_specs=[pl.BlockSpec((tm, Q_LORA), lambda i, h: (i, 6144 // Q_LORA)),
                  pl.BlockSpec((tm, KV_LORA), lambda i, h: (i, 7680 // KV_LORA)),
                  pl.BlockSpec((tm, 2 * ROPE_DIM), lambda i, h: (i, 8192 // (2 * ROPE_DIM))),
                  pl.BlockSpec((tm, ROPE_DIM), lambda i, h: (i, 0)),
                  pl.BlockSpec((tm, ROPE_DIM), lambda i, h: (i, 0)),
                  pl.BlockSpec((1, Q_LORA), lambda i, h: (0, 0)),
                  pl.BlockSpec((1, KV_LORA), lambda i, h: (0, 0)),
                  pl.BlockSpec((1, Q_LORA, 256), lambda i, h: (h, 0, 0)),
                  pl.BlockSpec((1, KV_LORA, NOPE_DIM), lambda i, h: (h, 0, 0)),
                  pl.BlockSpec((1, V_DIM, KV_LORA), lambda i, h: (h, 0, 0))],
        out_specs=[pl.BlockSpec((1, 1, tm, qk_dim), out_map),
                   pl.BlockSpec((1, 1, tm, qk_dim), out_map),
                   pl.BlockSpec((1, 1, V_DIM, tm), lambda i, h: (i // per_b, h, 0, i % per_b))],
        out_shape=[jax.ShapeDtypeStruct((B, H, T, qk_dim), BF16),
                   jax.ShapeDtypeStruct((B, H, T, qk_dim), BF16),
                   jax.ShapeDtypeStruct((B, H, V_DIM, T), BF16)],
        scratch_shapes=[pltpu.VMEM((tm, Q_LORA), BF16), pltpu.VMEM((tm, KV_LORA), BF16),
                        pltpu.VMEM((tm, ROPE_DIM), F32)],
        compiler_params=_params(("parallel", "arbitrary")),
        name="mla_proj",
    )(p, p, p, cs, sn, qn, kvn, wq, wk, wvt)


def _attn_kernel(q_ref, k_ref, vt_ref, o_ref, *, kc):
    q = q_ref[0, 0]
    T = k_ref.shape[2]
    nk = T // kc
    tq = q.shape[0]
    hq = min(MXU_DEPTH, tq)
    nh = tq // hq

    def scores(j):
        qj = q[j * hq:(j + 1) * hq]
        ss = [lax.dot_general(k_ref[0, 0, c * kc:(c + 1) * kc, :], qj, (((1,), (1,)), ((), ())),
                              preferred_element_type=F32) for c in range(nk)]
        m = jnp.max(ss[0], axis=0, keepdims=True)
        for c in range(1, nk):
            m = jnp.maximum(m, jnp.max(ss[c], axis=0, keepdims=True))
        return ss, m

    def finish(j, ss, m):
        acc = l = None
        for c in range(nk):
            p = jnp.exp2(ss[c] - m)
            lc = jnp.sum(p, axis=0, keepdims=True)
            ac = jnp.dot(vt_ref[0, 0, :, c * kc:(c + 1) * kc], p.astype(BF16), preferred_element_type=F32)
            l = lc if l is None else l + lc
            acc = ac if acc is None else acc + ac
        o_ref[0, j * hq:(j + 1) * hq, :] = (acc / l).T.astype(o_ref.dtype)

    prev = None
    for j in range(nh):
        cur = scores(j)
        if prev is not None:
            finish(j - 1, *prev)
        prev = cur
    finish(nh - 1, *prev)


def mla_attention(q, k, vt):
    B, H, T, dqk = q.shape
    dv = vt.shape[2]
    tq = min(1024, T)
    kc = min(1024, T)
    return pl.pallas_call(
        functools.partial(_attn_kernel, kc=kc),
        grid=(B, H, T // tq),
        in_specs=[pl.BlockSpec((1, 1, tq, dqk), lambda b, h, i: (b, h, i, 0)),
                  pl.BlockSpec((1, 1, T, dqk), lambda b, h, i: (b, h, 0, 0)),
                  pl.BlockSpec((1, 1, dv, T), lambda b, h, i: (b, h, 0, 0))],
        out_specs=pl.BlockSpec((1, tq, dv), lambda b, h, i: (b, i, h)),
        out_shape=jax.ShapeDtypeStruct((B, T, H * dv), BF16),
        compiler_params=_params(("parallel", "parallel", "parallel")),
        name="mla_attn",
    )(q, k, vt)


def _shift_rows(cur, prev8, next8, d):
    tm = cur.shape[0]
    if d == 0:
        return cur
    main = pltpu.roll(cur, (-d) % tm, axis=0)
    first = pltpu.roll(jnp.concatenate([prev8, cur[:16]], axis=0), (-d) % 24, axis=0)[8:16]
    last = pltpu.roll(jnp.concatenate([cur[tm - 16:], next8], axis=0), (-d) % 24, axis=0)[8:16]
    return jnp.concatenate([first, main[8:tm - 8], last], axis=0)


def _halo_specs(tm, cw, per_b, nrow8, col_of):
    r8 = tm // 8
    cur = pl.BlockSpec((tm, cw), lambda i, j: (i, col_of(j)))
    prv = pl.BlockSpec((8, cw), lambda i, j: (jnp.maximum(i * r8 - 1, 0), col_of(j)))
    nxt = pl.BlockSpec((8, cw), lambda i, j: (jnp.minimum((i + 1) * r8, nrow8 - 1), col_of(j)))
    return cur, prv, nxt


def _halo_load(cur_ref, prev_ref, next_ref, per_b):
    ti = pl.program_id(0) % per_b
    cur = cur_ref[...].astype(F32)
    prev8 = prev_ref[...].astype(F32) * jnp.where(ti == 0, 0.0, 1.0)
    next8 = next_ref[...].astype(F32) * jnp.where(ti == per_b - 1, 0.0, 1.0)
    return cur, prev8, next8


def _pair_masks(rev):
    i = lax.broadcasted_iota(I32, (PAIR, PAIR), 0)
    j = lax.broadcasted_iota(I32, (PAIR, PAIR), 1)
    same = lax.shift_right_logical(i, LOG2_CHUNK) == lax.shift_right_logical(j, LOG2_CHUNK)
    if rev:
        strict = jnp.logical_and(same, i < j)
        incl = jnp.logical_and(same, i <= j)
    else:
        strict = jnp.logical_and(same, i > j)
        incl = jnp.logical_and(same, i >= j)
    eye_b = i == j
    same_blk = lax.shift_right_logical(i, LOG2_INV_BLOCK) == lax.shift_right_logical(j, LOG2_INV_BLOCK)
    return strict, incl, eye_b, same_blk


def _unit_tri_inverse(nmat, eye, same_blk):
    n = nmat.shape[-1]
    nd = jnp.where(same_blk, nmat, 0.0)
    noff = nmat - nd
    x = eye + nd
    p = _bmm(nd, nd)
    yield
    for s in range(1, LOG2_INV_BLOCK):
        if s < LOG2_INV_BLOCK - 1:
            xp = _bmm(jnp.concatenate([x, p], axis=1), p)
            x = x + xp[:, :n]
            p = xp[:, n:]
        else:
            x = x + _bmm(x, p)
        yield
    y = _bmm(x, noff)
    yield
    acc = x
    for _ in range(CHUNK // INV_BLOCK - 1):
        acc = x + _bmm(y, acc)
        yield
    return acc


def _aligned(start, m):
    return start if isinstance(start, int) else pl.multiple_of(start, m)


def _interleave(n_steps, step_fn, gens, stages_per_step):
    for i in range(n_steps):
        step_fn(i)
        for _ in range(stages_per_step):
            for g in gens:
                next(g, None)
    for g in gens:
        for _ in g:
            pass


def _rows_to_cols(rows, eye_b):
    return jnp.sum(jnp.where(eye_b, rows, 0.0), axis=2, keepdims=True)


def _stack_heads(x, head0):
    return jnp.concatenate([jnp.where(head0, x, 0.0), jnp.where(head0, 0.0, x)], axis=1)


def _shift_mix_kernel(cur_ref, prev_ref, next_ref, mu_ref, o_ref, *, per_b, lora):
    cur, prev8, next8 = _halo_load(cur_ref, prev_ref, next_ref, per_b)
    prv = _shift_rows(cur, prev8, next8, -1)
    nxt = _shift_rows(cur, prev8, next8, 1)
    z = cur + (0.5 * (prv + nxt) - cur) * mu_ref[...]
    if lora:
        col = lax.broadcasted_iota(I32, z.shape, 1)
        z = jnp.where(col < 2 * LORA_PAD, jnp.tanh(z), jnp.where(col >= 4 * LORA_PAD, _sigmoid(z), z))
    o_ref[...] = z.astype(o_ref.dtype)


def shift_mix(p, mu, T, col0, width, cw, lora):
    M = p.shape[0]
    tm = min(512, T)
    per_b = T // tm
    c0 = col0 // cw
    cur, prv, nxt = _halo_specs(tm, cw, per_b, M // 8, lambda j: c0 + j)
    return pl.pallas_call(
        functools.partial(_shift_mix_kernel, per_b=per_b, lora=lora),
        grid=(M // tm, width // cw),
        in_specs=[cur, prv, nxt, pl.BlockSpec((1, cw), lambda i, j: (0, j))],
        out_specs=pl.BlockSpec((tm, cw), lambda i, j: (i, j)),
        out_shape=jax.ShapeDtypeStruct((M, width), BF16),
        compiler_params=_params(("parallel", "parallel")),
        name="rwkv_shift_mix",
    )(p, p, p, mu)


def _rwkv_prepare(g, G, d, rev, r_ref, k_ref, v_ref, la_ref, w2_ref, a2_ref, w0_ref, a0_ref, kk_ref, ka_ref,
                  p16_ref, p32_ref, bkt_ref, bon_ref):
    C = CHUNK
    rows = pl.ds(_aligned(g * (G * C), G * C), G * C)
    r = r_ref[0, rows, :].astype(F32)
    k = k_ref[0, rows, :].astype(F32)
    v = v_ref[0, rows, :].astype(F32)
    wd = la_ref[0, rows, d * LORA_PAD:(d + 1) * LORA_PAD]
    ad = la_ref[0, rows, (2 + d) * LORA_PAD:(3 + d) * LORA_PAD]
    lw = -RWKV_DECAY_SCALE * _sigmoid(w0_ref[d:d + 1, :] + jnp.dot(wd, w2_ref[d], preferred_element_type=F32))
    a_sig = _sigmoid(a0_ref[d:d + 1, :] + jnp.dot(ad, a2_ref[d], preferred_element_type=F32))

    lane = lax.broadcasted_iota(I32, (1, LANES), 1)
    head0 = lane < RWKV_N
    kx = k * kk_ref[...]
    sq = kx * kx
    ss = jnp.where(head0, jnp.sum(jnp.where(head0, sq, 0.0), axis=1, keepdims=True),
                   jnp.sum(jnp.where(head0, 0.0, sq), axis=1, keepdims=True))
    kk = kx * lax.rsqrt(ss + 1e-6)
    k_dir = k * (1.0 + (a_sig - 1.0) * ka_ref[...])
    b = kk * a_sig
    a = -kk
    bon_ref[d, rows, :] = r * k_dir
    yield

    g3 = lambda t: t.reshape(G, C, LANES)
    lw3, r3, a3, b3, kd3, v3 = g3(lw), g3(r), g3(a), g3(b), g3(k_dir), g3(v)
    ci = lax.broadcasted_iota(I32, (C, C), 0)
    cj = lax.broadcasted_iota(I32, (C, C), 1)
    tri = jnp.where((ci <= cj) if rev else (ci >= cj), 1.0, 0.0).astype(BF16)
    tri = jnp.broadcast_to(tri[None], (G, C, C))
    hi, lo = _split2(lw3)
    cum = (jnp.einsum('gij,gjk->gik', tri, hi, preferred_element_type=F32)
           + jnp.einsum('gij,gjk->gik', tri, lo, preferred_element_type=F32))
    yield
    end = 0 if rev else C - 1
    cum_end = cum[:, end:end + 1, :]
    e_neg = jnp.exp(-cum)
    e_tail = jnp.exp(cum_end - cum)
    h0 = head0.reshape(1, 1, LANES)
    a2s = _stack_heads(a3 * jnp.exp(cum - lw3), h0)
    r2s = _stack_heads(r3 * jnp.exp(cum), h0)
    k2s = _stack_heads(kd3 * e_neg, h0)
    b2s = _stack_heads(b3 * e_neg, h0)
    v2s = _stack_heads(v3, h0)
    kh2 = _stack_heads(kd3 * e_tail, h0)
    bh2 = _stack_heads(b3 * e_tail, h0)

    strict, incl, eye_b, same_blk = _pair_masks(rev)
    eye = jnp.where(eye_b, 1.0, 0.0)
    gram = _bmm_nt(jnp.concatenate([a2s, r2s], axis=1), jnp.concatenate([k2s, b2s], axis=1))
    a_ak = jnp.where(strict, gram[:, :PAIR, :PAIR], 0.0)
    a_ab = jnp.where(strict, gram[:, :PAIR, PAIR:], 0.0)
    a_rk = jnp.where(incl, gram[:, PAIR:, :PAIR], 0.0)
    a_rb = jnp.where(incl, gram[:, PAIR:, PAIR:], 0.0).astype(BF16)
    p16_ref[d, :, 2 * PAIR:4 * PAIR, :] = jnp.concatenate([v2s.astype(BF16), a_rb], axis=1)
    bkt_ref[d] = jnp.swapaxes(jnp.concatenate([bh2, kh2], axis=1), 1, 2).astype(BF16)
    p32_ref[d, :, 2 * PAIR:3 * PAIR, :] = jnp.broadcast_to(_rows_to_cols(jnp.exp(cum_end), eye_b), (G, PAIR, LANES))
    yield
    kv = _bmm(jnp.concatenate([a_ak, a_rk], axis=1), v2s)
    p32_ref[d, :, PAIR:2 * PAIR, :] = kv[:, PAIR:]
    yield
    tinv = yield from _unit_tri_inverse(a_ab, eye, same_blk)
    tt = _bmm(tinv, jnp.concatenate([a2s, kv[:, :PAIR]], axis=2))
    p16_ref[d, :, 0:2 * PAIR, :] = jnp.concatenate([tt[:, :, :LANES], r2s], axis=1).astype(BF16)
    p32_ref[d, :, 0:PAIR, :] = tt[:, :, LANES:]


RWKV_P16_ROWS = 4 * PAIR
RWKV_P32_ROWS = 3 * PAIR
RWKV_PREP_STAGES = 14


def _rwkv_step(H, p16_ref, p32_ref, bkt_ref, d, i):
    C = CHUNK
    lhs = p16_ref[d, i, 0:2 * PAIR, :]
    v2 = p16_ref[d, i, 2 * PAIR:3 * PAIR, :]
    a_rb = p16_ref[d, i, 3 * PAIR:4 * PAIR, :]
    sr = _dot(lhs, H)
    u2 = sr[:PAIR] + p32_ref[d, i, 0:PAIR, :]
    y2 = sr[PAIR:] + p32_ref[d, i, PAIR:2 * PAIR, :] + _dot(a_rb, u2)
    y = y2[:C] + y2[C:]
    upd = _dot(bkt_ref[d, i], jnp.concatenate([u2.astype(BF16), v2], axis=0))
    return H * p32_ref[d, i, 2 * PAIR:3 * PAIR, :] + upd, y


def _rwkv_scan_kernel(r_ref, k_ref, v_ref, la_ref, w2_ref, a2_ref, g2_ref, w0_ref, a0_ref,
                      kk_ref, ka_ref, rk_ref, gng_ref, gnb_ref, o_ref, yf_scr, yb_scr, bon_scr,
                      a16, a32, akt, b16, b32, bkt, *, nC, G):
    C = CHUNK
    nG = nC // G
    GC = G * C
    refs = (r_ref, k_ref, v_ref, la_ref, w2_ref, a2_ref, w0_ref, a0_ref, kk_ref, ka_ref)

    def prepare(g, bufs):
        return [_rwkv_prepare(g, G, 0, False, *refs, *bufs, bon_scr),
                _rwkv_prepare(nG - 1 - g, G, 1, True, *refs, *bufs, bon_scr)]

    def run(g, carry, cur, nxt_g, nxt):
        state = list(carry)
        gb = nG - 1 - g

        def step(i):
            ib = G - 1 - i
            state[0], yf = _rwkv_step(state[0], *cur, 0, i)
            state[1], yb = _rwkv_step(state[1], *cur, 1, ib)
            yf_scr[pl.ds(_aligned(g * GC + i * C, C), C), :] = yf
            yb_scr[pl.ds(_aligned(gb * GC + ib * C, C), C), :] = yb

        gens = [] if nxt_g is None else prepare(nxt_g, nxt)
        _interleave(G, step, gens, RWKV_PREP_STAGES // G + 1)
        return tuple(state)

    bufs_a, bufs_b = (a16, a32, akt), (b16, b32, bkt)
    _interleave(0, None, prepare(0, bufs_a), 0)

    def body(h, carry):
        g = 2 * h
        carry = run(g, carry, bufs_a, g + 1, bufs_b)
        return run(g + 1, carry, bufs_b, g + 2, bufs_a)

    z = jnp.zeros((LANES, LANES), F32)
    carry = lax.fori_loop(0, nG // 2 - 1, body, (z, z))
    carry = run(nG - 2, carry, bufs_a, nG - 1, bufs_b)
    run(nG - 1, carry, bufs_b, None, None)

    def fin(g, _):
        sl = pl.ds(_aligned(g * GC, GC), GC)
        y = yf_scr[sl, :] + yb_scr[sl, :]
        lane = lax.broadcasted_iota(I32, (1, LANES), 1)
        head0 = lane < RWKV_N

        def seg_sum(x):
            s0 = jnp.sum(jnp.where(head0, x, 0.0), axis=1, keepdims=True)
            s1 = jnp.sum(jnp.where(head0, 0.0, x), axis=1, keepdims=True)
            return jnp.where(head0, s0, s1)

        mu = seg_sum(y) * (1.0 / RWKV_N)
        dlt = y - mu
        var = seg_sum(dlt * dlt) * (1.0 / RWKV_N)
        yn = dlt * lax.rsqrt(var + RWKV_GN_EPS) * gng_ref[...] + gnb_ref[...]
        bsum = seg_sum((bon_scr[0, sl, :] + bon_scr[1, sl, :]) * rk_ref[...])
        v = v_ref[0, sl, :].astype(F32)
        gate = jnp.dot(la_ref[0, sl, 4 * LORA_PAD:], g2_ref[...], preferred_element_type=F32)
        o_ref[0, sl, :] = ((yn + bsum * v) * gate).astype(o_ref.dtype)
        return 0

    lax.fori_loop(0, nG, fin, 0)


def rwkv_scan(z_rkv, la, w2, a2, g2, w0, a0, k_k, k_a, r_k, gn_g, gn_b, B, T):
    nC = T // CHUNK
    G = min(GROUP, nC // 2)
    assert (nC // G) % 2 == 0
    npair = RWKV_WIDTH // LANES
    nl = la.shape[-1]
    vec = lambda: pl.BlockSpec((1, LANES), lambda b, p: (0, p))
    return pl.pallas_call(
        functools.partial(_rwkv_scan_kernel, nC=nC, G=G),
        grid=(B, npair),
        in_specs=[pl.BlockSpec((1, T, LANES), lambda b, p: (b, 0, p)),
                  pl.BlockSpec((1, T, LANES), lambda b, p: (b, 0, npair + p)),
                  pl.BlockSpec((1, T, LANES), lambda b, p: (b, 0, 2 * npair + p)),
                  pl.BlockSpec((1, T, nl), lambda b, p: (b, 0, 0)),
                  pl.BlockSpec((2, LORA_PAD, LANES), lambda b, p: (0, 0, p)),
                  pl.BlockSpec((2, LORA_PAD, LANES), lambda b, p: (0, 0, p)),
                  pl.BlockSpec((GATE_LORA, LANES), lambda b, p: (0, p)),
                  pl.BlockSpec((2, LANES), lambda b, p: (0, p)),
                  pl.BlockSpec((2, LANES), lambda b, p: (0, p)),
                  vec(), vec(), vec(), vec(), vec()],
        out_specs=pl.BlockSpec((1, T, LANES), lambda b, p: (b, 0, p)),
        out_shape=jax.ShapeDtypeStruct((B, T, RWKV_WIDTH), BF16),
        scratch_shapes=[pltpu.VMEM((T, LANES), F32), pltpu.VMEM((T, LANES), F32),
                        pltpu.VMEM((2, T, LANES), F32)]
        + 2 * [pltpu.VMEM((2, G, RWKV_P16_ROWS, LANES), BF16), pltpu.VMEM((2, G, RWKV_P32_ROWS, LANES), F32),
               pltpu.VMEM((2, G, LANES, 2 * PAIR), BF16)],
        compiler_params=_params(("parallel", "arbitrary")),
        name="rwkv_scan",
    )(z_rkv, z_rkv, z_rkv, la, w2, a2, g2, w0, a0, k_k, k_a, r_k, gn_g, gn_b)


def _gdn_conv_kernel(cur_ref, prev_ref, next_ref, w_ref, o_ref, *, per_b, q_blocks, k_blocks):
    cur, prev8, next8 = _halo_load(cur_ref, prev_ref, next_ref, per_b)
    half = CONV_WIDTH // 2
    acc = jnp.zeros_like(cur)
    for i in range(CONV_WIDTH):
        acc = acc + _shift_rows(cur, prev8, next8, i - half) * w_ref[i:i + 1, :]
    y = _silu(acc)
    j = pl.program_id(1)
    cw = y.shape[1]
    outs = []
    for s in range(cw // GDN_D):
        ys = y[:, s * GDN_D:(s + 1) * GDN_D]
        nrm = lax.rsqrt(jnp.sum(ys * ys, axis=1, keepdims=True) + 1e-6)
        outs.append(ys * nrm)
    yn = jnp.concatenate(outs, axis=1) if len(outs) > 1 else outs[0]
    res = jnp.where(j < q_blocks, yn * (GDN_D ** -0.5), jnp.where(j < k_blocks, yn, y))
    o_ref[...] = res.astype(o_ref.dtype)


def gdn_conv_op(p, conv_w, T):
    M = p.shape[0]
    tm = min(512, T)
    cw = 512
    per_b = T // tm
    cur, prv, nxt = _halo_specs(tm, cw, per_b, M // 8, lambda j: j)
    return pl.pallas_call(
        functools.partial(_gdn_conv_kernel, per_b=per_b, q_blocks=GDN_KEY_WIDTH // cw,
                          k_blocks=2 * GDN_KEY_WIDTH // cw),
        grid=(M // tm, GDN_QKV // cw),
        in_specs=[cur, prv, nxt, pl.BlockSpec((CONV_WIDTH, cw), lambda i, j: (0, j))],
        out_specs=pl.BlockSpec((tm, cw), lambda i, j: (i, j)),
        out_shape=jax.ShapeDtypeStruct((M, GDN_QKV), BF16),
        compiler_params=_params(("parallel", "parallel")),
        name="gdn_conv",
    )(p, p, p, conv_w)


def _gdn_gate_kernel(ba_ref, alog_ref, dtb_ref, o_ref):
    C = CHUNK
    x = ba_ref[...]
    tm = x.shape[0]
    nh = 2 * GDN_V_HEADS
    beta = _sigmoid(x)
    a = x + dtb_ref[...]
    sp = jnp.maximum(a, 0.0) + jnp.log1p(jnp.exp(-jnp.abs(a)))
    g = -jnp.exp(alog_ref[...]) * sp
    i = lax.broadcasted_iota(I32, (C, C), 0)
    j = lax.broadcasted_iota(I32, (C, C), 1)
    lower = jnp.where(i >= j, 1.0, 0.0).astype(BF16)
    upper = jnp.where(i <= j, 1.0, 0.0).astype(BF16)
    col = lax.broadcasted_iota(I32, (C, 2 * nh), 1)
    for c in range(tm // C):
        gc = g[c * C:(c + 1) * C]
        hi, mid, lo = _split3(gc)
        fw = (jnp.dot(lower, hi, preferred_element_type=F32) + jnp.dot(lower, mid, preferred_element_type=F32)
              + jnp.dot(lower, lo, preferred_element_type=F32))
        bw = (jnp.dot(upper, hi, preferred_element_type=F32) + jnp.dot(upper, mid, preferred_element_type=F32)
              + jnp.dot(upper, lo, preferred_element_type=F32))
        o_ref[c * C:(c + 1) * C, :] = jnp.where(col < nh, beta[c * C:(c + 1) * C],
                                                jnp.where(col < nh + GDN_V_HEADS, fw, bw))


def gdn_gates(ba, a_log, dt_bias, T):
    M = ba.shape[0]
    tm = min(512, T)
    nh = 2 * GDN_V_HEADS
    return pl.pallas_call(
        _gdn_gate_kernel,
        grid=(M // tm,),
        in_specs=[pl.BlockSpec((tm, 2 * nh), lambda i: (i, 0)),
                  pl.BlockSpec((1, 2 * nh), lambda i: (0, 0)),
                  pl.BlockSpec((1, 2 * nh), lambda i: (0, 0))],
        out_specs=pl.BlockSpec((tm, 2 * nh), lambda i: (i, 0)),
        out_shape=jax.ShapeDtypeStruct((M, 2 * nh), F32),
        compiler_params=_params(("parallel",)),
        name="gdn_gates",
    )(ba, a_log, dt_bias)


def _gdn_prepare(g, G, d, rev, kh, q_ref, k_ref, v_ref, bg_ref, p16_ref, u_ref, dec_ref):
    C = CHUNK
    rows = pl.ds(_aligned(g * (G * C), G * C), G * C)
    q = q_ref[0, rows, :].astype(F32).reshape(G, C, GDN_D)
    k = k_ref[0, rows, :].astype(F32).reshape(G, C, GDN_D)
    v = v_ref[0, rows, :].astype(F32).reshape(G, C, 2 * GDN_D)
    v2 = jnp.concatenate([v[:, :, :GDN_D], v[:, :, GDN_D:]], axis=1)
    k2 = jnp.concatenate([k, k], axis=1)
    q2 = jnp.concatenate([q, q], axis=1)
    chunks = pl.ds(g * G, G)
    beta_row = bg_ref[0, chunks, pl.ds(d * GDN_K_HEADS + kh, 1), :]
    gc_row = bg_ref[0, chunks, pl.ds((2 + d) * GDN_K_HEADS + kh, 1), :]
    strict, incl, eye_b, same_blk = _pair_masks(rev)
    eye = jnp.where(eye_b, 1.0, 0.0)
    beta_col = _rows_to_cols(beta_row, eye_b)
    gc_col = _rows_to_cols(gc_row, eye_b)
    decay = jnp.where(incl, jnp.exp(jnp.where(incl, gc_col - gc_row, 0.0)), 0.0)
    gram = _bmm_nt(jnp.concatenate([k2, q2], axis=1), k2)
    lower = jnp.where(strict, gram[:, :PAIR] * beta_col * decay, 0.0)
    eg = jnp.exp(gc_col)
    end = 0 if rev else C - 1
    gl0 = gc_row[:, :, end:end + 1]
    gl1 = gc_row[:, :, C + end:C + end + 1]
    rowi = lax.broadcasted_iota(I32, (1, PAIR, 1), 1)
    gl_col = jnp.where(rowi < C, gl0, gl1)
    lane2 = lax.broadcasted_iota(I32, (1, 1, 2 * GDN_D), 2)
    decay_s = jnp.where(lane2 < GDN_D, jnp.exp(gl0), jnp.exp(gl1))
    p16_ref[d, :, PAIR:2 * PAIR, :] = (q2 * eg).astype(BF16)
    p16_ref[d, :, 2 * PAIR:3 * PAIR, :] = (gram[:, PAIR:] * decay).astype(BF16)
    p16_ref[d, :, 3 * PAIR:4 * PAIR, :] = jnp.swapaxes(k2 * jnp.exp(gl_col - gc_col), 1, 2).astype(BF16)
    dec_ref[d] = jnp.broadcast_to(decay_s, (G, 8, 2 * GDN_D))
    yield
    t_inv = yield from _unit_tri_inverse(-lower, eye, same_blk)
    kb = k2 * beta_col
    uw = _bmm(t_inv, jnp.concatenate([v2 * beta_col, kb * eg], axis=2))
    u_ref[d] = uw[:, :, :GDN_D]
    p16_ref[d, :, 0:PAIR, :] = uw[:, :, GDN_D:].astype(BF16)


GDN_P16_ROWS = 4 * PAIR
GDN_PREP_STAGES = 10


def _gdn_step(S, p16_ref, u_ref, dec_ref, d, i):
    C = CHUNK
    ws = _dot(p16_ref[d, i, 0:2 * PAIR, :], S)
    head0_rows = lax.broadcasted_iota(I32, (PAIR, 1), 0) < C
    w_s = jnp.where(head0_rows, ws[:PAIR, :GDN_D], ws[:PAIR, GDN_D:])
    q_s = jnp.where(head0_rows, ws[PAIR:, :GDN_D], ws[PAIR:, GDN_D:])
    v_new = u_ref[d, i] - w_s
    o2 = q_s + _dot(p16_ref[d, i, 2 * PAIR:3 * PAIR, :], v_new)
    vcat = jnp.concatenate([jnp.where(head0_rows, v_new, 0.0), jnp.where(head0_rows, 0.0, v_new)], axis=1)
    S_new = S * dec_ref[d, i, 0:1, :] + _dot(p16_ref[d, i, 3 * PAIR:4 * PAIR, :], vcat)
    return S_new, jnp.concatenate([o2[:C], o2[C:]], axis=1)


def _gdn_scan_kernel(q_ref, k_ref, v_ref, z_ref, bg_ref, ng_ref, o_ref, of_scr, ob_scr,
                     a16, au, ad, b16, bu, bd, *, nC, G):
    C = CHUNK
    nG = nC // G
    GC = G * C
    kh = pl.program_id(1)

    def prepare(g, bufs):
        return [_gdn_prepare(g, G, 0, False, kh, q_ref, k_ref, v_ref, bg_ref, *bufs),
                _gdn_prepare(nG - 1 - g, G, 1, True, kh, q_ref, k_ref, v_ref, bg_ref, *bufs)]

    def run(g, carry, cur, nxt_g, nxt):
        state = list(carry)
        gb = nG - 1 - g

        def step(i):
            ib = G - 1 - i
            state[0], of = _gdn_step(state[0], *cur, 0, i)
            state[1], ob = _gdn_step(state[1], *cur, 1, ib)
            of_scr[pl.ds(_aligned(g * GC + i * C, C), C), :] = of
            ob_scr[pl.ds(_aligned(gb * GC + ib * C, C), C), :] = ob

        gens = [] if nxt_g is None else prepare(nxt_g, nxt)
        _interleave(G, step, gens, GDN_PREP_STAGES // G + 1)
        return tuple(state)

    bufs_a, bufs_b = (a16, au, ad), (b16, bu, bd)
    _interleave(0, None, prepare(0, bufs_a), 0)

    def body(h, carry):
        g = 2 * h
        carry = run(g, carry, bufs_a, g + 1, bufs_b)
        return run(g + 1, carry, bufs_b, g + 2, bufs_a)

    z0 = jnp.zeros((GDN_D, 2 * GDN_D), F32)
    carry = lax.fori_loop(0, nG // 2 - 1, body, (z0, z0))
    carry = run(nG - 2, carry, bufs_a, nG - 1, bufs_b)
    run(nG - 1, carry, bufs_b, None, None)

    def fin(g, _):
        sl = pl.ds(_aligned(g * GC, GC), GC)
        o = of_scr[sl, :] + ob_scr[sl, :]
        z = z_ref[0, sl, :].astype(F32)
        for s in range(2):
            os_ = o[:, s * GDN_D:(s + 1) * GDN_D]
            on = os_ * lax.rsqrt(jnp.mean(os_ * os_, axis=-1, keepdims=True) + 1e-6) * ng_ref[...]
            o_ref[0, sl, s * GDN_D:(s + 1) * GDN_D] = (on * _silu(z[:, s * GDN_D:(s + 1) * GDN_D])).astype(o_ref.dtype)
        return 0

    lax.fori_loop(0, nG, fin, 0)


def gdn_scan(qkv, z_src, z_col0, bg, norm_g, B, T):
    nC = T // CHUNK
    G = min(GROUP, nC // 2)
    assert (nC // G) % 2 == 0
    kb = GDN_KEY_WIDTH // GDN_D
    vb = 2 * GDN_KEY_WIDTH // (2 * GDN_D)
    return pl.pallas_call(
        functools.partial(_gdn_scan_kernel, nC=nC, G=G),
        grid=(B, GDN_K_HEADS),
        in_specs=[pl.BlockSpec((1, T, GDN_D), lambda b, h: (b, 0, h)),
                  pl.BlockSpec((1, T, GDN_D), lambda b, h: (b, 0, kb + h)),
                  pl.BlockSpec((1, T, 2 * GDN_D), lambda b, h: (b, 0, vb + h)),
                  pl.BlockSpec((1, T, 2 * GDN_D), lambda b, h: (b, 0, z_col0 + h)),
                  pl.BlockSpec((1, nC, 4 * GDN_K_HEADS, PAIR), lambda b, h: (b, 0, 0, 0)),
                  pl.BlockSpec((1, GDN_D), lambda b, h: (0, 0))],
        out_specs=pl.BlockSpec((1, T, 2 * GDN_D), lambda b, h: (b, 0, h)),
        out_shape=jax.ShapeDtypeStruct((B, T, GDN_VAL_WIDTH), BF16),
        scratch_shapes=[pltpu.VMEM((T, 2 * GDN_D), F32), pltpu.VMEM((T, 2 * GDN_D), F32)]
        + 2 * [pltpu.VMEM((2, G, GDN_P16_ROWS, GDN_D), BF16), pltpu.VMEM((2, G, PAIR, GDN_D), F32),
               pltpu.VMEM((2, G, 8, 2 * GDN_D), F32)],
        compiler_params=_params(("parallel", "arbitrary")),
        name="gdn_scan",
    )(qkv, qkv, qkv, z_src, bg, norm_g)


def _router_kernel(x_ref, sc_ref, sh_ref, wr_ref, h_ref, aff_ref):
    h = x_ref[...] * (1.0 + sc_ref[0]) + sh_ref[0]
    h_ref[...] = h.astype(BF16)
    hh, hm, hl = _split3(h)
    wh, wm, wl = _split3(wr_ref[...])
    nt = lambda a, b: lax.dot_general(a, b, (((1,), (1,)), ((), ())), preferred_element_type=F32)
    logits = (nt(wh, hh) + (nt(wh, hm) + nt(wm, hh)) + (nt(wh, hl) + nt(wl, hh) + nt(wm, hm)))
    m = jnp.max(logits, axis=0, keepdims=True)
    e = jnp.exp(logits - m)
    aff_ref[0] = e / jnp.sum(e, axis=0, keepdims=True)


def moe_router_op(x, sc, sh, wr_t, B, T):
    M, D = x.shape
    E = wr_t.shape[0]
    tm = min(512, T)
    per_b = T // tm
    return pl.pallas_call(
        _router_kernel,
        grid=(M // tm,),
        in_specs=[pl.BlockSpec((tm, D), lambda i: (i, 0)),
                  pl.BlockSpec((1, 1, D), lambda i: (i // per_b, 0, 0)),
                  pl.BlockSpec((1, 1, D), lambda i: (i // per_b, 0, 0)),
                  pl.BlockSpec((E, D), lambda i: (0, 0))],
        out_specs=[pl.BlockSpec((tm, D), lambda i: (i, 0)),
                   pl.BlockSpec((1, E, tm), lambda i: (i // per_b, 0, i % per_b))],
        out_shape=[jax.ShapeDtypeStruct((M, D), BF16), jax.ShapeDtypeStruct((B, E, T), F32)],
        compiler_params=_params(("parallel",)),
        name="moe_router",
    )(x, sc, sh, wr_t)


def _lane_cumsum(m01, blk):
    E, T = m01.shape
    i = lax.broadcasted_iota(I32, (blk, blk), 0)
    j = lax.broadcasted_iota(I32, (blk, blk), 1)
    upper = jnp.where(i <= j, 1.0, 0.0).astype(BF16)
    carry = jnp.zeros((E, 1), F32)
    parts = []
    for c in range(T // blk):
        seg = m01[:, c * blk:(c + 1) * blk]
        cs = jnp.dot(seg.astype(BF16), upper, preferred_element_type=F32) + carry
        parts.append(cs)
        carry = cs[:, blk - 1:blk]
    return jnp.concatenate(parts, axis=1) if len(parts) > 1 else parts[0]


def _topk_kernel(aff_ref, pos_ref, los_ref, *, cap, ts):
    a = aff_ref[0]
    E, T = a.shape
    bits = pltpu.bitcast(a, I32)

    def body(i, t):
        cand = t | lax.shift_left(jnp.int32(1), 30 - i)
        cnt = jnp.sum((bits >= cand).astype(I32), axis=1, keepdims=True)
        return jnp.where(cnt >= cap, cand, t)

    thr = lax.fori_loop(0, 31, body, jnp.zeros((E, 1), I32))
    gt = bits > thr
    eq = bits == thr
    need = cap - jnp.sum(gt.astype(I32), axis=1, keepdims=True)
    blk = min(512, T)
    eq01 = jnp.where(eq, 1.0, 0.0)
    eq_rank = _lane_cumsum(eq01, blk) - eq01
    sel = jnp.logical_or(gt, jnp.logical_and(eq, eq_rank < need.astype(F32)))
    sel01 = jnp.where(sel, 1.0, 0.0)
    slot = _lane_cumsum(sel01, blk) - sel01
    pos_ref[0] = jnp.where(sel, slot.astype(I32), -1)
    ti = lax.broadcasted_iota(I32, (T, LANES), 0)
    wi = lax.broadcasted_iota(I32, (T, LANES), 1)
    before = jnp.where(ti < wi * ts, 1.0, 0.0).astype(BF16)
    los_ref[0] = jnp.dot(sel01.astype(BF16), before, preferred_element_type=F32).astype(I32)


def moe_topk(aff_t, cap, ts):
    B, E, T = aff_t.shape
    return pl.pallas_call(
        functools.partial(_topk_kernel, cap=cap, ts=ts),
        grid=(B,),
        in_specs=[pl.BlockSpec((1, E, T), lambda b: (b, 0, 0))],
        out_specs=[pl.BlockSpec((1, E, T), lambda b: (b, 0, 0)),
                   pl.BlockSpec((1, E, LANES), lambda b: (b, 0, 0))],
        out_shape=[jax.ShapeDtypeStruct((B, E, T), I32), jax.ShapeDtypeStruct((B, E, LANES), I32)],
        compiler_params=_params(("parallel",)),
        name="moe_topk",
    )(aff_t)


def _slot_windows(cap, win):
    win = min(win, cap)
    return win, cap // win


def _gather_kernel(los_ref, pos_ref, h_ref, o_ref, acc, *, cap, nt, ts):
    b, e = pl.program_id(0), pl.program_id(1)
    base = (b * pl.num_programs(1) + e) * (nt + 1)
    win, nwin = _slot_windows(cap, LANES)
    acc[...] = jnp.zeros(acc.shape, F32)
    for c in range(nt):
        lo = los_ref[base + c]
        hi = los_ref[base + c + 1]
        for w in range(nwin):
            @pl.when(jnp.logical_and(lo < (w + 1) * win, hi > w * win))
            def _():
                slot = lax.broadcasted_iota(I32, (win, ts), 0) + w * win
                onehot = jnp.where(slot == pos_ref[0, 0, :, c * ts:(c + 1) * ts], 1.0, 0.0).astype(BF16)
                acc[w * win:(w + 1) * win, :] += jnp.dot(onehot, h_ref[c * ts:(c + 1) * ts, :],
                                                         preferred_element_type=F32)
    o_ref[0, 0] = acc[...].astype(o_ref.dtype)


def moe_gather(los, pos4, h, B, T, cap, ts):
    E = pos4.shape[1]
    D = h.shape[1]
    nt = T // ts
    return pl.pallas_call(
        functools.partial(_gather_kernel, cap=cap, nt=nt, ts=ts),
        grid_spec=pltpu.PrefetchScalarGridSpec(
            num_scalar_prefetch=1,
            grid=(B, E),
            in_specs=[pl.BlockSpec((1, 1, 1, T), lambda b, e, s: (b, e, 0, 0)),
                      pl.BlockSpec((T, D), lambda b, e, s: (b, 0))],
            out_specs=pl.BlockSpec((1, 1, cap, D), lambda b, e, s: (b, e, 0, 0)),
            scratch_shapes=[pltpu.VMEM((cap, D), F32)]),
        out_shape=jax.ShapeDtypeStruct((B, E, cap, D), BF16),
        compiler_params=_params(("parallel", "arbitrary")),
        name="moe_gather",
    )(los, pos4, h)


def _ffn_kernel(x_ref, wg_ref, wu_ref, wd_ref, o_ref):
    x = x_ref[0, 0]
    g = jnp.dot(x, wg_ref[0], preferred_element_type=F32)
    u = jnp.dot(x, wu_ref[0], preferred_element_type=F32)
    hid = (_silu(g) * u).astype(BF16)
    o_ref[0, 0] = jnp.dot(hid, wd_ref[0], preferred_element_type=F32).astype(o_ref.dtype)


def moe_ffn(xs, wg, wu, wd, layer):
    B, E, cap, D = xs.shape
    Fh = wg.shape[-1]
    e0 = layer * E
    return pl.pallas_call(
        _ffn_kernel,
        grid=(E, B),
        in_specs=[pl.BlockSpec((1, 1, cap, D), lambda e, b: (b, e, 0, 0)),
                  pl.BlockSpec((1, D, Fh), lambda e, b: (e0 + e, 0, 0)),
                  pl.BlockSpec((1, D, Fh), lambda e, b: (e0 + e, 0, 0)),
                  pl.BlockSpec((1, Fh, D), lambda e, b: (e0 + e, 0, 0))],
        out_specs=pl.BlockSpec((1, 1, cap, D), lambda e, b: (b, e, 0, 0)),
        out_shape=jax.ShapeDtypeStruct((B, E, cap, D), BF16),
        compiler_params=_params(("parallel", "arbitrary")),
        name="moe_ffn",
    )(xs, wg, wu, wd)


def _combine_kernel(los_ref, pos_ref, gate_ref, y_ref, x_ref, g_ref, lg_ref, lb_ref, o_ref, acc, *, cap, nt):
    b, i, e = pl.program_id(0), pl.program_id(1), pl.program_id(2)
    base = (b * pl.num_programs(2) + e) * (nt + 1) + i
    lo = los_ref[base]
    hi = los_ref[base + 1]
    pos = pos_ref[0].astype(F32)
    gate = gate_ref[0]
    tt, E = pos.shape
    lane = lax.broadcasted_iota(I32, (tt, E), 1)
    pcol = jnp.sum(jnp.where(lane == e, pos, 0.0), axis=1, keepdims=True)
    gcol = jnp.sum(jnp.where(lane == e, gate, 0.0), axis=1, keepdims=True)
    win, nwin = _slot_windows(cap, MXU_DEPTH)

    @pl.when(e == 0)
    def _():
        acc[...] = jnp.zeros(acc.shape, F32)

    for w in range(nwin):
        @pl.when(jnp.logical_and(lo < (w + 1) * win, hi > w * win))
        def _():
            slot = (lax.broadcasted_iota(I32, (tt, win), 1) + w * win).astype(F32)
            onehot = jnp.where(slot == pcol, 1.0, 0.0).astype(BF16)
            acc[...] += gcol * jnp.dot(onehot, y_ref[0, 0, w * win:(w + 1) * win, :], preferred_element_type=F32)

    @pl.when(e == pl.num_programs(2) - 1)
    def _():
        v = DEEPNORM_ALPHA * x_ref[...] + g_ref[0] * acc[...]
        o_ref[...] = _layer_norm_rows(v, lg_ref[...], lb_ref[...])


def moe_combine(los, pos_tm, aff_tm, yd, x, g, lg, lb, B, T, cap, ts):
    E = pos_tm.shape[-1]
    M, D = x.shape
    nt = T // ts
    return pl.pallas_call(
        functools.partial(_combine_kernel, cap=cap, nt=nt),
        grid_spec=pltpu.PrefetchScalarGridSpec(
            num_scalar_prefetch=1,
            grid=(B, nt, E),
            in_specs=[pl.BlockSpec((1, ts, E), lambda b, i, e, s: (b, i, 0)),
                      pl.BlockSpec((1, ts, E), lambda b, i, e, s: (b, i, 0)),
                      pl.BlockSpec((1, 1, cap, D), lambda b, i, e, s: (b, e, 0, 0)),
                      pl.BlockSpec((ts, D), lambda b, i, e, s: (b * nt + i, 0)),
                      pl.BlockSpec((1, 1, D), lambda b, i, e, s: (b, 0, 0)),
                      pl.BlockSpec((1, D), lambda b, i, e, s: (0, 0)),
                      pl.BlockSpec((1, D), lambda b, i, e, s: (0, 0))],
            out_specs=pl.BlockSpec((ts, D), lambda b, i, e, s: (b * nt + i, 0)),
            scratch_shapes=[pltpu.VMEM((ts, D), F32)]),
        out_shape=jax.ShapeDtypeStruct((M, D), F32),
        compiler_params=_params(("parallel", "parallel", "arbitrary")),
        name="moe_combine",
    )(los, pos_tm, aff_tm, yd, x, g, lg, lb)


def moe_layer(x, sc, sh, g, lg, lb, w_router, w_gate, w_up, w_down, layer, B, T):
    cap = CAPACITY_FACTOR * T // N_EXPERTS
    ts = min(512, T)
    nt = T // ts
    h, aff_t = moe_router_op(x, sc, sh, w_router.T, B, T)
    pos, los = moe_topk(aff_t, cap, ts)
    los = los[:, :, :nt + 1].reshape(-1)
    xs = moe_gather(los, pos[:, :, None, :], h, B, T, cap, ts)
    yd = moe_ffn(xs, w_gate, w_up, w_down, layer)
    pos_tm = jnp.swapaxes(pos, 1, 2)
    aff_tm = jnp.swapaxes(aff_t, 1, 2)
    return moe_combine(los, pos_tm, aff_tm, yd, x, g, lg, lb, B, T, cap, ts)


def _pad_cols(w, n):
    return jnp.pad(w, ((0, 0), (0, n - w.shape[1])))


def _rot_cols(w):
    half = w.shape[-1] // 2
    return jnp.concatenate([-w[..., half:], w[..., :half]], axis=-1)


def _even_w_in(w):
    D = w.shape[0]
    o = MLA_IN
    rkv = w[:, o:o + 3 * RWKV_WIDTH]
    cq = w[:, :Q_LORA]
    ckv = w[:, Q_LORA:Q_LORA + KV_LORA]
    kr = w[:, Q_LORA + KV_LORA:MLA_IN]
    l0 = o + 3 * RWKV_WIDTH
    lora = [_pad_cols(w[:, l0 + i * DECAY_LORA:l0 + (i + 1) * DECAY_LORA], LORA_PAD) for i in range(4)]
    gd = w[:, l0 + 4 * DECAY_LORA:]
    pad = jnp.zeros((D, LANES), w.dtype)
    out = jnp.concatenate([rkv, cq, ckv, kr, _rot_cols(kr), pad] + lora + [gd], axis=1)
    assert out.shape[1] == EVEN_COLS
    return out.astype(BF16)


def _even_mu(mu):
    rkv = mu[:3 * RWKV_WIDTH]
    l0 = 3 * RWKV_WIDTH
    lora = [jnp.pad(mu[l0 + i * DECAY_LORA:l0 + (i + 1) * DECAY_LORA], (0, LORA_PAD - DECAY_LORA)) for i in range(4)]
    gd = mu[l0 + 4 * DECAY_LORA:]
    return rkv.reshape(1, -1), jnp.concatenate(lora + [gd]).reshape(1, -1)


def _pad_rows(w, n):
    return jnp.pad(w, ((0, 0), (0, n - w.shape[1]), (0, 0)))


def even_mixer(x, sc, sh, positions_tables, w_in, shift_mu, q_norm, w_uq, kv_norm, w_ukv, w0, w2, a0, a2, g2,
               k_k, k_a, r_k, gn_g, gn_b, B, T):
    cs, sn = positions_tables
    H = MLA_HEADS
    p = mm_mod(x, sc, sh, _even_w_in(w_in), T, 1024, 1024, F32)
    wq = w_uq.reshape(Q_LORA, H, NOPE_DIM + ROPE_DIM)
    wq = jnp.concatenate([wq, _rot_cols(wq[..., NOPE_DIM:])], axis=-1)
    wq = jnp.swapaxes(wq, 0, 1).astype(BF16)
    wkv = w_ukv.reshape(KV_LORA, H, NOPE_DIM + V_DIM)
    wk = jnp.swapaxes(wkv[..., :NOPE_DIM], 0, 1).astype(BF16)
    wvt = jnp.transpose(wkv[..., NOPE_DIM:], (1, 2, 0)).astype(BF16)
    q, k, vt = mla_proj(p, cs, sn, q_norm.reshape(1, -1), kv_norm.reshape(1, -1), wq, wk, wvt, B, T)
    attn = mla_attention(q, k, vt)
    mu_rkv, mu_lora = _even_mu(shift_mu)
    z_rkv = shift_mix(p, mu_rkv, T, 0, 3 * RWKV_WIDTH, 1024, False)
    la = shift_mix(p, mu_lora, T, 8448, 768, 768, True)
    y_rwkv = rwkv_scan(z_rkv.reshape(B, T, -1), la.reshape(B, T, -1),
                       _pad_rows(w2, LORA_PAD).astype(BF16), _pad_rows(a2, LORA_PAD).astype(BF16),
                       g2.astype(BF16), w0, a0, k_k.reshape(1, -1), k_a.reshape(1, -1), r_k.reshape(1, -1),
                       gn_g.reshape(1, -1), gn_b.reshape(1, -1), B, T)
    return attn.reshape(B * T, -1), y_rwkv.reshape(B * T, -1)


def odd_mixer(x, sc, sh, w_in, conv_w, a_log, dt_bias, norm_g, B, T):
    M = B * T
    nC = T // CHUNK
    wb = w_in.astype(BF16)
    n_main = GDN_QKV + GDN_VAL_WIDTH
    p = mm_mod(x, sc, sh, wb[:, :n_main], T, 1024, 1024, F32)
    ba = mm_mod(x, sc, sh, wb[:, n_main:], T, 512, 128, F32)
    qkv = gdn_conv_op(p, conv_w, T)
    nh = 2 * GDN_V_HEADS
    bg = gdn_gates(ba, jnp.pad(a_log.reshape(1, -1), ((0, 0), (nh, 0))),
                   jnp.pad(dt_bias.reshape(1, -1), ((0, 0), (nh, 0))), T)
    bg = bg.reshape(B, nC, CHUNK, 2, 2, GDN_K_HEADS, 2)
    bg = jnp.transpose(bg, (0, 1, 3, 4, 5, 6, 2)).reshape(B, nC, 4 * GDN_K_HEADS, PAIR)
    o = gdn_scan(qkv.reshape(B, T, -1), p.reshape(B, T, -1), GDN_QKV // (2 * GDN_D), bg, norm_g.reshape(1, -1), B, T)
    return o.reshape(M, -1)


def kernel(x, c, positions, ada_w, ada_b, ln_g, ln_b, e_w_in, e_shift_mu, mla_q_norm, mla_w_uq, mla_kv_norm,
           mla_w_ukv, rwkv_w0, rwkv_w2, rwkv_a0, rwkv_a2, rwkv_g2, rwkv_k_k, rwkv_k_a, rwkv_r_k, rwkv_gn_g,
           rwkv_gn_b, e_w_out, o_w_in, gdn_conv, gdn_a_log, gdn_dt_bias, gdn_norm, o_w_out, moe_router,
           moe_w_gate, moe_w_up, moe_w_down):
    B, T, D = x.shape
    M = B * T
    depth = ada_w.shape[0]
    mod = ada_modulation(c, ada_w, ada_b)
    tables = rope_tables(positions)
    xf = x.reshape(M, D)
    n_exp = moe_w_gate.shape[1]
    wg_all = moe_w_gate.astype(BF16).reshape(depth * n_exp, *moe_w_gate.shape[2:])
    wu_all = moe_w_up.astype(BF16).reshape(depth * n_exp, *moe_w_up.shape[2:])
    wd_all = moe_w_down.astype(BF16).reshape(depth * n_exp, *moe_w_down.shape[2:])
    for i in range(depth):
        m6 = mod[i].reshape(B, 6, 1, D)
        sh_m, sc_m, g_m, sh_f, sc_f, g_f = (m6[:, n] for n in range(6))
        j = i // 2
        lg0, lb0 = ln_g[i, 0].reshape(1, D), ln_b[i, 0].reshape(1, D)
        lg1, lb1 = ln_g[i, 1].reshape(1, D), ln_b[i, 1].reshape(1, D)
        if i % 2 == 0:
            a0, a1 = even_mixer(xf, sc_m, sh_m, tables, e_w_in[j], e_shift_mu[j], mla_q_norm[j], mla_w_uq[j],
                                mla_kv_norm[j], mla_w_ukv[j], rwkv_w0[j], rwkv_w2[j], rwkv_a0[j], rwkv_a2[j],
                                rwkv_g2[j], rwkv_k_k[j], rwkv_k_a[j], rwkv_r_k[j], rwkv_gn_g[j], rwkv_gn_b[j],
                                B, T)
            xf = mm_ln(a0, 0, a1, 0, e_w_out[j].astype(BF16), xf, g_m, lg0, lb0, T, 512)
        else:
            o = odd_mixer(xf, sc_m, sh_m, o_w_in[j], gdn_conv[j], gdn_a_log[j], gdn_dt_bias[j], gdn_norm[j], B, T)
            xf = mm_ln(o, 0, o, 1, o_w_out[j].astype(BF16), xf, g_m, lg0, lb0, T, 512)
        xf = moe_layer(xf, sc_f, sh_f, g_f, lg1, lb1, moe_router[i], wg_all, wu_all, wd_all, i, B, T)
    return xf.reshape(B, T, D)
```

```python
import functools
import math

import jax
import jax.numpy as jnp
from jax import lax
from jax.experimental import pallas as pl
from jax.experimental.pallas import tpu as pltpu

F32 = jnp.float32
BF16 = jnp.bfloat16
I32 = jnp.int32

DEPTH = 4
DEEPNORM_ALPHA = (2 * DEPTH) ** 0.25
MLA_HEADS = 16
Q_LORA = 1536
KV_LORA = 512
NOPE_DIM = 128
ROPE_DIM = 64
V_DIM = 128
ROPE_THETA = 10000.0
MLA_IN = Q_LORA + KV_LORA + ROPE_DIM
RWKV_HEADS = 32
RWKV_N = 64
RWKV_WIDTH = RWKV_HEADS * RWKV_N
DECAY_LORA = 96
AAA_LORA = 96
GATE_LORA = 256
RWKV_DECAY_SCALE = 0.6065306597126334
RWKV_GN_EPS = 64e-5
GDN_K_HEADS = 16
GDN_V_HEADS = 32
GDN_D = 128
GDN_KEY_WIDTH = GDN_K_HEADS * GDN_D
GDN_VAL_WIDTH = GDN_V_HEADS * GDN_D
GDN_QKV = 2 * GDN_KEY_WIDTH + GDN_VAL_WIDTH
CONV_WIDTH = 5
N_EXPERTS = 16
CAPACITY_FACTOR = 2

LANES = 128
MXU_DEPTH = 256
CHUNK = 64
LOG2_CHUNK = 6
PAIR = 2 * CHUNK
GROUP = 8
INV_BLOCK = 16
LOG2_INV_BLOCK = 4
LORA_PAD = 128
EVEN_COLS = 9216
VMEM_LIMIT = 56 * 1024 * 1024


def _params(sem, vmem=VMEM_LIMIT):
    return pltpu.CompilerParams(dimension_semantics=sem, vmem_limit_bytes=vmem)


def _dot(a, b):
    return jnp.dot(a.astype(BF16), b.astype(BF16), preferred_element_type=F32)


def _dot_nt(a, b):
    return lax.dot_general(a.astype(BF16), b.astype(BF16), (((1,), (1,)), ((), ())),
                           preferred_element_type=F32)


def _dot_tn(a, b):
    return lax.dot_general(a.astype(BF16), b.astype(BF16), (((0,), (0,)), ((), ())),
                           preferred_element_type=F32)


def _bmm(a, b):
    return jnp.einsum('gij,gjk->gik', a.astype(BF16), b.astype(BF16), preferred_element_type=F32)


def _bmm_nt(a, b):
    return jnp.einsum('gik,gjk->gij', a.astype(BF16), b.astype(BF16), preferred_element_type=F32)


def _split2(x):
    hi = x.astype(BF16)
    lo = (x - hi.astype(F32)).astype(BF16)
    return hi, lo


def _split3(x):
    hi = x.astype(BF16)
    r1 = x - hi.astype(F32)
    mid = r1.astype(BF16)
    lo = (r1 - mid.astype(F32)).astype(BF16)
    return hi, mid, lo


def _sigmoid(x):
    return 1.0 / (1.0 + jnp.exp(-x))


def _silu(x):
    return x * _sigmoid(x)


def _ada_kernel(c_ref, w_ref, b_ref, o_ref):
    c = c_ref[...]
    o_ref[0] = _dot(_silu(c), w_ref[0]) + b_ref[0]


def ada_modulation(c, ada_w, ada_b):
    L, D, N = ada_w.shape
    B = c.shape[0]
    rows = 8
    cp = jnp.zeros((rows, D), F32).at[:B].set(c)
    tn = 1024
    out = pl.pallas_call(
        _ada_kernel,
        grid=(L, N // tn),
        in_specs=[pl.BlockSpec((rows, D), lambda l, j: (0, 0)),
                  pl.BlockSpec((1, D, tn), lambda l, j: (l, 0, j)),
                  pl.BlockSpec((1, 1, tn), lambda l, j: (l, 0, j))],
        out_specs=pl.BlockSpec((1, rows, tn), lambda l, j: (l, 0, j)),
        out_shape=jax.ShapeDtypeStruct((L, rows, N), F32),
        compiler_params=_params(("parallel", "parallel")),
        name="ada_mod",
    )(cp, ada_w, ada_b.reshape(L, 1, N))
    return out[:, :B]


def _mm_mod_kernel(x_ref, sc_ref, sh_ref, w_ref, o_ref, h_scr):
    @pl.when(pl.program_id(1) == 0)
    def _():
        h_scr[...] = (x_ref[...] * (1.0 + sc_ref[0]) + sh_ref[0]).astype(BF16)

    o_ref[...] = jnp.dot(h_scr[...], w_ref[...], preferred_element_type=F32).astype(o_ref.dtype)


def mm_mod(x, sc, sh, w, T, tm, tn, out_dtype):
    M, D = x.shape
    N = w.shape[1]
    tm = min(tm, T)
    tn = min(tn, N)
    per_b = T // tm
    return pl.pallas_call(
        _mm_mod_kernel,
        grid=(M // tm, N // tn),
        in_specs=[pl.BlockSpec((tm, D), lambda i, j: (i, 0)),
                  pl.BlockSpec((1, 1, D), lambda i, j: (i // per_b, 0, 0)),
                  pl.BlockSpec((1, 1, D), lambda i, j: (i // per_b, 0, 0)),
                  pl.BlockSpec((D, tn), lambda i, j: (0, j))],
        out_specs=pl.BlockSpec((tm, tn), lambda i, j: (i, j)),
        out_shape=jax.ShapeDtypeStruct((M, N), out_dtype),
        scratch_shapes=[pltpu.VMEM((tm, D), BF16)],
        compiler_params=_params(("parallel", "arbitrary")),
        name="mm_mod",
    )(x, sc, sh, w)


def _layer_norm_rows(v, lg, lb):
    mu = jnp.mean(v, axis=-1, keepdims=True)
    d = v - mu
    var = jnp.mean(d * d, axis=-1, keepdims=True)
    return d * lax.rsqrt(var + 1e-5) * lg + lb


def _mm_ln_kernel(a0_ref, a1_ref, w_ref, x_ref, g_ref, lg_ref, lb_ref, o_ref, acc):
    k = pl.program_id(1)

    @pl.when(k == 0)
    def _():
        acc[...] = jnp.dot(a0_ref[...], w_ref[...], preferred_element_type=F32)

    @pl.when(k == 1)
    def _():
        y = acc[...] + jnp.dot(a1_ref[...], w_ref[...], preferred_element_type=F32)
        v = DEEPNORM_ALPHA * x_ref[...] + g_ref[0] * y
        o_ref[...] = _layer_norm_rows(v, lg_ref[...], lb_ref[...])


def mm_ln(a0, a0_blk, a1, a1_blk, w, x, g, lg, lb, T, tm):
    M, D = x.shape
    tm = min(tm, T)
    per_b = T // tm
    return pl.pallas_call(
        _mm_ln_kernel,
        grid=(M // tm, 2),
        in_specs=[pl.BlockSpec((tm, D), lambda i, k: (i, a0_blk)),
                  pl.BlockSpec((tm, D), lambda i, k: (i, a1_blk)),
                  pl.BlockSpec((D, D), lambda i, k: (k, 0)),
                  pl.BlockSpec((tm, D), lambda i, k: (i, 0)),
                  pl.BlockSpec((1, 1, D), lambda i, k: (i // per_b, 0, 0)),
                  pl.BlockSpec((1, D), lambda i, k: (0, 0)),
                  pl.BlockSpec((1, D), lambda i, k: (0, 0))],
        out_specs=pl.BlockSpec((tm, D), lambda i, k: (i, 0)),
        out_shape=jax.ShapeDtypeStruct((M, D), F32),
        scratch_shapes=[pltpu.VMEM((tm, D), F32)],
        compiler_params=_params(("parallel", "arbitrary")),
        name="mm_ln",
    )(a0, a1, w, x, g, lg, lb)


def _rope_kernel(pos_ref, inv_ref, cs_ref, sn_ref):
    ang = pos_ref[...].astype(F32) * inv_ref[...]
    cs_ref[...] = jnp.cos(ang)
    sn_ref[...] = jnp.sin(ang)


def rope_tables(positions):
    B, T = positions.shape
    M = B * T
    inv = ROPE_THETA ** (-jnp.arange(0, ROPE_DIM, 2, dtype=F32) / ROPE_DIM)
    inv2 = jnp.concatenate([inv, inv]).reshape(1, ROPE_DIM)
    tm = min(1024, T)
    return pl.pallas_call(
        _rope_kernel,
        grid=(M // tm,),
        in_specs=[pl.BlockSpec((tm, 1), lambda i: (i, 0)),
                  pl.BlockSpec((1, ROPE_DIM), lambda i: (0, 0))],
        out_specs=[pl.BlockSpec((tm, ROPE_DIM), lambda i: (i, 0)),
                   pl.BlockSpec((tm, ROPE_DIM), lambda i: (i, 0))],
        out_shape=[jax.ShapeDtypeStruct((M, ROPE_DIM), F32)] * 2,
        compiler_params=_params(("parallel",)),
        name="rope_tables",
    )(positions.reshape(M, 1), inv2)


def _rms_rows(x, g, eps=1e-6):
    return x * lax.rsqrt(jnp.mean(x * x, axis=-1, keepdims=True) + eps) * g


def _mla_proj_kernel(cq_ref, ckv_ref, kr_ref, cs_ref, sn_ref, qn_ref, kvn_ref, wq_ref, wk_ref, wvt_ref,
                     q_ref, k_ref, vt_ref, cq_scr, ckv_scr, kr_scr):
    @pl.when(pl.program_id(1) == 0)
    def _():
        cq_scr[...] = _rms_rows(cq_ref[...], qn_ref[...]).astype(BF16)
        ckv_scr[...] = _rms_rows(ckv_ref[...], kvn_ref[...]).astype(BF16)
        kr = kr_ref[...]
        kr_scr[...] = kr[:, :ROPE_DIM] * cs_ref[...] + kr[:, ROPE_DIM:] * sn_ref[...]

    scale = (NOPE_DIM + ROPE_DIM) ** -0.5 * math.log2(math.e)
    qf = jnp.dot(cq_scr[...], wq_ref[0], preferred_element_type=F32)
    q_rope = qf[:, NOPE_DIM:NOPE_DIM + ROPE_DIM] * cs_ref[...] + qf[:, NOPE_DIM + ROPE_DIM:] * sn_ref[...]
    q_ref[0, 0, :, :NOPE_DIM] = (qf[:, :NOPE_DIM] * scale).astype(BF16)
    q_ref[0, 0, :, NOPE_DIM:] = (q_rope * scale).astype(BF16)
    ckv = ckv_scr[...]
    k_ref[0, 0, :, :NOPE_DIM] = jnp.dot(ckv, wk_ref[0], preferred_element_type=F32).astype(BF16)
    k_ref[0, 0, :, NOPE_DIM:] = kr_scr[...].astype(BF16)
    vt_ref[0, 0] = lax.dot_general(wvt_ref[0], ckv, (((1,), (1,)), ((), ())),
                                   preferred_element_type=F32).astype(BF16)


def mla_proj(p, cs, sn, qn, kvn, wq, wk, wvt, B, T):
    M = B * T
    H = MLA_HEADS
    tm = min(1024, T)
    per_b = T // tm
    qk_dim = NOPE_DIM + ROPE_DIM
    out_map = lambda i, h: (i // per_b, h, i % per_b, 0)
    return pl.pallas_call(
        _mla_proj_kernel,
        grid=(M // tm, H),
        in_specs=[pl.BlockSpec((tm, Q_LORA), lambda i, h: (i, 6144 // Q_LORA)),
                  pl.BlockSpec((tm, KV_LORA), lambda i, h: (i, 7680 // KV_LORA)),
                  pl.BlockSpec((tm, 2 * ROPE_DIM), lambda i, h: (i, 8192 // (2 * ROPE_DIM))),
                  pl.BlockSpec((tm, ROPE_DIM), lambda i, h: (i, 0)),
                  pl.BlockSpec((tm, ROPE_DIM), lambda i, h: (i, 0)),
                  pl.BlockSpec((1, Q_LORA), lambda i, h: (0, 0)),
                  pl.BlockSpec((1, KV_LORA), lambda i, h: (0, 0)),
                  pl.BlockSpec((1, Q_LORA, 256), lambda i, h: (h, 0, 0)),
                  pl.BlockSpec((1, KV_LORA, NOPE_DIM), lambda i, h: (h, 0, 0)),
                  pl.BlockSpec((1, V_DIM, KV_LORA), lambda i, h: (h, 0, 0))],
        out_specs=[pl.BlockSpec((1, 1, tm, qk_dim), out_map),
                   pl.BlockSpec((1, 1, tm, qk_dim), out_map),
                   pl.BlockSpec((1, 1, V_DIM, tm), lambda i, h: (i // per_b, h, 0, i % per_b))],
        out_shape=[jax.ShapeDtypeStruct((B, H, T, qk_dim), BF16),
                   jax.ShapeDtypeStruct((B, H, T, qk_dim), BF16),
                   jax.ShapeDtypeStruct((B, H, V_DIM, T), BF16)],
        scratch_shapes=[pltpu.VMEM((tm, Q_LORA), BF16), pltpu.VMEM((tm, KV_LORA), BF16),
                        pltpu.VMEM((tm, ROPE_DIM), F32)],
        compiler_params=_params(("parallel", "arbitrary")),
        name="mla_proj",
    )(p, p, p, cs, sn, qn, kvn, wq, wk, wvt)


def _attn_kernel(q_ref, k_ref, vt_ref, o_ref, *, kc):
    q = q_ref[0, 0]
    T = k_ref.shape[2]
    nk = T // kc
    tq = q.shape[0]
    hq = min(MXU_DEPTH, tq)
    nh = tq // hq

    def scores(j):
        qj = q[j * hq:(j + 1) * hq]
        ss = [lax.dot_general(k_ref[0, 0, c * kc:(c + 1) * kc, :], qj, (((1,), (1,)), ((), ())),
                              preferred_element_type=F32) for c in range(nk)]
        m = jnp.max(ss[0], axis=0, keepdims=True)
        for c in range(1, nk):
            m = jnp.maximum(m, jnp.max(ss[c], axis=0, keepdims=True))
        return ss, m

    def finish(j, ss, m):
        acc = l = None
        for c in range(nk):
            p = jnp.exp2(ss[c] - m)
            lc = jnp.sum(p, axis=0, keepdims=True)
            ac = jnp.dot(vt_ref[0, 0, :, c * kc:(c + 1) * kc], p.astype(BF16), preferred_element_type=F32)
            l = lc if l is None else l + lc
            acc = ac if acc is None else acc + ac
        o_ref[0, j * hq:(j + 1) * hq, :] = (acc / l).T.astype(o_ref.dtype)

    prev = None
    for j in range(nh):
        cur = scores(j)
        if prev is not None:
            finish(j - 1, *prev)
        prev = cur
    finish(nh - 1, *prev)


def mla_attention(q, k, vt):
    B, H, T, dqk = q.shape
    dv = vt.shape[2]
    tq = min(1024, T)
    kc = min(1024, T)
    return pl.pallas_call(
        functools.partial(_attn_kernel, kc=kc),
        grid=(B, H, T // tq),
        in_specs=[pl.BlockSpec((1, 1, tq, dqk), lambda b, h, i: (b, h, i, 0)),
                  pl.BlockSpec((1, 1, T, dqk), lambda b, h, i: (b, h, 0, 0)),
                  pl.BlockSpec((1, 1, dv, T), lambda b, h, i: (b, h, 0, 0))],
        out_specs=pl.BlockSpec((1, tq, dv), lambda b, h, i: (b, i, h)),
        out_shape=jax.ShapeDtypeStruct((B, T, H * dv), BF16),
        compiler_params=_params(("parallel", "parallel", "parallel")),
        name="mla_attn",
    )(q, k, vt)


def _shift_rows(cur, prev8, next8, d):
    tm = cur.shape[0]
    if d == 0:
        return cur
    main = pltpu.roll(cur, (-d) % tm, axis=0)
    first = pltpu.roll(jnp.concatenate([prev8, cur[:16]], axis=0), (-d) % 24, axis=0)[8:16]
    last = pltpu.roll(jnp.concatenate([cur[tm - 16:], next8], axis=0), (-d) % 24, axis=0)[8:16]
    return jnp.concatenate([first, main[8:tm - 8], last], axis=0)


def _halo_specs(tm, cw, per_b, nrow8, col_of):
    r8 = tm // 8
    cur = pl.BlockSpec((tm, cw), lambda i, j: (i, col_of(j)))
    prv = pl.BlockSpec((8, cw), lambda i, j: (jnp.maximum(i * r8 - 1, 0), col_of(j)))
    nxt = pl.BlockSpec((8, cw), lambda i, j: (jnp.minimum((i + 1) * r8, nrow8 - 1), col_of(j)))
    return cur, prv, nxt


def _halo_load(cur_ref, prev_ref, next_ref, per_b):
    ti = pl.program_id(0) % per_b
    cur = cur_ref[...].astype(F32)
    prev8 = prev_ref[...].astype(F32) * jnp.where(ti == 0, 0.0, 1.0)
    next8 = next_ref[...].astype(F32) * jnp.where(ti == per_b - 1, 0.0, 1.0)
    return cur, prev8, next8


def _pair_masks(rev):
    i = lax.broadcasted_iota(I32, (PAIR, PAIR), 0)
    j = lax.broadcasted_iota(I32, (PAIR, PAIR), 1)
    same = lax.shift_right_logical(i, LOG2_CHUNK) == lax.shift_right_logical(j, LOG2_CHUNK)
    if rev:
        strict = jnp.logical_and(same, i < j)
        incl = jnp.logical_and(same, i <= j)
    else:
        strict = jnp.logical_and(same, i > j)
        incl = jnp.logical_and(same, i >= j)
    eye_b = i == j
    same_blk = lax.shift_right_logical(i, LOG2_INV_BLOCK) == lax.shift_right_logical(j, LOG2_INV_BLOCK)
    return strict, incl, eye_b, same_blk


def _unit_tri_inverse(nmat, eye, same_blk):
    n = nmat.shape[-1]
    nd = jnp.where(same_blk, nmat, 0.0)
    noff = nmat - nd
    x = eye + nd
    p = _bmm(nd, nd)
    yield
    for s in range(1, LOG2_INV_BLOCK):
        if s < LOG2_INV_BLOCK - 1:
            xp = _bmm(jnp.concatenate([x, p], axis=1), p)
            x = x + xp[:, :n]
            p = xp[:, n:]
        else:
            x = x + _bmm(x, p)
        yield
    y = _bmm(x, noff)
    yield
    acc = x
    for _ in range(CHUNK // INV_BLOCK - 1):
        acc = x + _bmm(y, acc)
        yield
    return acc


def _aligned(start, m):
    return start if isinstance(start, int) else pl.multiple_of(start, m)


def _interleave(n_steps, step_fn, gens, stages_per_step):
    for i in range(n_steps):
        step_fn(i)
        for _ in range(stages_per_step):
            for g in gens:
                next(g, None)
    for g in gens:
        for _ in g:
            pass


def _rows_to_cols(rows, eye_b):
    return jnp.sum(jnp.where(eye_b, rows, 0.0), axis=2, keepdims=True)


def _stack_heads(x, head0):
    return jnp.concatenate([jnp.where(head0, x, 0.0), jnp.where(head0, 0.0, x)], axis=1)


def _shift_mix_kernel(cur_ref, prev_ref, next_ref, mu_ref, o_ref, *, per_b, lora):
    cur, prev8, next8 = _halo_load(cur_ref, prev_ref, next_ref, per_b)
    prv = _shift_rows(cur, prev8, next8, -1)
    nxt = _shift_rows(cur, prev8, next8, 1)
    z = cur + (0.5 * (prv + nxt) - cur) * mu_ref[...]
    if lora:
        col = lax.broadcasted_iota(I32, z.shape, 1)
        z = jnp.where(col < 2 * LORA_PAD, jnp.tanh(z), jnp.where(col >= 4 * LORA_PAD, _sigmoid(z), z))
    o_ref[...] = z.astype(o_ref.dtype)


def shift_mix(p, mu, T, col0, width, cw, lora):
    M = p.shape[0]
    tm = min(512, T)
    per_b = T // tm
    c0 = col0 // cw
    cur, prv, nxt = _halo_specs(tm, cw, per_b, M // 8, lambda j: c0 + j)
    return pl.pallas_call(
        functools.partial(_shift_mix_kernel, per_b=per_b, lora=lora),
        grid=(M // tm, width // cw),
        in_specs=[cur, prv, nxt, pl.BlockSpec((1, cw), lambda i, j: (0, j))],
        out_specs=pl.BlockSpec((tm, cw), lambda i, j: (i, j)),
        out_shape=jax.ShapeDtypeStruct((M, width), BF16),
        compiler_params=_params(("parallel", "parallel")),
        name="rwkv_shift_mix",
    )(p, p, p, mu)


def _rwkv_prepare(g, G, d, rev, r_ref, k_ref, v_ref, la_ref, w2_ref, a2_ref, w0_ref, a0_ref, kk_ref, ka_ref,
                  p16_ref, p32_ref, bkt_ref, bon_ref):
    C = CHUNK
    rows = pl.ds(_aligned(g * (G * C), G * C), G * C)
    r = r_ref[0, rows, :].astype(F32)
    k = k_ref[0, rows, :].astype(F32)
    v = v_ref[0, rows, :].astype(F32)
    wd = la_ref[0, rows, d * LORA_PAD:(d + 1) * LORA_PAD]
    ad = la_ref[0, rows, (2 + d) * LORA_PAD:(3 + d) * LORA_PAD]
    lw = -RWKV_DECAY_SCALE * _sigmoid(w0_ref[d:d + 1, :] + jnp.dot(wd, w2_ref[d], preferred_element_type=F32))
    a_sig = _sigmoid(a0_ref[d:d + 1, :] + jnp.dot(ad, a2_ref[d], preferred_element_type=F32))

    lane = lax.broadcasted_iota(I32, (1, LANES), 1)
    head0 = lane < RWKV_N
    kx = k * kk_ref[...]
    sq = kx * kx
    ss = jnp.where(head0, jnp.sum(jnp.where(head0, sq, 0.0), axis=1, keepdims=True),
                   jnp.sum(jnp.where(head0, 0.0, sq), axis=1, keepdims=True))
    kk = kx * lax.rsqrt(ss + 1e-6)
    k_dir = k * (1.0 + (a_sig - 1.0) * ka_ref[...])
    b = kk * a_sig
    a = -kk
    bon_ref[d, rows, :] = r * k_dir
    yield

    g3 = lambda t: t.reshape(G, C, LANES)
    lw3, r3, a3, b3, kd3, v3 = g3(lw), g3(r), g3(a), g3(b), g3(k_dir), g3(v)
    ci = lax.broadcasted_iota(I32, (C, C), 0)
    cj = lax.broadcasted_iota(I32, (C, C), 1)
    tri = jnp.where((ci <= cj) if rev else (ci >= cj), 1.0, 0.0).astype(BF16)
    tri = jnp.broadcast_to(tri[None], (G, C, C))
    hi, lo = _split2(lw3)
    cum = (jnp.einsum('gij,gjk->gik', tri, hi, preferred_element_type=F32)
           + jnp.einsum('gij,gjk->gik', tri, lo, preferred_element_type=F32))
    yield
    end = 0 if rev else C - 1
    cum_end = cum[:, end:end + 1, :]
    e_neg = jnp.exp(-cum)
    e_tail = jnp.exp(cum_end - cum)
    h0 = head0.reshape(1, 1, LANES)
    a2s = _stack_heads(a3 * jnp.exp(cum - lw3), h0)
    r2s = _stack_heads(r3 * jnp.exp(cum), h0)
    k2s = _stack_heads(kd3 * e_neg, h0)
    b2s = _stack_heads(b3 * e_neg, h0)
    v2s = _stack_heads(v3, h0)
    kh2 = _stack_heads(kd3 * e_tail, h0)
    bh2 = _stack_heads(b3 * e_tail, h0)

    strict, incl, eye_b, same_blk = _pair_masks(rev)
    eye = jnp.where(eye_b, 1.0, 0.0)
    gram = _bmm_nt(jnp.concatenate([a2s, r2s], axis=1), jnp.concatenate([k2s, b2s], axis=1))
    a_ak = jnp.where(strict, gram[:, :PAIR, :PAIR], 0.0)
    a_ab = jnp.where(strict, gram[:, :PAIR, PAIR:], 0.0)
    a_rk = jnp.where(incl, gram[:, PAIR:, :PAIR], 0.0)
    a_rb = jnp.where(incl, gram[:, PAIR:, PAIR:], 0.0).astype(BF16)
    p16_ref[d, :, 2 * PAIR:4 * PAIR, :] = jnp.concatenate([v2s.astype(BF16), a_rb], axis=1)
    bkt_ref[d] = jnp.swapaxes(jnp.concatenate([bh2, kh2], axis=1), 1, 2).astype(BF16)
    p32_ref[d, :, 2 * PAIR:3 * PAIR, :] = jnp.broadcast_to(_rows_to_cols(jnp.exp(cum_end), eye_b), (G, PAIR, LANES))
    yield
    kv = _bmm(jnp.concatenate([a_ak, a_rk], axis=1), v2s)
    p32_ref[d, :, PAIR:2 * PAIR, :] = kv[:, PAIR:]
    yield
    tinv = yield from _unit_tri_inverse(a_ab, eye, same_blk)
    tt = _bmm(tinv, jnp.concatenate([a2s, kv[:, :PAIR]], axis=2))
    p16_ref[d, :, 0:2 * PAIR, :] = jnp.concatenate([tt[:, :, :LANES], r2s], axis=1).astype(BF16)
    p32_ref[d, :, 0:PAIR, :] = tt[:, :, LANES:]


RWKV_P16_ROWS = 4 * PAIR
RWKV_P32_ROWS = 3 * PAIR
RWKV_PREP_STAGES = 14


def _rwkv_step(H, p16_ref, p32_ref, bkt_ref, d, i):
    C = CHUNK
    lhs = p16_ref[d, i, 0:2 * PAIR, :]
    v2 = p16_ref[d, i, 2 * PAIR:3 * PAIR, :]
    a_rb = p16_ref[d, i, 3 * PAIR:4 * PAIR, :]
    sr = _dot(lhs, H)
    u2 = sr[:PAIR] + p32_ref[d, i, 0:PAIR, :]
    y2 = sr[PAIR:] + p32_ref[d, i, PAIR:2 * PAIR, :] + _dot(a_rb, u2)
    y = y2[:C] + y2[C:]
    upd = _dot(bkt_ref[d, i], jnp.concatenate([u2.astype(BF16), v2], axis=0))
    return H * p32_ref[d, i, 2 * PAIR:3 * PAIR, :] + upd, y


def _rwkv_scan_kernel(r_ref, k_ref, v_ref, la_ref, w2_ref, a2_ref, g2_ref, w0_ref, a0_ref,
                      kk_ref, ka_ref, rk_ref, gng_ref, gnb_ref, o_ref, yf_scr, yb_scr, bon_scr,
                      a16, a32, akt, b16, b32, bkt, *, nC, G):
    C = CHUNK
    nG = nC // G
    GC = G * C
    refs = (r_ref, k_ref, v_ref, la_ref, w2_ref, a2_ref, w0_ref, a0_ref, kk_ref, ka_ref)

    def prepare(g, bufs):
        return [_rwkv_prepare(g, G, 0, False, *refs, *bufs, bon_scr),
                _rwkv_prepare(nG - 1 - g, G, 1, True, *refs, *bufs, bon_scr)]

    def run(g, carry, cur, nxt_g, nxt):
        state = list(carry)
        gb = nG - 1 - g

        def step(i):
            ib = G - 1 - i
            state[0], yf = _rwkv_step(state[0], *cur, 0, i)
            state[1], yb = _rwkv_step(state[1], *cur, 1, ib)
            yf_scr[pl.ds(_aligned(g * GC + i * C, C), C), :] = yf
            yb_scr[pl.ds(_aligned(gb * GC + ib * C, C), C), :] = yb

        gens = [] if nxt_g is None else prepare(nxt_g, nxt)
        _interleave(G, step, gens, RWKV_PREP_STAGES // G + 1)
        return tuple(state)

    bufs_a, bufs_b = (a16, a32, akt), (b16, b32, bkt)
    _interleave(0, None, prepare(0, bufs_a), 0)

    def body(h, carry):
        g = 2 * h
        carry = run(g, carry, bufs_a, g + 1, bufs_b)
        return run(g + 1, carry, bufs_b, g + 2, bufs_a)

    z = jnp.zeros((LANES, LANES), F32)
    carry = lax.fori_loop(0, nG // 2 - 1, body, (z, z))
    carry = run(nG - 2, carry, bufs_a, nG - 1, bufs_b)
    run(nG - 1, carry, bufs_b, None, None)

    def fin(g, _):
        sl = pl.ds(_aligned(g * GC, GC), GC)
        y = yf_scr[sl, :] + yb_scr[sl, :]
        lane = lax.broadcasted_iota(I32, (1, LANES), 1)
        head0 = lane < RWKV_N

        def seg_sum(x):
            s0 = jnp.sum(jnp.where(head0, x, 0.0), axis=1, keepdims=True)
            s1 = jnp.sum(jnp.where(head0, 0.0, x), axis=1, keepdims=True)
            return jnp.where(head0, s0, s1)

        mu = seg_sum(y) * (1.0 / RWKV_N)
        dlt = y - mu
        var = seg_sum(dlt * dlt) * (1.0 / RWKV_N)
        yn = dlt * lax.rsqrt(var + RWKV_GN_EPS) * gng_ref[...] + gnb_ref[...]
        bsum = seg_sum((bon_scr[0, sl, :] + bon_scr[1, sl, :]) * rk_ref[...])
        v = v_ref[0, sl, :].astype(F32)
        gate = jnp.dot(la_ref[0, sl, 4 * LORA_PAD:], g2_ref[...], preferred_element_type=F32)
        o_ref[0, sl, :] = ((yn + bsum * v) * gate).astype(o_ref.dtype)
        return 0

    lax.fori_loop(0, nG, fin, 0)


def rwkv_scan(z_rkv, la, w2, a2, g2, w0, a0, k_k, k_a, r_k, gn_g, gn_b, B, T):
    nC = T // CHUNK
    G = min(GROUP, nC // 2)
    assert (nC // G) % 2 == 0
    npair = RWKV_WIDTH // LANES
    nl = la.shape[-1]
    vec = lambda: pl.BlockSpec((1, LANES), lambda b, p: (0, p))
    return pl.pallas_call(
        functools.partial(_rwkv_scan_kernel, nC=nC, G=G),
        grid=(B, npair),
        in_specs=[pl.BlockSpec((1, T, LANES), lambda b, p: (b, 0, p)),
                  pl.BlockSpec((1, T, LANES), lambda b, p: (b, 0, npair + p)),
                  pl.BlockSpec((1, T, LANES), lambda b, p: (b, 0, 2 * npair + p)),
                  pl.BlockSpec((1, T, nl), lambda b, p: (b, 0, 0)),
                  pl.BlockSpec((2, LORA_PAD, LANES), lambda b, p: (0, 0, p)),
                  pl.BlockSpec((2, LORA_PAD, LANES), lambda b, p: (0, 0, p)),
                  pl.BlockSpec((GATE_LORA, LANES), lambda b, p: (0, p)),
                  pl.BlockSpec((2, LANES), lambda b, p: (0, p)),
                  pl.BlockSpec((2, LANES), lambda b, p: (0, p)),
                  vec(), vec(), vec(), vec(), vec()],
        out_specs=pl.BlockSpec((1, T, LANES), lambda b, p: (b, 0, p)),
        out_shape=jax.ShapeDtypeStruct((B, T, RWKV_WIDTH), BF16),
        scratch_shapes=[pltpu.VMEM((T, LANES), F32), pltpu.VMEM((T, LANES), F32),
                        pltpu.VMEM((2, T, LANES), F32)]
        + 2 * [pltpu.VMEM((2, G, RWKV_P16_ROWS, LANES), BF16), pltpu.VMEM((2, G, RWKV_P32_ROWS, LANES), F32),
               pltpu.VMEM((2, G, LANES, 2 * PAIR), BF16)],
        compiler_params=_params(("parallel", "arbitrary")),
        name="rwkv_scan",
    )(z_rkv, z_rkv, z_rkv, la, w2, a2, g2, w0, a0, k_k, k_a, r_k, gn_g, gn_b)


def _gdn_conv_kernel(cur_ref, prev_ref, next_ref, w_ref, o_ref, *, per_b, q_blocks, k_blocks):
    cur, prev8, next8 = _halo_load(cur_ref, prev_ref, next_ref, per_b)
    half = CONV_WIDTH // 2
    acc = jnp.zeros_like(cur)
    for i in range(CONV_WIDTH):
        acc = acc + _shift_rows(cur, prev8, next8, i - half) * w_ref[i:i + 1, :]
    y = _silu(acc)
    j = pl.program_id(1)
    cw = y.shape[1]
    outs = []
    for s in range(cw // GDN_D):
        ys = y[:, s * GDN_D:(s + 1) * GDN_D]
        nrm = lax.rsqrt(jnp.sum(ys * ys, axis=1, keepdims=True) + 1e-6)
        outs.append(ys * nrm)
    yn = jnp.concatenate(outs, axis=1) if len(outs) > 1 else outs[0]
    res = jnp.where(j < q_blocks, yn * (GDN_D ** -0.5), jnp.where(j < k_blocks, yn, y))
    o_ref[...] = res.astype(o_ref.dtype)


def gdn_conv_op(p, conv_w, T):
    M = p.shape[0]
    tm = min(512, T)
    cw = 512
    per_b = T // tm
    cur, prv, nxt = _halo_specs(tm, cw, per_b, M // 8, lambda j: j)
    return pl.pallas_call(
        functools.partial(_gdn_conv_kernel, per_b=per_b, q_blocks=GDN_KEY_WIDTH // cw,
                          k_blocks=2 * GDN_KEY_WIDTH // cw),
        grid=(M // tm, GDN_QKV // cw),
        in_specs=[cur, prv, nxt, pl.BlockSpec((CONV_WIDTH, cw), lambda i, j: (0, j))],
        out_specs=pl.BlockSpec((tm, cw), lambda i, j: (i, j)),
        out_shape=jax.ShapeDtypeStruct((M, GDN_QKV), BF16),
        compiler_params=_params(("parallel", "parallel")),
        name="gdn_conv",
    )(p, p, p, conv_w)


def _gdn_gate_kernel(ba_ref, alog_ref, dtb_ref, o_ref):
    C = CHUNK
    x = ba_ref[...]
    tm = x.shape[0]
    nh = 2 * GDN_V_HEADS
    beta = _sigmoid(x)
    a = x + dtb_ref[...]
    sp = jnp.maximum(a, 0.0) + jnp.log1p(jnp.exp(-jnp.abs(a)))
    g = -jnp.exp(alog_ref[...]) * sp
    i = lax.broadcasted_iota(I32, (C, C), 0)
    j = lax.broadcasted_iota(I32, (C, C), 1)
    lower = jnp.where(i >= j, 1.0, 0.0).astype(BF16)
    upper = jnp.where(i <= j, 1.0, 0.0).astype(BF16)
    col = lax.broadcasted_iota(I32, (C, 2 * nh), 1)
    for c in range(tm // C):
        gc = g[c * C:(c + 1) * C]
        hi, mid, lo = _split3(gc)
        fw = (jnp.dot(lower, hi, preferred_element_type=F32) + jnp.dot(lower, mid, preferred_element_type=F32)
              + jnp.dot(lower, lo, preferred_element_type=F32))
        bw = (jnp.dot(upper, hi, preferred_element_type=F32) + jnp.dot(upper, mid, preferred_element_type=F32)
              + jnp.dot(upper, lo, preferred_element_type=F32))
        o_ref[c * C:(c + 1) * C, :] = jnp.where(col < nh, beta[c * C:(c + 1) * C],
                                                jnp.where(col < nh + GDN_V_HEADS, fw, bw))


def gdn_gates(ba, a_log, dt_bias, T):
    M = ba.shape[0]
    tm = min(512, T)
    nh = 2 * GDN_V_HEADS
    return pl.pallas_call(
        _gdn_gate_kernel,
        grid=(M // tm,),
        in_specs=[pl.BlockSpec((tm, 2 * nh), lambda i: (i, 0)),
                  pl.BlockSpec((1, 2 * nh), lambda i: (0, 0)),
                  pl.BlockSpec((1, 2 * nh), lambda i: (0, 0))],
        out_specs=pl.BlockSpec((tm, 2 * nh), lambda i: (i, 0)),
        out_shape=jax.ShapeDtypeStruct((M, 2 * nh), F32),
        compiler_params=_params(("parallel",)),
        name="gdn_gates",
    )(ba, a_log, dt_bias)


def _gdn_prepare(g, G, d, rev, kh, q_ref, k_ref, v_ref, bg_ref, p16_ref, u_ref, dec_ref):
    C = CHUNK
    rows = pl.ds(_aligned(g * (G * C), G * C), G * C)
    q = q_ref[0, rows, :].astype(F32).reshape(G, C, GDN_D)
    k = k_ref[0, rows, :].astype(F32).reshape(G, C, GDN_D)
    v = v_ref[0, rows, :].astype(F32).reshape(G, C, 2 * GDN_D)
    v2 = jnp.concatenate([v[:, :, :GDN_D], v[:, :, GDN_D:]], axis=1)
    k2 = jnp.concatenate([k, k], axis=1)
    q2 = jnp.concatenate([q, q], axis=1)
    chunks = pl.ds(g * G, G)
    beta_row = bg_ref[0, chunks, pl.ds(d * GDN_K_HEADS + kh, 1), :]
    gc_row = bg_ref[0, chunks, pl.ds((2 + d) * GDN_K_HEADS + kh, 1), :]
    strict, incl, eye_b, same_blk = _pair_masks(rev)
    eye = jnp.where(eye_b, 1.0, 0.0)
    beta_col = _rows_to_cols(beta_row, eye_b)
    gc_col = _rows_to_cols(gc_row, eye_b)
    decay = jnp.where(incl, jnp.exp(jnp.where(incl, gc_col - gc_row, 0.0)), 0.0)
    gram = _bmm_nt(jnp.concatenate([k2, q2], axis=1), k2)
    lower = jnp.where(strict, gram[:, :PAIR] * beta_col * decay, 0.0)
    eg = jnp.exp(gc_col)
    end = 0 if rev else C - 1
    gl0 = gc_row[:, :, end:end + 1]
    gl1 = gc_row[:, :, C + end:C + end + 1]
    rowi = lax.broadcasted_iota(I32, (1, PAIR, 1), 1)
    gl_col = jnp.where(rowi < C, gl0, gl1)
    lane2 = lax.broadcasted_iota(I32, (1, 1, 2 * GDN_D), 2)
    decay_s = jnp.where(lane2 < GDN_D, jnp.exp(gl0), jnp.exp(gl1))
    p16_ref[d, :, PAIR:2 * PAIR, :] = (q2 * eg).astype(BF16)
    p16_ref[d, :, 2 * PAIR:3 * PAIR, :] = (gram[:, PAIR:] * decay).astype(BF16)
    p16_ref[d, :, 3 * PAIR:4 * PAIR, :] = jnp.swapaxes(k2 * jnp.exp(gl_col - gc_col), 1, 2).astype(BF16)
    dec_ref[d] = jnp.broadcast_to(decay_s, (G, 8, 2 * GDN_D))
    yield
    t_inv = yield from _unit_tri_inverse(-lower, eye, same_blk)
    kb = k2 * beta_col
    uw = _bmm(t_inv, jnp.concatenate([v2 * beta_col, kb * eg], axis=2))
    u_ref[d] = uw[:, :, :GDN_D]
    p16_ref[d, :, 0:PAIR, :] = uw[:, :, GDN_D:].astype(BF16)


GDN_P16_ROWS = 4 * PAIR
GDN_PREP_STAGES = 10


def _gdn_step(S, p16_ref, u_ref, dec_ref, d, i):
    C = CHUNK
    ws = _dot(p16_ref[d, i, 0:2 * PAIR, :], S)
    head0_rows = lax.broadcasted_iota(I32, (PAIR, 1), 0) < C
    w_s = jnp.where(head0_rows, ws[:PAIR, :GDN_D], ws[:PAIR, GDN_D:])
    q_s = jnp.where(head0_rows, ws[PAIR:, :GDN_D], ws[PAIR:, GDN_D:])
    v_new = u_ref[d, i] - w_s
    o2 = q_s + _dot(p16_ref[d, i, 2 * PAIR:3 * PAIR, :], v_new)
    vcat = jnp.concatenate([jnp.where(head0_rows, v_new, 0.0), jnp.where(head0_rows, 0.0, v_new)], axis=1)
    S_new = S * dec_ref[d, i, 0:1, :] + _dot(p16_ref[d, i, 3 * PAIR:4 * PAIR, :], vcat)
    return S_new, jnp.concatenate([o2[:C], o2[C:]], axis=1)


def _gdn_scan_kernel(q_ref, k_ref, v_ref, z_ref, bg_ref, ng_ref, o_ref, of_scr, ob_scr,
                     a16, au, ad, b16, bu, bd, *, nC, G):
    C = CHUNK
    nG = nC // G
    GC = G * C
    kh = pl.program_id(1)

    def prepare(g, bufs):
        return [_gdn_prepare(g, G, 0, False, kh, q_ref, k_ref, v_ref, bg_ref, *bufs),
                _gdn_prepare(nG - 1 - g, G, 1, True, kh, q_ref, k_ref, v_ref, bg_ref, *bufs)]

    def run(g, carry, cur, nxt_g, nxt):
        state = list(carry)
        gb = nG - 1 - g

        def step(i):
            ib = G - 1 - i
            state[0], of = _gdn_step(state[0], *cur, 0, i)
            state[1], ob = _gdn_step(state[1], *cur, 1, ib)
            of_scr[pl.ds(_aligned(g * GC + i * C, C), C), :] = of
            ob_scr[pl.ds(_aligned(gb * GC + ib * C, C), C), :] = ob

        gens = [] if nxt_g is None else prepare(nxt_g, nxt)
        _interleave(G, step, gens, GDN_PREP_STAGES // G + 1)
        return tuple(state)

    bufs_a, bufs_b = (a16, au, ad), (b16, bu, bd)
    _interleave(0, None, prepare(0, bufs_a), 0)

    def body(h, carry):
        g = 2 * h
        carry = run(g, carry, bufs_a, g + 1, bufs_b)
        return run(g + 1, carry, bufs_b, g + 2, bufs_a)

    z0 = jnp.zeros((GDN_D, 2 * GDN_D), F32)
    carry = lax.fori_loop(0, nG // 2 - 1, body, (z0, z0))
    carry = run(nG - 2, carry, bufs_a, nG - 1, bufs_b)
    run(nG - 1, carry, bufs_b, None, None)

    def fin(g, _):
        sl = pl.ds(_aligned(g * GC, GC), GC)
        o = of_scr[sl, :] + ob_scr[sl, :]
        z = z_ref[0, sl, :].astype(F32)
        for s in range(2):
            os_ = o[:, s * GDN_D:(s + 1) * GDN_D]
            on = os_ * lax.rsqrt(jnp.mean(os_ * os_, axis=-1, keepdims=True) + 1e-6) * ng_ref[...]
            o_ref[0, sl, s * GDN_D:(s + 1) * GDN_D] = (on * _silu(z[:, s * GDN_D:(s + 1) * GDN_D])).astype(o_ref.dtype)
        return 0

    lax.fori_loop(0, nG, fin, 0)


def gdn_scan(qkv, z_src, z_col0, bg, norm_g, B, T):
    nC = T // CHUNK
    G = min(GROUP, nC // 2)
    assert (nC // G) % 2 == 0
    kb = GDN_KEY_WIDTH // GDN_D
    vb = 2 * GDN_KEY_WIDTH // (2 * GDN_D)
    return pl.pallas_call(
        functools.partial(_gdn_scan_kernel, nC=nC, G=G),
        grid=(B, GDN_K_HEADS),
        in_specs=[pl.BlockSpec((1, T, GDN_D), lambda b, h: (b, 0, h)),
                  pl.BlockSpec((1, T, GDN_D), lambda b, h: (b, 0, kb + h)),
                  pl.BlockSpec((1, T, 2 * GDN_D), lambda b, h: (b, 0, vb + h)),
                  pl.BlockSpec((1, T, 2 * GDN_D), lambda b, h: (b, 0, z_col0 + h)),
                  pl.BlockSpec((1, nC, 4 * GDN_K_HEADS, PAIR), lambda b, h: (b, 0, 0, 0)),
                  pl.BlockSpec((1, GDN_D), lambda b, h: (0, 0))],
        out_specs=pl.BlockSpec((1, T, 2 * GDN_D), lambda b, h: (b, 0, h)),
        out_shape=jax.ShapeDtypeStruct((B, T, GDN_VAL_WIDTH), BF16),
        scratch_shapes=[pltpu.VMEM((T, 2 * GDN_D), F32), pltpu.VMEM((T, 2 * GDN_D), F32)]
        + 2 * [pltpu.VMEM((2, G, GDN_P16_ROWS, GDN_D), BF16), pltpu.VMEM((2, G, PAIR, GDN_D), F32),
               pltpu.VMEM((2, G, 8, 2 * GDN_D), F32)],
        compiler_params=_params(("parallel", "arbitrary")),
        name="gdn_scan",
    )(qkv, qkv, qkv, z_src, bg, norm_g)


def _router_kernel(x_ref, sc_ref, sh_ref, wr_ref, h_ref, aff_ref):
    h = x_ref[...] * (1.0 + sc_ref[0]) + sh_ref[0]
    h_ref[...] = h.astype(BF16)
    hh, hm, hl = _split3(h)
    wh, wm, wl = _split3(wr_ref[...])
    nt = lambda a, b: lax.dot_general(a, b, (((1,), (1,)), ((), ())), preferred_element_type=F32)
    logits = (nt(wh, hh) + (nt(wh, hm) + nt(wm, hh)) + (nt(wh, hl) + nt(wl, hh) + nt(wm, hm)))
    m = jnp.max(logits, axis=0, keepdims=True)
    e = jnp.exp(logits - m)
    aff_ref[0] = e / jnp.sum(e, axis=0, keepdims=True)


def moe_router_op(x, sc, sh, wr_t, B, T):
    M, D = x.shape
    E = wr_t.shape[0]
    tm = min(512, T)
    per_b = T // tm
    return pl.pallas_call(
        _router_kernel,
        grid=(M // tm,),
        in_specs=[pl.BlockSpec((tm, D), lambda i: (i, 0)),
                  pl.BlockSpec((1, 1, D), lambda i: (i // per_b, 0, 0)),
                  pl.BlockSpec((1, 1, D), lambda i: (i // per_b, 0, 0)),
                  pl.BlockSpec((E, D), lambda i: (0, 0))],
        out_specs=[pl.BlockSpec((tm, D), lambda i: (i, 0)),
                   pl.BlockSpec((1, E, tm), lambda i: (i // per_b, 0, i % per_b))],
        out_shape=[jax.ShapeDtypeStruct((M, D), BF16), jax.ShapeDtypeStruct((B, E, T), F32)],
        compiler_params=_params(("parallel",)),
        name="moe_router",
    )(x, sc, sh, wr_t)


def _lane_cumsum(m01, blk):
    E, T = m01.shape
    i = lax.broadcasted_iota(I32, (blk, blk), 0)
    j = lax.broadcasted_iota(I32, (blk, blk), 1)
    upper = jnp.where(i <= j, 1.0, 0.0).astype(BF16)
    carry = jnp.zeros((E, 1), F32)
    parts = []
    for c in range(T // blk):
        seg = m01[:, c * blk:(c + 1) * blk]
        cs = jnp.dot(seg.astype(BF16), upper, preferred_element_type=F32) + carry
        parts.append(cs)
        carry = cs[:, blk - 1:blk]
    return jnp.concatenate(parts, axis=1) if len(parts) > 1 else parts[0]


def _topk_kernel(aff_ref, pos_ref, los_ref, *, cap, ts):
    a = aff_ref[0]
    E, T = a.shape
    bits = pltpu.bitcast(a, I32)

    def body(i, t):
        cand = t | lax.shift_left(jnp.int32(1), 30 - i)
        cnt = jnp.sum((bits >= cand).astype(I32), axis=1, keepdims=True)
        return jnp.where(cnt >= cap, cand, t)

    thr = lax.fori_loop(0, 31, body, jnp.zeros((E, 1), I32))
    gt = bits > thr
    eq = bits == thr
    need = cap - jnp.sum(gt.astype(I32), axis=1, keepdims=True)
    blk = min(512, T)
    eq01 = jnp.where(eq, 1.0, 0.0)
    eq_rank = _lane_cumsum(eq01, blk) - eq01
    sel = jnp.logical_or(gt, jnp.logical_and(eq, eq_rank < need.astype(F32)))
    sel01 = jnp.where(sel, 1.0, 0.0)
    slot = _lane_cumsum(sel01, blk) - sel01
    pos_ref[0] = jnp.where(sel, slot.astype(I32), -1)
    ti = lax.broadcasted_iota(I32, (T, LANES), 0)
    wi = lax.broadcasted_iota(I32, (T, LANES), 1)
    before = jnp.where(ti < wi * ts, 1.0, 0.0).astype(BF16)
    los_ref[0] = jnp.dot(sel01.astype(BF16), before, preferred_element_type=F32).astype(I32)


def moe_topk(aff_t, cap, ts):
    B, E, T = aff_t.shape
    return pl.pallas_call(
        functools.partial(_topk_kernel, cap=cap, ts=ts),
        grid=(B,),
        in_specs=[pl.BlockSpec((1, E, T), lambda b: (b, 0, 0))],
        out_specs=[pl.BlockSpec((1, E, T), lambda b: (b, 0, 0)),
                   pl.BlockSpec((1, E, LANES), lambda b: (b, 0, 0))],
        out_shape=[jax.ShapeDtypeStruct((B, E, T), I32), jax.ShapeDtypeStruct((B, E, LANES), I32)],
        compiler_params=_params(("parallel",)),
        name="moe_topk",
    )(aff_t)


def _slot_windows(cap, win):
    win = min(win, cap)
    return win, cap // win


def _gather_kernel(los_ref, pos_ref, h_ref, o_ref, acc, *, cap, nt, ts):
    b, e = pl.program_id(0), pl.program_id(1)
    base = (b * pl.num_programs(1) + e) * (nt + 1)
    win, nwin = _slot_windows(cap, LANES)
    acc[...] = jnp.zeros(acc.shape, F32)
    for c in range(nt):
        lo = los_ref[base + c]
        hi = los_ref[base + c + 1]
        for w in range(nwin):
            @pl.when(jnp.logical_and(lo < (w + 1) * win, hi > w * win))
            def _():
                slot = lax.broadcasted_iota(I32, (win, ts), 0) + w * win
                onehot = jnp.where(slot == pos_ref[0, 0, :, c * ts:(c + 1) * ts], 1.0, 0.0).astype(BF16)
                acc[w * win:(w + 1) * win, :] += jnp.dot(onehot, h_ref[c * ts:(c + 1) * ts, :],
                                                         preferred_element_type=F32)
    o_ref[0, 0] = acc[...].astype(o_ref.dtype)


def moe_gather(los, pos4, h, B, T, cap, ts):
    E = pos4.shape[1]
    D = h.shape[1]
    nt = T // ts
    return pl.pallas_call(
        functools.partial(_gather_kernel, cap=cap, nt=nt, ts=ts),
        grid_spec=pltpu.PrefetchScalarGridSpec(
            num_scalar_prefetch=1,
            grid=(B, E),
            in_specs=[pl.BlockSpec((1, 1, 1, T), lambda b, e, s: (b, e, 0, 0)),
                      pl.BlockSpec((T, D), lambda b, e, s: (b, 0))],
            out_specs=pl.BlockSpec((1, 1, cap, D), lambda b, e, s: (b, e, 0, 0)),
            scratch_shapes=[pltpu.VMEM((cap, D), F32)]),
        out_shape=jax.ShapeDtypeStruct((B, E, cap, D), BF16),
        compiler_params=_params(("parallel", "arbitrary")),
        name="moe_gather",
    )(los, pos4, h)


FFN_SPLIT = 4


def _ffn_kernel(x_ref, wg_ref, wu_ref, wd_ref, o_ref, acc, wg_s, wu_s, wd_s):
    q, b = pl.program_id(1), pl.program_id(2)

    @pl.when(b == 0)
    def _():
        wg_s[...] = wg_ref[0].astype(BF16)
        wu_s[...] = wu_ref[0].astype(BF16)
        wd_s[...] = wd_ref[0].astype(BF16)

    x = x_ref[0, 0]
    g = jnp.dot(x, wg_s[...], preferred_element_type=F32)
    u = jnp.dot(x, wu_s[...], preferred_element_type=F32)
    hid = (_silu(g) * u).astype(BF16)
    part = jnp.dot(hid, wd_s[...], preferred_element_type=F32)

    @pl.when(q == 0)
    def _():
        acc[b] = part

    @pl.when(jnp.logical_and(q > 0, q < FFN_SPLIT - 1))
    def _():
        acc[b] += part

    @pl.when(q == FFN_SPLIT - 1)
    def _():
        o_ref[0, 0] = (acc[b] + part).astype(o_ref.dtype)


def moe_ffn(xs, wg, wu, wd, layer):
    B, E, cap, D = xs.shape
    Fh = wg.shape[-1]
    fq = Fh // FFN_SPLIT
    e0 = layer * E
    out_map = lambda e, q, b: (jnp.where(q == FFN_SPLIT - 1, b, 0), e, 0, 0)
    return pl.pallas_call(
        _ffn_kernel,
        grid=(E, FFN_SPLIT, B),
        in_specs=[pl.BlockSpec((1, 1, cap, D), lambda e, q, b: (b, e, 0, 0)),
                  pl.BlockSpec((1, D, fq), lambda e, q, b: (e0 + e, 0, q)),
                  pl.BlockSpec((1, D, fq), lambda e, q, b: (e0 + e, 0, q)),
                  pl.BlockSpec((1, fq, D), lambda e, q, b: (e0 + e, q, 0))],
        out_specs=pl.BlockSpec((1, 1, cap, D), out_map),
        out_shape=jax.ShapeDtypeStruct((B, E, cap, D), BF16),
        scratch_shapes=[pltpu.VMEM((B, cap, D), F32), pltpu.VMEM((D, fq), BF16), pltpu.VMEM((D, fq), BF16),
                        pltpu.VMEM((fq, D), BF16)],
        compiler_params=_params(("parallel", "arbitrary", "arbitrary")),
        name="moe_ffn",
    )(xs, wg, wu, wd)


def _combine_kernel(los_ref, pos_ref, gate_ref, y_ref, x_ref, g_ref, lg_ref, lb_ref, o_ref, acc, *, cap, nt):
    b, i, e = pl.program_id(0), pl.program_id(1), pl.program_id(2)
    base = (b * pl.num_programs(2) + e) * (nt + 1) + i
    lo = los_ref[base]
    hi = los_ref[base + 1]
    pos = pos_ref[0].astype(F32)
    gate = gate_ref[0]
    tt, E = pos.shape
    lane = lax.broadcasted_iota(I32, (tt, E), 1)
    pcol = jnp.sum(jnp.where(lane == e, pos, 0.0), axis=1, keepdims=True)
    gcol = jnp.sum(jnp.where(lane == e, gate, 0.0), axis=1, keepdims=True)
    win, nwin = _slot_windows(cap, MXU_DEPTH)

    @pl.when(e == 0)
    def _():
        acc[...] = jnp.zeros(acc.shape, F32)

    for w in range(nwin):
        @pl.when(jnp.logical_and(lo < (w + 1) * win, hi > w * win))
        def _():
            slot = (lax.broadcasted_iota(I32, (tt, win), 1) + w * win).astype(F32)
            onehot = jnp.where(slot == pcol, 1.0, 0.0).astype(BF16)
            acc[...] += gcol * jnp.dot(onehot, y_ref[0, 0, w * win:(w + 1) * win, :], preferred_element_type=F32)

    @pl.when(e == pl.num_programs(2) - 1)
    def _():
        v = DEEPNORM_ALPHA * x_ref[...] + g_ref[0] * acc[...]
        o_ref[...] = _layer_norm_rows(v, lg_ref[...], lb_ref[...])


def moe_combine(los, pos_tm, aff_tm, yd, x, g, lg, lb, B, T, cap, ts):
    E = pos_tm.shape[-1]
    M, D = x.shape
    nt = T // ts
    return pl.pallas_call(
        functools.partial(_combine_kernel, cap=cap, nt=nt),
        grid_spec=pltpu.PrefetchScalarGridSpec(
            num_scalar_prefetch=1,
            grid=(B, nt, E),
            in_specs=[pl.BlockSpec((1, ts, E), lambda b, i, e, s: (b, i, 0)),
                      pl.BlockSpec((1, ts, E), lambda b, i, e, s: (b, i, 0)),
                      pl.BlockSpec((1, 1, cap, D), lambda b, i, e, s: (b, e, 0, 0)),
                      pl.BlockSpec((ts, D), lambda b, i, e, s: (b * nt + i, 0)),
                      pl.BlockSpec((1, 1, D), lambda b, i, e, s: (b, 0, 0)),
                      pl.BlockSpec((1, D), lambda b, i, e, s: (0, 0)),
                      pl.BlockSpec((1, D), lambda b, i, e, s: (0, 0))],
            out_specs=pl.BlockSpec((ts, D), lambda b, i, e, s: (b * nt + i, 0)),
            scratch_shapes=[pltpu.VMEM((ts, D), F32)]),
        out_shape=jax.ShapeDtypeStruct((M, D), F32),
        compiler_params=_params(("parallel", "parallel", "arbitrary")),
        name="moe_combine",
    )(los, pos_tm, aff_tm, yd, x, g, lg, lb)


def moe_layer(x, sc, sh, g, lg, lb, w_router, w_gate, w_up, w_down, layer, B, T):
    cap = CAPACITY_FACTOR * T // N_EXPERTS
    ts = min(512, T)
    nt = T // ts
    h, aff_t = moe_router_op(x, sc, sh, w_router.T, B, T)
    pos, los = moe_topk(aff_t, cap, ts)
    los = los[:, :, :nt + 1].reshape(-1)
    xs = moe_gather(los, pos[:, :, None, :], h, B, T, cap, ts)
    yd = moe_ffn(xs, w_gate, w_up, w_down, layer)
    pos_tm = jnp.swapaxes(pos, 1, 2)
    aff_tm = jnp.swapaxes(aff_t, 1, 2)
    return moe_combine(los, pos_tm, aff_tm, yd, x, g, lg, lb, B, T, cap, ts)


def _pad_cols(w, n):
    return jnp.pad(w, ((0, 0), (0, n - w.shape[1])))


def _rot_cols(w):
    half = w.shape[-1] // 2
    return jnp.concatenate([-w[..., half:], w[..., :half]], axis=-1)


def _even_w_in(w):
    D = w.shape[0]
    o = MLA_IN
    rkv = w[:, o:o + 3 * RWKV_WIDTH]
    cq = w[:, :Q_LORA]
    ckv = w[:, Q_LORA:Q_LORA + KV_LORA]
    kr = w[:, Q_LORA + KV_LORA:MLA_IN]
    l0 = o + 3 * RWKV_WIDTH
    lora = [_pad_cols(w[:, l0 + i * DECAY_LORA:l0 + (i + 1) * DECAY_LORA], LORA_PAD) for i in range(4)]
    gd = w[:, l0 + 4 * DECAY_LORA:]
    pad = jnp.zeros((D, LANES), w.dtype)
    out = jnp.concatenate([rkv, cq, ckv, kr, _rot_cols(kr), pad] + lora + [gd], axis=1)
    assert out.shape[1] == EVEN_COLS
    return out.astype(BF16)


def _even_mu(mu):
    rkv = mu[:3 * RWKV_WIDTH]
    l0 = 3 * RWKV_WIDTH
    lora = [jnp.pad(mu[l0 + i * DECAY_LORA:l0 + (i + 1) * DECAY_LORA], (0, LORA_PAD - DECAY_LORA)) for i in range(4)]
    gd = mu[l0 + 4 * DECAY_LORA:]
    return rkv.reshape(1, -1), jnp.concatenate(lora + [gd]).reshape(1, -1)


def _pad_rows(w, n):
    return jnp.pad(w, ((0, 0), (0, n - w.shape[1]), (0, 0)))


def even_mixer(x, sc, sh, positions_tables, w_in, shift_mu, q_norm, w_uq, kv_norm, w_ukv, w0, w2, a0, a2, g2,
               k_k, k_a, r_k, gn_g, gn_b, B, T):
    cs, sn = positions_tables
    H = MLA_HEADS
    p = mm_mod(x, sc, sh, _even_w_in(w_in), T, 1024, 1536, F32)
    wq = w_uq.reshape(Q_LORA, H, NOPE_DIM + ROPE_DIM)
    wq = jnp.concatenate([wq, _rot_cols(wq[..., NOPE_DIM:])], axis=-1)
    wq = jnp.swapaxes(wq, 0, 1).astype(BF16)
    wkv = w_ukv.reshape(KV_LORA, H, NOPE_DIM + V_DIM)
    wk = jnp.swapaxes(wkv[..., :NOPE_DIM], 0, 1).astype(BF16)
    wvt = jnp.transpose(wkv[..., NOPE_DIM:], (1, 2, 0)).astype(BF16)
    q, k, vt = mla_proj(p, cs, sn, q_norm.reshape(1, -1), kv_norm.reshape(1, -1), wq, wk, wvt, B, T)
    attn = mla_attention(q, k, vt)
    mu_rkv, mu_lora = _even_mu(shift_mu)
    z_rkv = shift_mix(p, mu_rkv, T, 0, 3 * RWKV_WIDTH, 1024, False)
    la = shift_mix(p, mu_lora, T, 8448, 768, 768, True)
    y_rwkv = rwkv_scan(z_rkv.reshape(B, T, -1), la.reshape(B, T, -1),
                       _pad_rows(w2, LORA_PAD).astype(BF16), _pad_rows(a2, LORA_PAD).astype(BF16),
                       g2.astype(BF16), w0, a0, k_k.reshape(1, -1), k_a.reshape(1, -1), r_k.reshape(1, -1),
                       gn_g.reshape(1, -1), gn_b.reshape(1, -1), B, T)
    return attn.reshape(B * T, -1), y_rwkv.reshape(B * T, -1)


def odd_mixer(x, sc, sh, w_in, conv_w, a_log, dt_bias, norm_g, B, T):
    M = B * T
    nC = T // CHUNK
    wb = w_in.astype(BF16)
    n_main = GDN_QKV + GDN_VAL_WIDTH
    p = mm_mod(x, sc, sh, wb[:, :n_main], T, 1024, 1536, F32)
    ba = mm_mod(x, sc, sh, wb[:, n_main:], T, 512, 128, F32)
    qkv = gdn_conv_op(p, conv_w, T)
    nh = 2 * GDN_V_HEADS
    bg = gdn_gates(ba, jnp.pad(a_log.reshape(1, -1), ((0, 0), (nh, 0))),
                   jnp.pad(dt_bias.reshape(1, -1), ((0, 0), (nh, 0))), T)
    bg = bg.reshape(B, nC, CHUNK, 2, 2, GDN_K_HEADS, 2)
    bg = jnp.transpose(bg, (0, 1, 3, 4, 5, 6, 2)).reshape(B, nC, 4 * GDN_K_HEADS, PAIR)
    o = gdn_scan(qkv.reshape(B, T, -1), p.reshape(B, T, -1), GDN_QKV // (2 * GDN_D), bg, norm_g.reshape(1, -1), B, T)
    return o.reshape(M, -1)


def kernel(x, c, positions, ada_w, ada_b, ln_g, ln_b, e_w_in, e_shift_mu, mla_q_norm, mla_w_uq, mla_kv_norm,
           mla_w_ukv, rwkv_w0, rwkv_w2, rwkv_a0, rwkv_a2, rwkv_g2, rwkv_k_k, rwkv_k_a, rwkv_r_k, rwkv_gn_g,
           rwkv_gn_b, e_w_out, o_w_in, gdn_conv, gdn_a_log, gdn_dt_bias, gdn_norm, o_w_out, moe_router,
           moe_w_gate, moe_w_up, moe_w_down):
    B, T, D = x.shape
    M = B * T
    depth = ada_w.shape[0]
    mod = ada_modulation(c, ada_w, ada_b)
    tables = rope_tables(positions)
    xf = x.reshape(M, D)
    n_exp = moe_w_gate.shape[1]
    wg_all = moe_w_gate.reshape(depth * n_exp, *moe_w_gate.shape[2:])
    wu_all = moe_w_up.reshape(depth * n_exp, *moe_w_up.shape[2:])
    wd_all = moe_w_down.reshape(depth * n_exp, *moe_w_down.shape[2:])
    for i in range(depth):
        m6 = mod[i].reshape(B, 6, 1, D)
        sh_m, sc_m, g_m, sh_f, sc_f, g_f = (m6[:, n] for n in range(6))
        j = i // 2
        lg0, lb0 = ln_g[i, 0].reshape(1, D), ln_b[i, 0].reshape(1, D)
        lg1, lb1 = ln_g[i, 1].reshape(1, D), ln_b[i, 1].reshape(1, D)
        if i % 2 == 0:
            a0, a1 = even_mixer(xf, sc_m, sh_m, tables, e_w_in[j], e_shift_mu[j], mla_q_norm[j], mla_w_uq[j],
                                mla_kv_norm[j], mla_w_ukv[j], rwkv_w0[j], rwkv_w2[j], rwkv_a0[j], rwkv_a2[j],
                                rwkv_g2[j], rwkv_k_k[j], rwkv_k_a[j], rwkv_r_k[j], rwkv_gn_g[j], rwkv_gn_b[j],
                                B, T)
            xf = mm_ln(a0, 0, a1, 0, e_w_out[j].astype(BF16), xf, g_m, lg0, lb0, T, 512)
        else:
            o = odd_mixer(xf, sc_m, sh_m, o_w_in[j], gdn_conv[j], gdn_a_log[j], gdn_dt_bias[j], gdn_norm[j], B, T)
            xf = mm_ln(o, 0, o, 1, o_w_out[j].astype(BF16), xf, g_m, lg0, lb0, T, 512)
        xf = moe_layer(xf, sc_f, sh_f, g_f, lg1, lb1, moe_router[i], wg_all, wu_all, wd_all, i, B, T)
    return xf.reshape(B, T, D)
```

```python
import functools
import math

import jax
import jax.numpy as jnp
from jax import lax
from jax.experimental import pallas as pl
from jax.experimental.pallas import tpu as pltpu

F32 = jnp.float32
BF16 = jnp.bfloat16
I32 = jnp.int32

DEPTH = 4
DEEPNORM_ALPHA = (2 * DEPTH) ** 0.25
MLA_HEADS = 16
Q_LORA = 1536
KV_LORA = 512
NOPE_DIM = 128
ROPE_DIM = 64
V_DIM = 128
ROPE_THETA = 10000.0
MLA_IN = Q_LORA + KV_LORA + ROPE_DIM
RWKV_HEADS = 32
RWKV_N = 64
RWKV_WIDTH = RWKV_HEADS * RWKV_N
DECAY_LORA = 96
AAA_LORA = 96
GATE_LORA = 256
RWKV_DECAY_SCALE = 0.6065306597126334
RWKV_GN_EPS = 64e-5
GDN_K_HEADS = 16
GDN_V_HEADS = 32
GDN_D = 128
GDN_KEY_WIDTH = GDN_K_HEADS * GDN_D
GDN_VAL_WIDTH = GDN_V_HEADS * GDN_D
GDN_QKV = 2 * GDN_KEY_WIDTH + GDN_VAL_WIDTH
CONV_WIDTH = 5
N_EXPERTS = 16
CAPACITY_FACTOR = 2

LANES = 128
MXU_DEPTH = 256
CHUNK = 64
LOG2_CHUNK = 6
PAIR = 2 * CHUNK
GROUP = 8
INV_BLOCK = 16
LOG2_INV_BLOCK = 4
LORA_PAD = 128
EVEN_COLS = 9216
VMEM_LIMIT = 56 * 1024 * 1024


def _params(sem, vmem=VMEM_LIMIT):
    return pltpu.CompilerParams(dimension_semantics=sem, vmem_limit_bytes=vmem)


def _dot(a, b):
    return jnp.dot(a.astype(BF16), b.astype(BF16), preferred_element_type=F32)


def _dot_nt(a, b):
    return lax.dot_general(a.astype(BF16), b.astype(BF16), (((1,), (1,)), ((), ())),
                           preferred_element_type=F32)


def _dot_tn(a, b):
    return lax.dot_general(a.astype(BF16), b.astype(BF16), (((0,), (0,)), ((), ())),
                           preferred_element_type=F32)


def _bmm(a, b):
    return jnp.einsum('gij,gjk->gik', a.astype(BF16), b.astype(BF16), preferred_element_type=F32)


def _bmm_nt(a, b):
    return jnp.einsum('gik,gjk->gij', a.astype(BF16), b.astype(BF16), preferred_element_type=F32)


def _split2(x):
    hi = x.astype(BF16)
    lo = (x - hi.astype(F32)).astype(BF16)
    return hi, lo


def _split3(x):
    hi = x.astype(BF16)
    r1 = x - hi.astype(F32)
    mid = r1.astype(BF16)
    lo = (r1 - mid.astype(F32)).astype(BF16)
    return hi, mid, lo


def _sigmoid(x):
    return 1.0 / (1.0 + jnp.exp(-x))


def _silu(x):
    return x * _sigmoid(x)


def _ada_kernel(c_ref, w_ref, b_ref, o_ref):
    c = c_ref[...]
    o_ref[0] = _dot(_silu(c), w_ref[0]) + b_ref[0]


def ada_modulation(c, ada_w, ada_b):
    L, D, N = ada_w.shape
    B = c.shape[0]
    rows = 8
    cp = jnp.zeros((rows, D), F32).at[:B].set(c)
    tn = 1024
    out = pl.pallas_call(
        _ada_kernel,
        grid=(L, N // tn),
        in_specs=[pl.BlockSpec((rows, D), lambda l, j: (0, 0)),
                  pl.BlockSpec((1, D, tn), lambda l, j: (l, 0, j)),
                  pl.BlockSpec((1, 1, tn), lambda l, j: (l, 0, j))],
        out_specs=pl.BlockSpec((1, rows, tn), lambda l, j: (l, 0, j)),
        out_shape=jax.ShapeDtypeStruct((L, rows, N), F32),
        compiler_params=_params(("parallel", "parallel")),
        name="ada_mod",
    )(cp, ada_w, ada_b.reshape(L, 1, N))
    return out[:, :B]


def _mm_mod_kernel(x_ref, sc_ref, sh_ref, w_ref, o_ref, h_scr):
    @pl.when(pl.program_id(1) == 0)
    def _():
        h_scr[...] = (x_ref[...] * (1.0 + sc_ref[0]) + sh_ref[0]).astype(BF16)

    o_ref[...] = jnp.dot(h_scr[...], w_ref[...], preferred_element_type=F32).astype(o_ref.dtype)


def mm_mod(x, sc, sh, w, T, tm, tn, out_dtype):
    M, D = x.shape
    N = w.shape[1]
    tm = min(tm, T)
    tn = min(tn, N)
    per_b = T // tm
    return pl.pallas_call(
        _mm_mod_kernel,
        grid=(M // tm, N // tn),
        in_specs=[pl.BlockSpec((tm, D), lambda i, j: (i, 0)),
                  pl.BlockSpec((1, 1, D), lambda i, j: (i // per_b, 0, 0)),
                  pl.BlockSpec((1, 1, D), lambda i, j: (i // per_b, 0, 0)),
                  pl.BlockSpec((D, tn), lambda i, j: (0, j))],
        out_specs=pl.BlockSpec((tm, tn), lambda i, j: (i, j)),
        out_shape=jax.ShapeDtypeStruct((M, N), out_dtype),
        scratch_shapes=[pltpu.VMEM((tm, D), BF16)],
        compiler_params=_params(("parallel", "arbitrary")),
        name="mm_mod",
    )(x, sc, sh, w)


def _layer_norm_rows(v, lg, lb):
    mu = jnp.mean(v, axis=-1, keepdims=True)
    d = v - mu
    var = jnp.mean(d * d, axis=-1, keepdims=True)
    return d * lax.rsqrt(var + 1e-5) * lg + lb


def _mm_ln_kernel(a0_ref, a1_ref, w_ref, x_ref, g_ref, lg_ref, lb_ref, o_ref, acc):
    k = pl.program_id(1)

    @pl.when(k == 0)
    def _():
        acc[...] = jnp.dot(a0_ref[...], w_ref[...], preferred_element_type=F32)

    @pl.when(k == 1)
    def _():
        y = acc[...] + jnp.dot(a1_ref[...], w_ref[...], preferred_element_type=F32)
        v = DEEPNORM_ALPHA * x_ref[...] + g_ref[0] * y
        o_ref[...] = _layer_norm_rows(v, lg_ref[...], lb_ref[...])


def mm_ln(a0, a0_blk, a1, a1_blk, w, x, g, lg, lb, T, tm):
    M, D = x.shape
    tm = min(tm, T)
    per_b = T // tm
    return pl.pallas_call(
        _mm_ln_kernel,
        grid=(M // tm, 2),
        in_specs=[pl.BlockSpec((tm, D), lambda i, k: (i, a0_blk)),
                  pl.BlockSpec((tm, D), lambda i, k: (i, a1_blk)),
                  pl.BlockSpec((D, D), lambda i, k: (k, 0)),
                  pl.BlockSpec((tm, D), lambda i, k: (i, 0)),
                  pl.BlockSpec((1, 1, D), lambda i, k: (i // per_b, 0, 0)),
                  pl.BlockSpec((1, D), lambda i, k: (0, 0)),
                  pl.BlockSpec((1, D), lambda i, k: (0, 0))],
        out_specs=pl.BlockSpec((tm, D), lambda i, k: (i, 0)),
        out_shape=jax.ShapeDtypeStruct((M, D), F32),
        scratch_shapes=[pltpu.VMEM((tm, D), F32)],
        compiler_params=_params(("parallel", "arbitrary")),
        name="mm_ln",
    )(a0, a1, w, x, g, lg, lb)


def _rope_kernel(pos_ref, inv_ref, cs_ref, sn_ref):
    ang = pos_ref[...].astype(F32) * inv_ref[...]
    cs_ref[...] = jnp.cos(ang)
    sn_ref[...] = jnp.sin(ang)


def rope_tables(positions):
    B, T = positions.shape
    M = B * T
    inv = ROPE_THETA ** (-jnp.arange(0, ROPE_DIM, 2, dtype=F32) / ROPE_DIM)
    inv2 = jnp.concatenate([inv, inv]).reshape(1, ROPE_DIM)
    tm = min(1024, T)
    return pl.pallas_call(
        _rope_kernel,
        grid=(M // tm,),
        in_specs=[pl.BlockSpec((tm, 1), lambda i: (i, 0)),
                  pl.BlockSpec((1, ROPE_DIM), lambda i: (0, 0))],
        out_specs=[pl.BlockSpec((tm, ROPE_DIM), lambda i: (i, 0)),
                   pl.BlockSpec((tm, ROPE_DIM), lambda i: (i, 0))],
        out_shape=[jax.ShapeDtypeStruct((M, ROPE_DIM), F32)] * 2,
        compiler_params=_params(("parallel",)),
        name="rope_tables",
    )(positions.reshape(M, 1), inv2)


def _rms_rows(x, g, eps=1e-6):
    return x * lax.rsqrt(jnp.mean(x * x, axis=-1, keepdims=True) + eps) * g


def _mla_proj_kernel(cq_ref, ckv_ref, kr_ref, cs_ref, sn_ref, qn_ref, kvn_ref, wq_ref, wk_ref, wvt_ref,
                     q_ref, k_ref, vt_ref, cq_scr, ckv_scr, kr_scr):
    @pl.when(pl.program_id(1) == 0)
    def _():
        cq_scr[...] = _rms_rows(cq_ref[...], qn_ref[...]).astype(BF16)
        ckv_scr[...] = _rms_rows(ckv_ref[...], kvn_ref[...]).astype(BF16)
        kr = kr_ref[...]
        kr_scr[...] = kr[:, :ROPE_DIM] * cs_ref[...] + kr[:, ROPE_DIM:] * sn_ref[...]

    scale = (NOPE_DIM + ROPE_DIM) ** -0.5 * math.log2(math.e)
    qf = jnp.dot(cq_scr[...], wq_ref[0], preferred_element_type=F32)
    q_rope = qf[:, NOPE_DIM:NOPE_DIM + ROPE_DIM] * cs_ref[...] + qf[:, NOPE_DIM + ROPE_DIM:] * sn_ref[...]
    q_ref[0, 0, :, :NOPE_DIM] = (qf[:, :NOPE_DIM] * scale).astype(BF16)
    q_ref[0, 0, :, NOPE_DIM:] = (q_rope * scale).astype(BF16)
    ckv = ckv_scr[...]
    k_ref[0, 0, :, :NOPE_DIM] = jnp.dot(ckv, wk_ref[0], preferred_element_type=F32).astype(BF16)
    k_ref[0, 0, :, NOPE_DIM:] = kr_scr[...].astype(BF16)
    vt_ref[0, 0] = lax.dot_general(wvt_ref[0], ckv, (((1,), (1,)), ((), ())),
                                   preferred_element_type=F32).astype(BF16)


def mla_proj(p, cs, sn, qn, kvn, wq, wk, wvt, B, T):
    M = B * T
    H = MLA_HEADS
    tm = min(1024, T)
    per_b = T // tm
    qk_dim = NOPE_DIM + ROPE_DIM
    out_map = lambda i, h: (i // per_b, h, i % per_b, 0)
    return pl.pallas_call(
        _mla_proj_kernel,
        grid=(M // tm, H),
        in_specs=[pl.BlockSpec((tm, Q_LORA), lambda i, h: (i, 6144 // Q_LORA)),
                  pl.BlockSpec((tm, KV_LORA), lambda i, h: (i, 7680 // KV_LORA)),
                  pl.BlockSpec((tm, 2 * ROPE_DIM), lambda i, h: (i, 8192 // (2 * ROPE_DIM))),
                  pl.BlockSpec((tm, ROPE_DIM), lambda i, h: (i, 0)),
                  pl.BlockSpec((tm, ROPE_DIM), lambda i, h: (i, 0)),
                  pl.BlockSpec((1, Q_LORA), lambda i, h: (0, 0)),
                  pl.BlockSpec((1, KV_LORA), lambda i, h: (0, 0)),
                  pl.BlockSpec((1, Q_LORA, 256), lambda i, h: (h, 0, 0)),
                  pl.BlockSpec((1, KV_LORA, NOPE_DIM), lambda i, h: (h, 0, 0)),
                  pl.BlockSpec((1, V_DIM, KV_LORA), lambda i, h: (h, 0, 0))],
        out_specs=[pl.BlockSpec((1, 1, tm, qk_dim), out_map),
                   pl.BlockSpec((1, 1, tm, qk_dim), out_map),
                   pl.BlockSpec((1, 1, V_DIM, tm), lambda i, h: (i // per_b, h, 0, i % per_b))],
        out_shape=[jax.ShapeDtypeStruct((B, H, T, qk_dim), BF16),
                   jax.ShapeDtypeStruct((B, H, T, qk_dim), BF16),
                   jax.ShapeDtypeStruct((B, H, V_DIM, T), BF16)],
        scratch_shapes=[pltpu.VMEM((tm, Q_LORA), BF16), pltpu.VMEM((tm, KV_LORA), BF16),
                        pltpu.VMEM((tm, ROPE_DIM), F32)],
        compiler_params=_params(("parallel", "arbitrary")),
        name="mla_proj",
    )(p, p, p, cs, sn, qn, kvn, wq, wk, wvt)


def _attn_kernel(q_ref, k_ref, vt_ref, o_ref, *, kc):
    q = q_ref[0, 0]
    T = k_ref.shape[2]
    nk = T // kc
    tq = q.shape[0]
    hq = min(MXU_DEPTH, tq)
    nh = tq // hq

    def scores(j):
        qj = q[j * hq:(j + 1) * hq]
        ss = [lax.dot_general(k_ref[0, 0, c * kc:(c + 1) * kc, :], qj, (((1,), (1,)), ((), ())),
                              preferred_element_type=F32) for c in range(nk)]
        m = jnp.max(ss[0], axis=0, keepdims=True)
        for c in range(1, nk):
            m = jnp.maximum(m, jnp.max(ss[c], axis=0, keepdims=True))
        return ss, m

    def finish(j, ss, m):
        acc = l = None
        for c in range(nk):
            p = jnp.exp2(ss[c] - m)
            lc = jnp.sum(p, axis=0, keepdims=True)
            ac = jnp.dot(vt_ref[0, 0, :, c * kc:(c + 1) * kc], p.astype(BF16), preferred_element_type=F32)
            l = lc if l is None else l + lc
            acc = ac if acc is None else acc + ac
        o_ref[0, j * hq:(j + 1) * hq, :] = (acc / l).T.astype(o_ref.dtype)

    prev = None
    for j in range(nh):
        cur = scores(j)
        if prev is not None:
            finish(j - 1, *prev)
        prev = cur
    finish(nh - 1, *prev)


def mla_attention(q, k, vt):
    B, H, T, dqk = q.shape
    dv = vt.shape[2]
    tq = min(2048, T)
    kc = min(1024, T)
    return pl.pallas_call(
        functools.partial(_attn_kernel, kc=kc),
        grid=(B, H, T // tq),
        in_specs=[pl.BlockSpec((1, 1, tq, dqk), lambda b, h, i: (b, h, i, 0)),
                  pl.BlockSpec((1, 1, T, dqk), lambda b, h, i: (b, h, 0, 0)),
                  pl.BlockSpec((1, 1, dv, T), lambda b, h, i: (b, h, 0, 0))],
        out_specs=pl.BlockSpec((1, tq, dv), lambda b, h, i: (b, i, h)),
        out_shape=jax.ShapeDtypeStruct((B, T, H * dv), BF16),
        compiler_params=_params(("parallel", "parallel", "parallel")),
        name="mla_attn",
    )(q, k, vt)


def _shift_rows(cur, prev8, next8, d):
    tm = cur.shape[0]
    if d == 0:
        return cur
    main = pltpu.roll(cur, (-d) % tm, axis=0)
    first = pltpu.roll(jnp.concatenate([prev8, cur[:16]], axis=0), (-d) % 24, axis=0)[8:16]
    last = pltpu.roll(jnp.concatenate([cur[tm - 16:], next8], axis=0), (-d) % 24, axis=0)[8:16]
    return jnp.concatenate([first, main[8:tm - 8], last], axis=0)


def _halo_specs(tm, cw, per_b, nrow8, col_of):
    r8 = tm // 8
    cur = pl.BlockSpec((tm, cw), lambda i, j: (i, col_of(j)))
    prv = pl.BlockSpec((8, cw), lambda i, j: (jnp.maximum(i * r8 - 1, 0), col_of(j)))
    nxt = pl.BlockSpec((8, cw), lambda i, j: (jnp.minimum((i + 1) * r8, nrow8 - 1), col_of(j)))
    return cur, prv, nxt


def _halo_load(cur_ref, prev_ref, next_ref, per_b):
    ti = pl.program_id(0) % per_b
    cur = cur_ref[...].astype(F32)
    prev8 = prev_ref[...].astype(F32) * jnp.where(ti == 0, 0.0, 1.0)
    next8 = next_ref[...].astype(F32) * jnp.where(ti == per_b - 1, 0.0, 1.0)
    return cur, prev8, next8


def _pair_masks(rev):
    i = lax.broadcasted_iota(I32, (PAIR, PAIR), 0)
    j = lax.broadcasted_iota(I32, (PAIR, PAIR), 1)
    same = lax.shift_right_logical(i, LOG2_CHUNK) == lax.shift_right_logical(j, LOG2_CHUNK)
    if rev:
        strict = jnp.logical_and(same, i < j)
        incl = jnp.logical_and(same, i <= j)
    else:
        strict = jnp.logical_and(same, i > j)
        incl = jnp.logical_and(same, i >= j)
    eye_b = i == j
    same_blk = lax.shift_right_logical(i, LOG2_INV_BLOCK) == lax.shift_right_logical(j, LOG2_INV_BLOCK)
    return strict, incl, eye_b, same_blk


def _unit_tri_inverse(nmat, eye, same_blk):
    n = nmat.shape[-1]
    nd = jnp.where(same_blk, nmat, 0.0)
    noff = nmat - nd
    x = eye + nd
    p = _bmm(nd, nd)
    yield
    for s in range(1, LOG2_INV_BLOCK):
        if s < LOG2_INV_BLOCK - 1:
            xp = _bmm(jnp.concatenate([x, p], axis=1), p)
            x = x + xp[:, :n]
            p = xp[:, n:]
        else:
            x = x + _bmm(x, p)
        yield
    y = _bmm(x, noff)
    yield
    acc = x
    for _ in range(CHUNK // INV_BLOCK - 1):
        acc = x + _bmm(y, acc)
        yield
    return acc


def _aligned(start, m):
    return start if isinstance(start, int) else pl.multiple_of(start, m)


def _interleave(n_steps, step_fn, gens, stages_per_step):
    for i in range(n_steps):
        step_fn(i)
        for _ in range(stages_per_step):
            for g in gens:
                next(g, None)
    for g in gens:
        for _ in g:
            pass


def _rows_to_cols(rows, eye_b):
    return jnp.sum(jnp.where(eye_b, rows, 0.0), axis=2, keepdims=True)


def _stack_heads(x, head0):
    return jnp.concatenate([jnp.where(head0, x, 0.0), jnp.where(head0, 0.0, x)], axis=1)


def _shift_mix_kernel(cur_ref, prev_ref, next_ref, mu_ref, o_ref, *, per_b, lora):
    cur, prev8, next8 = _halo_load(cur_ref, prev_ref, next_ref, per_b)
    prv = _shift_rows(cur, prev8, next8, -1)
    nxt = _shift_rows(cur, prev8, next8, 1)
    z = cur + (0.5 * (prv + nxt) - cur) * mu_ref[...]
    if lora:
        col = lax.broadcasted_iota(I32, z.shape, 1)
        z = jnp.where(col < 2 * LORA_PAD, jnp.tanh(z), jnp.where(col >= 4 * LORA_PAD, _sigmoid(z), z))
    o_ref[...] = z.astype(o_ref.dtype)


def shift_mix(p, mu, T, col0, width, cw, lora):
    M = p.shape[0]
    tm = min(512, T)
    per_b = T // tm
    c0 = col0 // cw
    cur, prv, nxt = _halo_specs(tm, cw, per_b, M // 8, lambda j: c0 + j)
    return pl.pallas_call(
        functools.partial(_shift_mix_kernel, per_b=per_b, lora=lora),
        grid=(M // tm, width // cw),
        in_specs=[cur, prv, nxt, pl.BlockSpec((1, cw), lambda i, j: (0, j))],
        out_specs=pl.BlockSpec((tm, cw), lambda i, j: (i, j)),
        out_shape=jax.ShapeDtypeStruct((M, width), BF16),
        compiler_params=_params(("parallel", "parallel")),
        name="rwkv_shift_mix",
    )(p, p, p, mu)


def _rwkv_prepare(g, G, d, rev, r_ref, k_ref, v_ref, la_ref, w2_ref, a2_ref, w0_ref, a0_ref, kk_ref, ka_ref,
                  p16_ref, p32_ref, bkt_ref, bon_ref):
    C = CHUNK
    rows = pl.ds(_aligned(g * (G * C), G * C), G * C)
    r = r_ref[0, rows, :].astype(F32)
    k = k_ref[0, rows, :].astype(F32)
    v = v_ref[0, rows, :].astype(F32)
    wd = la_ref[0, rows, d * LORA_PAD:(d + 1) * LORA_PAD]
    ad = la_ref[0, rows, (2 + d) * LORA_PAD:(3 + d) * LORA_PAD]
    lw = -RWKV_DECAY_SCALE * _sigmoid(w0_ref[d:d + 1, :] + jnp.dot(wd, w2_ref[d], preferred_element_type=F32))
    a_sig = _sigmoid(a0_ref[d:d + 1, :] + jnp.dot(ad, a2_ref[d], preferred_element_type=F32))

    lane = lax.broadcasted_iota(I32, (1, LANES), 1)
    head0 = lane < RWKV_N
    kx = k * kk_ref[...]
    sq = kx * kx
    ss = jnp.where(head0, jnp.sum(jnp.where(head0, sq, 0.0), axis=1, keepdims=True),
                   jnp.sum(jnp.where(head0, 0.0, sq), axis=1, keepdims=True))
    kk = kx * lax.rsqrt(ss + 1e-6)
    k_dir = k * (1.0 + (a_sig - 1.0) * ka_ref[...])
    b = kk * a_sig
    a = -kk
    bon_ref[d, rows, :] = r * k_dir
    yield

    g3 = lambda t: t.reshape(G, C, LANES)
    lw3, r3, a3, b3, kd3, v3 = g3(lw), g3(r), g3(a), g3(b), g3(k_dir), g3(v)
    ci = lax.broadcasted_iota(I32, (C, C), 0)
    cj = lax.broadcasted_iota(I32, (C, C), 1)
    tri = jnp.where((ci <= cj) if rev else (ci >= cj), 1.0, 0.0).astype(BF16)
    tri = jnp.broadcast_to(tri[None], (G, C, C))
    hi, lo = _split2(lw3)
    cum = (jnp.einsum('gij,gjk->gik', tri, hi, preferred_element_type=F32)
           + jnp.einsum('gij,gjk->gik', tri, lo, preferred_element_type=F32))
    yield
    end = 0 if rev else C - 1
    cum_end = cum[:, end:end + 1, :]
    e_neg = jnp.exp(-cum)
    e_tail = jnp.exp(cum_end - cum)
    h0 = head0.reshape(1, 1, LANES)
    a2s = _stack_heads(a3 * jnp.exp(cum - lw3), h0)
    r2s = _stack_heads(r3 * jnp.exp(cum), h0)
    k2s = _stack_heads(kd3 * e_neg, h0)
    b2s = _stack_heads(b3 * e_neg, h0)
    v2s = _stack_heads(v3, h0)
    kh2 = _stack_heads(kd3 * e_tail, h0)
    bh2 = _stack_heads(b3 * e_tail, h0)

    strict, incl, eye_b, same_blk = _pair_masks(rev)
    eye = jnp.where(eye_b, 1.0, 0.0)
    gram = _bmm_nt(jnp.concatenate([a2s, r2s], axis=1), jnp.concatenate([k2s, b2s], axis=1))
    a_ak = jnp.where(strict, gram[:, :PAIR, :PAIR], 0.0)
    a_ab = jnp.where(strict, gram[:, :PAIR, PAIR:], 0.0)
    a_rk = jnp.where(incl, gram[:, PAIR:, :PAIR], 0.0)
    a_rb = jnp.where(incl, gram[:, PAIR:, PAIR:], 0.0).astype(BF16)
    p16_ref[d, :, 2 * PAIR:4 * PAIR, :] = jnp.concatenate([v2s.astype(BF16), a_rb], axis=1)
    bkt_ref[d] = jnp.swapaxes(jnp.concatenate([bh2, kh2], axis=1), 1, 2).astype(BF16)
    p32_ref[d, :, 2 * PAIR:3 * PAIR, :] = jnp.broadcast_to(_rows_to_cols(jnp.exp(cum_end), eye_b), (G, PAIR, LANES))
    yield
    kv = _bmm(jnp.concatenate([a_ak, a_rk], axis=1), v2s)
    p32_ref[d, :, PAIR:2 * PAIR, :] = kv[:, PAIR:]
    yield
    tinv = yield from _unit_tri_inverse(a_ab, eye, same_blk)
    tt = _bmm(tinv, jnp.concatenate([a2s, kv[:, :PAIR]], axis=2))
    p16_ref[d, :, 0:2 * PAIR, :] = jnp.concatenate([tt[:, :, :LANES], r2s], axis=1).astype(BF16)
    p32_ref[d, :, 0:PAIR, :] = tt[:, :, LANES:]


RWKV_P16_ROWS = 4 * PAIR
RWKV_P32_ROWS = 3 * PAIR
RWKV_PREP_STAGES = 14


def _rwkv_step(H, p16_ref, p32_ref, bkt_ref, d, i):
    C = CHUNK
    lhs = p16_ref[d, i, 0:2 * PAIR, :]
    v2 = p16_ref[d, i, 2 * PAIR:3 * PAIR, :]
    a_rb = p16_ref[d, i, 3 * PAIR:4 * PAIR, :]
    sr = _dot(lhs, H)
    u2 = sr[:PAIR] + p32_ref[d, i, 0:PAIR, :]
    y2 = sr[PAIR:] + p32_ref[d, i, PAIR:2 * PAIR, :] + _dot(a_rb, u2)
    y = y2[:C] + y2[C:]
    upd = _dot(bkt_ref[d, i], jnp.concatenate([u2.astype(BF16), v2], axis=0))
    return H * p32_ref[d, i, 2 * PAIR:3 * PAIR, :] + upd, y


def _rwkv_scan_kernel(r_ref, k_ref, v_ref, la_ref, w2_ref, a2_ref, g2_ref, w0_ref, a0_ref,
                      kk_ref, ka_ref, rk_ref, gng_ref, gnb_ref, o_ref, yf_scr, yb_scr, bon_scr,
                      a16, a32, akt, b16, b32, bkt, *, nC, G):
    C = CHUNK
    nG = nC // G
    GC = G * C
    refs = (r_ref, k_ref, v_ref, la_ref, w2_ref, a2_ref, w0_ref, a0_ref, kk_ref, ka_ref)

    def prepare(g, bufs):
        return [_rwkv_prepare(g, G, 0, False, *refs, *bufs, bon_scr),
                _rwkv_prepare(nG - 1 - g, G, 1, True, *refs, *bufs, bon_scr)]

    def run(g, carry, cur, nxt_g, nxt):
        state = list(carry)
        gb = nG - 1 - g

        def step(i):
            ib = G - 1 - i
            state[0], yf = _rwkv_step(state[0], *cur, 0, i)
            state[1], yb = _rwkv_step(state[1], *cur, 1, ib)
            yf_scr[pl.ds(_aligned(g * GC + i * C, C), C), :] = yf
            yb_scr[pl.ds(_aligned(gb * GC + ib * C, C), C), :] = yb

        gens = [] if nxt_g is None else prepare(nxt_g, nxt)
        _interleave(G, step, gens, RWKV_PREP_STAGES // G + 1)
        return tuple(state)

    bufs_a, bufs_b = (a16, a32, akt), (b16, b32, bkt)
    _interleave(0, None, prepare(0, bufs_a), 0)

    def body(h, carry):
        g = 2 * h
        carry = run(g, carry, bufs_a, g + 1, bufs_b)
        return run(g + 1, carry, bufs_b, g + 2, bufs_a)

    z = jnp.zeros((LANES, LANES), F32)
    carry = lax.fori_loop(0, nG // 2 - 1, body, (z, z))
    carry = run(nG - 2, carry, bufs_a, nG - 1, bufs_b)
    run(nG - 1, carry, bufs_b, None, None)

    def fin(g, _):
        sl = pl.ds(_aligned(g * GC, GC), GC)
        y = yf_scr[sl, :] + yb_scr[sl, :]
        lane = lax.broadcasted_iota(I32, (1, LANES), 1)
        head0 = lane < RWKV_N

        def seg_sum(x):
            s0 = jnp.sum(jnp.where(head0, x, 0.0), axis=1, keepdims=True)
            s1 = jnp.sum(jnp.where(head0, 0.0, x), axis=1, keepdims=True)
            return jnp.where(head0, s0, s1)

        mu = seg_sum(y) * (1.0 / RWKV_N)
        dlt = y - mu
        var = seg_sum(dlt * dlt) * (1.0 / RWKV_N)
        yn = dlt * lax.rsqrt(var + RWKV_GN_EPS) * gng_ref[...] + gnb_ref[...]
        bsum = seg_sum((bon_scr[0, sl, :] + bon_scr[1, sl, :]) * rk_ref[...])
        v = v_ref[0, sl, :].astype(F32)
        gate = jnp.dot(la_ref[0, sl, 4 * LORA_PAD:], g2_ref[...], preferred_element_type=F32)
        o_ref[0, sl, :] = ((yn + bsum * v) * gate).astype(o_ref.dtype)
        return 0

    lax.fori_loop(0, nG, fin, 0)


def rwkv_scan(z_rkv, la, w2, a2, g2, w0, a0, k_k, k_a, r_k, gn_g, gn_b, B, T):
    nC = T // CHUNK
    G = min(GROUP, nC // 2)
    assert (nC // G) % 2 == 0
    npair = RWKV_WIDTH // LANES
    nl = la.shape[-1]
    vec = lambda: pl.BlockSpec((1, LANES), lambda b, p: (0, p))
    return pl.pallas_call(
        functools.partial(_rwkv_scan_kernel, nC=nC, G=G),
        grid=(B, npair),
        in_specs=[pl.BlockSpec((1, T, LANES), lambda b, p: (b, 0, p)),
                  pl.BlockSpec((1, T, LANES), lambda b, p: (b, 0, npair + p)),
                  pl.BlockSpec((1, T, LANES), lambda b, p: (b, 0, 2 * npair + p)),
                  pl.BlockSpec((1, T, nl), lambda b, p: (b, 0, 0)),
                  pl.BlockSpec((2, LORA_PAD, LANES), lambda b, p: (0, 0, p)),
                  pl.BlockSpec((2, LORA_PAD, LANES), lambda b, p: (0, 0, p)),
                  pl.BlockSpec((GATE_LORA, LANES), lambda b, p: (0, p)),
                  pl.BlockSpec((2, LANES), lambda b, p: (0, p)),
                  pl.BlockSpec((2, LANES), lambda b, p: (0, p)),
                  vec(), vec(), vec(), vec(), vec()],
        out_specs=pl.BlockSpec((1, T, LANES), lambda b, p: (b, 0, p)),
        out_shape=jax.ShapeDtypeStruct((B, T, RWKV_WIDTH), BF16),
        scratch_shapes=[pltpu.VMEM((T, LANES), F32), pltpu.VMEM((T, LANES), F32),
                        pltpu.VMEM((2, T, LANES), F32)]
        + 2 * [pltpu.VMEM((2, G, RWKV_P16_ROWS, LANES), BF16), pltpu.VMEM((2, G, RWKV_P32_ROWS, LANES), F32),
               pltpu.VMEM((2, G, LANES, 2 * PAIR), BF16)],
        compiler_params=_params(("parallel", "arbitrary")),
        name="rwkv_scan",
    )(z_rkv, z_rkv, z_rkv, la, w2, a2, g2, w0, a0, k_k, k_a, r_k, gn_g, gn_b)


def _gdn_conv_kernel(cur_ref, prev_ref, next_ref, w_ref, o_ref, *, per_b, q_blocks, k_blocks):
    cur, prev8, next8 = _halo_load(cur_ref, prev_ref, next_ref, per_b)
    half = CONV_WIDTH // 2
    acc = jnp.zeros_like(cur)
    for i in range(CONV_WIDTH):
        acc = acc + _shift_rows(cur, prev8, next8, i - half) * w_ref[i:i + 1, :]
    y = _silu(acc)
    j = pl.program_id(1)
    cw = y.shape[1]
    outs = []
    for s in range(cw // GDN_D):
        ys = y[:, s * GDN_D:(s + 1) * GDN_D]
        nrm = lax.rsqrt(jnp.sum(ys * ys, axis=1, keepdims=True) + 1e-6)
        outs.append(ys * nrm)
    yn = jnp.concatenate(outs, axis=1) if len(outs) > 1 else outs[0]
    res = jnp.where(j < q_blocks, yn * (GDN_D ** -0.5), jnp.where(j < k_blocks, yn, y))
    o_ref[...] = res.astype(o_ref.dtype)


def gdn_conv_op(p, conv_w, T):
    M = p.shape[0]
    tm = min(512, T)
    cw = 512
    per_b = T // tm
    cur, prv, nxt = _halo_specs(tm, cw, per_b, M // 8, lambda j: j)
    return pl.pallas_call(
        functools.partial(_gdn_conv_kernel, per_b=per_b, q_blocks=GDN_KEY_WIDTH // cw,
                          k_blocks=2 * GDN_KEY_WIDTH // cw),
        grid=(M // tm, GDN_QKV // cw),
        in_specs=[cur, prv, nxt, pl.BlockSpec((CONV_WIDTH, cw), lambda i, j: (0, j))],
        out_specs=pl.BlockSpec((tm, cw), lambda i, j: (i, j)),
        out_shape=jax.ShapeDtypeStruct((M, GDN_QKV), BF16),
        compiler_params=_params(("parallel", "parallel")),
        name="gdn_conv",
    )(p, p, p, conv_w)


def _gdn_gate_kernel(ba_ref, alog_ref, dtb_ref, o_ref):
    C = CHUNK
    x = ba_ref[...]
    tm = x.shape[0]
    nh = 2 * GDN_V_HEADS
    beta = _sigmoid(x)
    a = x + dtb_ref[...]
    sp = jnp.maximum(a, 0.0) + jnp.log1p(jnp.exp(-jnp.abs(a)))
    g = -jnp.exp(alog_ref[...]) * sp
    i = lax.broadcasted_iota(I32, (C, C), 0)
    j = lax.broadcasted_iota(I32, (C, C), 1)
    lower = jnp.where(i >= j, 1.0, 0.0).astype(BF16)
    upper = jnp.where(i <= j, 1.0, 0.0).astype(BF16)
    col = lax.broadcasted_iota(I32, (C, 2 * nh), 1)
    for c in range(tm // C):
        gc = g[c * C:(c + 1) * C]
        hi, mid, lo = _split3(gc)
        fw = (jnp.dot(lower, hi, preferred_element_type=F32) + jnp.dot(lower, mid, preferred_element_type=F32)
              + jnp.dot(lower, lo, preferred_element_type=F32))
        bw = (jnp.dot(upper, hi, preferred_element_type=F32) + jnp.dot(upper, mid, preferred_element_type=F32)
              + jnp.dot(upper, lo, preferred_element_type=F32))
        o_ref[c * C:(c + 1) * C, :] = jnp.where(col < nh, beta[c * C:(c + 1) * C],
                                                jnp.where(col < nh + GDN_V_HEADS, fw, bw))


def gdn_gates(ba, a_log, dt_bias, T):
    M = ba.shape[0]
    tm = min(512, T)
    nh = 2 * GDN_V_HEADS
    return pl.pallas_call(
        _gdn_gate_kernel,
        grid=(M // tm,),
        in_specs=[pl.BlockSpec((tm, 2 * nh), lambda i: (i, 0)),
                  pl.BlockSpec((1, 2 * nh), lambda i: (0, 0)),
                  pl.BlockSpec((1, 2 * nh), lambda i: (0, 0))],
        out_specs=pl.BlockSpec((tm, 2 * nh), lambda i: (i, 0)),
        out_shape=jax.ShapeDtypeStruct((M, 2 * nh), F32),
        compiler_params=_params(("parallel",)),
        name="gdn_gates",
    )(ba, a_log, dt_bias)


def _gdn_prepare(g, G, d, rev, kh, q_ref, k_ref, v_ref, bg_ref, p16_ref, u_ref, dec_ref):
    C = CHUNK
    rows = pl.ds(_aligned(g * (G * C), G * C), G * C)
    q = q_ref[0, rows, :].astype(F32).reshape(G, C, GDN_D)
    k = k_ref[0, rows, :].astype(F32).reshape(G, C, GDN_D)
    v = v_ref[0, rows, :].astype(F32).reshape(G, C, 2 * GDN_D)
    v2 = jnp.concatenate([v[:, :, :GDN_D], v[:, :, GDN_D:]], axis=1)
    k2 = jnp.concatenate([k, k], axis=1)
    q2 = jnp.concatenate([q, q], axis=1)
    chunks = pl.ds(g * G, G)
    beta_row = bg_ref[0, chunks, pl.ds(d * GDN_K_HEADS + kh, 1), :]
    gc_row = bg_ref[0, chunks, pl.ds((2 + d) * GDN_K_HEADS + kh, 1), :]
    strict, incl, eye_b, same_blk = _pair_masks(rev)
    eye = jnp.where(eye_b, 1.0, 0.0)
    beta_col = _rows_to_cols(beta_row, eye_b)
    gc_col = _rows_to_cols(gc_row, eye_b)
    decay = jnp.where(incl, jnp.exp(jnp.where(incl, gc_col - gc_row, 0.0)), 0.0)
    gram = _bmm_nt(jnp.concatenate([k2, q2], axis=1), k2)
    lower = jnp.where(strict, gram[:, :PAIR] * beta_col * decay, 0.0)
    eg = jnp.exp(gc_col)
    end = 0 if rev else C - 1
    gl0 = gc_row[:, :, end:end + 1]
    gl1 = gc_row[:, :, C + end:C + end + 1]
    rowi = lax.broadcasted_iota(I32, (1, PAIR, 1), 1)
    gl_col = jnp.where(rowi < C, gl0, gl1)
    lane2 = lax.broadcasted_iota(I32, (1, 1, 2 * GDN_D), 2)
    decay_s = jnp.where(lane2 < GDN_D, jnp.exp(gl0), jnp.exp(gl1))
    p16_ref[d, :, PAIR:2 * PAIR, :] = (q2 * eg).astype(BF16)
    p16_ref[d, :, 2 * PAIR:3 * PAIR, :] = (gram[:, PAIR:] * decay).astype(BF16)
    p16_ref[d, :, 3 * PAIR:4 * PAIR, :] = jnp.swapaxes(k2 * jnp.exp(gl_col - gc_col), 1, 2).astype(BF16)
    dec_ref[d] = jnp.broadcast_to(decay_s, (G, 8, 2 * GDN_D))
    yield
    t_inv = yield from _unit_tri_inverse(-lower, eye, same_blk)
    kb = k2 * beta_col
    uw = _bmm(t_inv, jnp.concatenate([v2 * beta_col, kb * eg], axis=2))
    u_ref[d] = uw[:, :, :GDN_D]
    p16_ref[d, :, 0:PAIR, :] = uw[:, :, GDN_D:].astype(BF16)


GDN_P16_ROWS = 4 * PAIR
GDN_PREP_STAGES = 10


def _gdn_step(S, p16_ref, u_ref, dec_ref, d, i):
    C = CHUNK
    ws = _dot(p16_ref[d, i, 0:2 * PAIR, :], S)
    head0_rows = lax.broadcasted_iota(I32, (PAIR, 1), 0) < C
    w_s = jnp.where(head0_rows, ws[:PAIR, :GDN_D], ws[:PAIR, GDN_D:])
    q_s = jnp.where(head0_rows, ws[PAIR:, :GDN_D], ws[PAIR:, GDN_D:])
    v_new = u_ref[d, i] - w_s
    o2 = q_s + _dot(p16_ref[d, i, 2 * PAIR:3 * PAIR, :], v_new)
    vcat = jnp.concatenate([jnp.where(head0_rows, v_new, 0.0), jnp.where(head0_rows, 0.0, v_new)], axis=1)
    S_new = S * dec_ref[d, i, 0:1, :] + _dot(p16_ref[d, i, 3 * PAIR:4 * PAIR, :], vcat)
    return S_new, jnp.concatenate([o2[:C], o2[C:]], axis=1)


def _gdn_scan_kernel(q_ref, k_ref, v_ref, z_ref, bg_ref, ng_ref, o_ref, of_scr, ob_scr,
                     a16, au, ad, b16, bu, bd, *, nC, G):
    C = CHUNK
    nG = nC // G
    GC = G * C
    kh = pl.program_id(1)

    def prepare(g, bufs):
        return [_gdn_prepare(g, G, 0, False, kh, q_ref, k_ref, v_ref, bg_ref, *bufs),
                _gdn_prepare(nG - 1 - g, G, 1, True, kh, q_ref, k_ref, v_ref, bg_ref, *bufs)]

    def run(g, carry, cur, nxt_g, nxt):
        state = list(carry)
        gb = nG - 1 - g

        def step(i):
            ib = G - 1 - i
            state[0], of = _gdn_step(state[0], *cur, 0, i)
            state[1], ob = _gdn_step(state[1], *cur, 1, ib)
            of_scr[pl.ds(_aligned(g * GC + i * C, C), C), :] = of
            ob_scr[pl.ds(_aligned(gb * GC + ib * C, C), C), :] = ob

        gens = [] if nxt_g is None else prepare(nxt_g, nxt)
        _interleave(G, step, gens, GDN_PREP_STAGES // G + 1)
        return tuple(state)

    bufs_a, bufs_b = (a16, au, ad), (b16, bu, bd)
    _interleave(0, None, prepare(0, bufs_a), 0)

    def body(h, carry):
        g = 2 * h
        carry = run(g, carry, bufs_a, g + 1, bufs_b)
        return run(g + 1, carry, bufs_b, g + 2, bufs_a)

    z0 = jnp.zeros((GDN_D, 2 * GDN_D), F32)
    carry = lax.fori_loop(0, nG // 2 - 1, body, (z0, z0))
    carry = run(nG - 2, carry, bufs_a, nG - 1, bufs_b)
    run(nG - 1, carry, bufs_b, None, None)

    def fin(g, _):
        sl = pl.ds(_aligned(g * GC, GC), GC)
        o = of_scr[sl, :] + ob_scr[sl, :]
        z = z_ref[0, sl, :].astype(F32)
        for s in range(2):
            os_ = o[:, s * GDN_D:(s + 1) * GDN_D]
            on = os_ * lax.rsqrt(jnp.mean(os_ * os_, axis=-1, keepdims=True) + 1e-6) * ng_ref[...]
            o_ref[0, sl, s * GDN_D:(s + 1) * GDN_D] = (on * _silu(z[:, s * GDN_D:(s + 1) * GDN_D])).astype(o_ref.dtype)
        return 0

    lax.fori_loop(0, nG, fin, 0)


def gdn_scan(qkv, z_src, z_col0, bg, norm_g, B, T):
    nC = T // CHUNK
    G = min(GROUP, nC // 2)
    assert (nC // G) % 2 == 0
    kb = GDN_KEY_WIDTH // GDN_D
    vb = 2 * GDN_KEY_WIDTH // (2 * GDN_D)
    return pl.pallas_call(
        functools.partial(_gdn_scan_kernel, nC=nC, G=G),
        grid=(B, GDN_K_HEADS),
        in_specs=[pl.BlockSpec((1, T, GDN_D), lambda b, h: (b, 0, h)),
                  pl.BlockSpec((1, T, GDN_D), lambda b, h: (b, 0, kb + h)),
                  pl.BlockSpec((1, T, 2 * GDN_D), lambda b, h: (b, 0, vb + h)),
                  pl.BlockSpec((1, T, 2 * GDN_D), lambda b, h: (b, 0, z_col0 + h)),
                  pl.BlockSpec((1, nC, 4 * GDN_K_HEADS, PAIR), lambda b, h: (b, 0, 0, 0)),
                  pl.BlockSpec((1, GDN_D), lambda b, h: (0, 0))],
        out_specs=pl.BlockSpec((1, T, 2 * GDN_D), lambda b, h: (b, 0, h)),
        out_shape=jax.ShapeDtypeStruct((B, T, GDN_VAL_WIDTH), BF16),
        scratch_shapes=[pltpu.VMEM((T, 2 * GDN_D), F32), pltpu.VMEM((T, 2 * GDN_D), F32)]
        + 2 * [pltpu.VMEM((2, G, GDN_P16_ROWS, GDN_D), BF16), pltpu.VMEM((2, G, PAIR, GDN_D), F32),
               pltpu.VMEM((2, G, 8, 2 * GDN_D), F32)],
        compiler_params=_params(("parallel", "arbitrary")),
        name="gdn_scan",
    )(qkv, qkv, qkv, z_src, bg, norm_g)


def _router_kernel(x_ref, sc_ref, sh_ref, wr_ref, h_ref, aff_ref):
    h = x_ref[...] * (1.0 + sc_ref[0]) + sh_ref[0]
    h_ref[...] = h.astype(BF16)
    hh, hm, hl = _split3(h)
    wh, wm, wl = _split3(wr_ref[...])
    nt = lambda a, b: lax.dot_general(a, b, (((1,), (1,)), ((), ())), preferred_element_type=F32)
    logits = (nt(wh, hh) + (nt(wh, hm) + nt(wm, hh)) + (nt(wh, hl) + nt(wl, hh) + nt(wm, hm)))
    m = jnp.max(logits, axis=0, keepdims=True)
    e = jnp.exp(logits - m)
    aff_ref[0] = e / jnp.sum(e, axis=0, keepdims=True)


def moe_router_op(x, sc, sh, wr_t, B, T):
    M, D = x.shape
    E = wr_t.shape[0]
    tm = min(512, T)
    per_b = T // tm
    return pl.pallas_call(
        _router_kernel,
        grid=(M // tm,),
        in_specs=[pl.BlockSpec((tm, D), lambda i: (i, 0)),
                  pl.BlockSpec((1, 1, D), lambda i: (i // per_b, 0, 0)),
                  pl.BlockSpec((1, 1, D), lambda i: (i // per_b, 0, 0)),
                  pl.BlockSpec((E, D), lambda i: (0, 0))],
        out_specs=[pl.BlockSpec((tm, D), lambda i: (i, 0)),
                   pl.BlockSpec((1, E, tm), lambda i: (i // per_b, 0, i % per_b))],
        out_shape=[jax.ShapeDtypeStruct((M, D), BF16), jax.ShapeDtypeStruct((B, E, T), F32)],
        compiler_params=_params(("parallel",)),
        name="moe_router",
    )(x, sc, sh, wr_t)


def _lane_cumsum(m01, blk):
    E, T = m01.shape
    i = lax.broadcasted_iota(I32, (blk, blk), 0)
    j = lax.broadcasted_iota(I32, (blk, blk), 1)
    upper = jnp.where(i <= j, 1.0, 0.0).astype(BF16)
    carry = jnp.zeros((E, 1), F32)
    parts = []
    for c in range(T // blk):
        seg = m01[:, c * blk:(c + 1) * blk]
        cs = jnp.dot(seg.astype(BF16), upper, preferred_element_type=F32) + carry
        parts.append(cs)
        carry = cs[:, blk - 1:blk]
    return jnp.concatenate(parts, axis=1) if len(parts) > 1 else parts[0]


def _topk_kernel(aff_ref, pos_ref, los_ref, *, cap, ts):
    a = aff_ref[0]
    E, T = a.shape
    bits = pltpu.bitcast(a, I32)

    def body(i, t):
        cand = t | lax.shift_left(jnp.int32(1), 30 - i)
        cnt = jnp.sum((bits >= cand).astype(I32), axis=1, keepdims=True)
        return jnp.where(cnt >= cap, cand, t)

    thr = lax.fori_loop(0, 31, body, jnp.zeros((E, 1), I32))
    gt = bits > thr
    eq = bits == thr
    need = cap - jnp.sum(gt.astype(I32), axis=1, keepdims=True)
    blk = min(512, T)
    eq01 = jnp.where(eq, 1.0, 0.0)
    eq_rank = _lane_cumsum(eq01, blk) - eq01
    sel = jnp.logical_or(gt, jnp.logical_and(eq, eq_rank < need.astype(F32)))
    sel01 = jnp.where(sel, 1.0, 0.0)
    slot = _lane_cumsum(sel01, blk) - sel01
    pos_ref[0] = jnp.where(sel, slot.astype(I32), -1)
    ti = lax.broadcasted_iota(I32, (T, LANES), 0)
    wi = lax.broadcasted_iota(I32, (T, LANES), 1)
    before = jnp.where(ti < wi * ts, 1.0, 0.0).astype(BF16)
    los_ref[0] = jnp.dot(sel01.astype(BF16), before, preferred_element_type=F32).astype(I32)


def moe_topk(aff_t, cap, ts):
    B, E, T = aff_t.shape
    return pl.pallas_call(
        functools.partial(_topk_kernel, cap=cap, ts=ts),
        grid=(B,),
        in_specs=[pl.BlockSpec((1, E, T), lambda b: (b, 0, 0))],
        out_specs=[pl.BlockSpec((1, E, T), lambda b: (b, 0, 0)),
                   pl.BlockSpec((1, E, LANES), lambda b: (b, 0, 0))],
        out_shape=[jax.ShapeDtypeStruct((B, E, T), I32), jax.ShapeDtypeStruct((B, E, LANES), I32)],
        compiler_params=_params(("parallel",)),
        name="moe_topk",
    )(aff_t)


def _slot_windows(cap, win):
    win = min(win, cap)
    return win, cap // win


def _gather_kernel(los_ref, pos_ref, h_ref, o_ref, acc, *, cap, nt, ts):
    b, e = pl.program_id(0), pl.program_id(1)
    base = (b * pl.num_programs(1) + e) * (nt + 1)
    win, nwin = _slot_windows(cap, LANES)
    acc[...] = jnp.zeros(acc.shape, F32)
    for c in range(nt):
        lo = los_ref[base + c]
        hi = los_ref[base + c + 1]
        for w in range(nwin):
            @pl.when(jnp.logical_and(lo < (w + 1) * win, hi > w * win))
            def _():
                slot = lax.broadcasted_iota(I32, (win, ts), 0) + w * win
                onehot = jnp.where(slot == pos_ref[0, 0, :, c * ts:(c + 1) * ts], 1.0, 0.0).astype(BF16)
                acc[w * win:(w + 1) * win, :] += jnp.dot(onehot, h_ref[c * ts:(c + 1) * ts, :],
                                                         preferred_element_type=F32)
    o_ref[0, 0] = acc[...].astype(o_ref.dtype)


def moe_gather(los, pos4, h, B, T, cap, ts):
    E = pos4.shape[1]
    D = h.shape[1]
    nt = T // ts
    return pl.pallas_call(
        functools.partial(_gather_kernel, cap=cap, nt=nt, ts=ts),
        grid_spec=pltpu.PrefetchScalarGridSpec(
            num_scalar_prefetch=1,
            grid=(B, E),
            in_specs=[pl.BlockSpec((1, 1, 1, T), lambda b, e, s: (b, e, 0, 0)),
                      pl.BlockSpec((T, D), lambda b, e, s: (b, 0))],
            out_specs=pl.BlockSpec((1, 1, cap, D), lambda b, e, s: (b, e, 0, 0)),
            scratch_shapes=[pltpu.VMEM((cap, D), F32)]),
        out_shape=jax.ShapeDtypeStruct((B, E, cap, D), BF16),
        compiler_params=_params(("parallel", "arbitrary")),
        name="moe_gather",
    )(los, pos4, h)


def _ffn_kernel(x_ref, wg_ref, wu_ref, wd_ref, o_ref):
    x = x_ref[0, 0]
    g = jnp.dot(x, wg_ref[0], preferred_element_type=F32)
    u = jnp.dot(x, wu_ref[0], preferred_element_type=F32)
    hid = (_silu(g) * u).astype(BF16)
    o_ref[0, 0] = jnp.dot(hid, wd_ref[0], preferred_element_type=F32).astype(o_ref.dtype)


def moe_ffn(xs, wg, wu, wd, layer):
    B, E, cap, D = xs.shape
    Fh = wg.shape[-1]
    e0 = layer * E
    return pl.pallas_call(
        _ffn_kernel,
        grid=(E, B),
        in_specs=[pl.BlockSpec((1, 1, cap, D), lambda e, b: (b, e, 0, 0)),
                  pl.BlockSpec((1, D, Fh), lambda e, b: (e0 + e, 0, 0)),
                  pl.BlockSpec((1, D, Fh), lambda e, b: (e0 + e, 0, 0)),
                  pl.BlockSpec((1, Fh, D), lambda e, b: (e0 + e, 0, 0))],
        out_specs=pl.BlockSpec((1, 1, cap, D), lambda e, b: (b, e, 0, 0)),
        out_shape=jax.ShapeDtypeStruct((B, E, cap, D), BF16),
        compiler_params=_params(("parallel", "arbitrary")),
        name="moe_ffn",
    )(xs, wg, wu, wd)


def _combine_kernel(los_ref, pos_ref, gate_ref, y_ref, x_ref, g_ref, lg_ref, lb_ref, o_ref, acc, *, cap, nt):
    b, i, e = pl.program_id(0), pl.program_id(1), pl.program_id(2)
    base = (b * pl.num_programs(2) + e) * (nt + 1) + i
    lo = los_ref[base]
    hi = los_ref[base + 1]
    pos = pos_ref[0].astype(F32)
    gate = gate_ref[0]
    tt, E = pos.shape
    lane = lax.broadcasted_iota(I32, (tt, E), 1)
    pcol = jnp.sum(jnp.where(lane == e, pos, 0.0), axis=1, keepdims=True)
    gcol = jnp.sum(jnp.where(lane == e, gate, 0.0), axis=1, keepdims=True)
    win, nwin = _slot_windows(cap, MXU_DEPTH)

    @pl.when(e == 0)
    def _():
        acc[...] = jnp.zeros(acc.shape, F32)

    for w in range(nwin):
        @pl.when(jnp.logical_and(lo < (w + 1) * win, hi > w * win))
        def _():
            slot = (lax.broadcasted_iota(I32, (tt, win), 1) + w * win).astype(F32)
            onehot = jnp.where(slot == pcol, 1.0, 0.0).astype(BF16)
            acc[...] += gcol * jnp.dot(onehot, y_ref[0, 0, w * win:(w + 1) * win, :], preferred_element_type=F32)

    @pl.when(e == pl.num_programs(2) - 1)
    def _():
        v = DEEPNORM_ALPHA * x_ref[...] + g_ref[0] * acc[...]
        o_ref[...] = _layer_norm_rows(v, lg_ref[...], lb_ref[...])


def moe_combine(los, pos_tm, aff_tm, yd, x, g, lg, lb, B, T, cap, ts):
    E = pos_tm.shape[-1]
    M, D = x.shape
    nt = T // ts
    return pl.pallas_call(
        functools.partial(_combine_kernel, cap=cap, nt=nt),
        grid_spec=pltpu.PrefetchScalarGridSpec(
            num_scalar_prefetch=1,
            grid=(B, nt, E),
            in_specs=[pl.BlockSpec((1, ts, E), lambda b, i, e, s: (b, i, 0)),
                      pl.BlockSpec((1, ts, E), lambda b, i, e, s: (b, i, 0)),
                      pl.BlockSpec((1, 1, cap, D), lambda b, i, e, s: (b, e, 0, 0)),
                      pl.BlockSpec((ts, D), lambda b, i, e, s: (b * nt + i, 0)),
                      pl.BlockSpec((1, 1, D), lambda b, i, e, s: (b, 0, 0)),
                      pl.BlockSpec((1, D), lambda b, i, e, s: (0, 0)),
                      pl.BlockSpec((1, D), lambda b, i, e, s: (0, 0))],
            out_specs=pl.BlockSpec((ts, D), lambda b, i, e, s: (b * nt + i, 0)),
            scratch_shapes=[pltpu.VMEM((ts, D), F32)]),
        out_shape=jax.ShapeDtypeStruct((M, D), F32),
        compiler_params=_params(("parallel", "parallel", "arbitrary")),
        name="moe_combine",
    )(los, pos_tm, aff_tm, yd, x, g, lg, lb)


def moe_layer(x, sc, sh, g, lg, lb, w_router, w_gate, w_up, w_down, layer, B, T):
    cap = CAPACITY_FACTOR * T // N_EXPERTS
    ts = min(512, T)
    nt = T // ts
    h, aff_t = moe_router_op(x, sc, sh, w_router.T, B, T)
    pos, los = moe_topk(aff_t, cap, ts)
    los = los[:, :, :nt + 1].reshape(-1)
    xs = moe_gather(los, pos[:, :, None, :], h, B, T, cap, ts)
    yd = moe_ffn(xs, w_gate, w_up, w_down, layer)
    pos_tm = jnp.swapaxes(pos, 1, 2)
    aff_tm = jnp.swapaxes(aff_t, 1, 2)
    return moe_combine(los, pos_tm, aff_tm, yd, x, g, lg, lb, B, T, cap, ts)


def _pad_cols(w, n):
    return jnp.pad(w, ((0, 0), (0, n - w.shape[1])))


def _rot_cols(w):
    half = w.shape[-1] // 2
    return jnp.concatenate([-w[..., half:], w[..., :half]], axis=-1)


def _even_w_in(w):
    D = w.shape[0]
    o = MLA_IN
    rkv = w[:, o:o + 3 * RWKV_WIDTH]
    cq = w[:, :Q_LORA]
    ckv = w[:, Q_LORA:Q_LORA + KV_LORA]
    kr = w[:, Q_LORA + KV_LORA:MLA_IN]
    l0 = o + 3 * RWKV_WIDTH
    lora = [_pad_cols(w[:, l0 + i * DECAY_LORA:l0 + (i + 1) * DECAY_LORA], LORA_PAD) for i in range(4)]
    gd = w[:, l0 + 4 * DECAY_LORA:]
    pad = jnp.zeros((D, LANES), w.dtype)
    out = jnp.concatenate([rkv, cq, ckv, kr, _rot_cols(kr), pad] + lora + [gd], axis=1)
    assert out.shape[1] == EVEN_COLS
    return out.astype(BF16)


def _even_mu(mu):
    rkv = mu[:3 * RWKV_WIDTH]
    l0 = 3 * RWKV_WIDTH
    lora = [jnp.pad(mu[l0 + i * DECAY_LORA:l0 + (i + 1) * DECAY_LORA], (0, LORA_PAD - DECAY_LORA)) for i in range(4)]
    gd = mu[l0 + 4 * DECAY_LORA:]
    return rkv.reshape(1, -1), jnp.concatenate(lora + [gd]).reshape(1, -1)


def _pad_rows(w, n):
    return jnp.pad(w, ((0, 0), (0, n - w.shape[1]), (0, 0)))


def even_mixer(x, sc, sh, positions_tables, w_in, shift_mu, q_norm, w_uq, kv_norm, w_ukv, w0, w2, a0, a2, g2,
               k_k, k_a, r_k, gn_g, gn_b, B, T):
    cs, sn = positions_tables
    H = MLA_HEADS
    p = mm_mod(x, sc, sh, _even_w_in(w_in), T, 1024, 1536, F32)
    wq = w_uq.reshape(Q_LORA, H, NOPE_DIM + ROPE_DIM)
    wq = jnp.concatenate([wq, _rot_cols(wq[..., NOPE_DIM:])], axis=-1)
    wq = jnp.swapaxes(wq, 0, 1).astype(BF16)
    wkv = w_ukv.reshape(KV_LORA, H, NOPE_DIM + V_DIM)
    wk = jnp.swapaxes(wkv[..., :NOPE_DIM], 0, 1).astype(BF16)
    wvt = jnp.transpose(wkv[..., NOPE_DIM:], (1, 2, 0)).astype(BF16)
    q, k, vt = mla_proj(p, cs, sn, q_norm.reshape(1, -1), kv_norm.reshape(1, -1), wq, wk, wvt, B, T)
    attn = mla_attention(q, k, vt)
    mu_rkv, mu_lora = _even_mu(shift_mu)
    z_rkv = shift_mix(p, mu_rkv, T, 0, 3 * RWKV_WIDTH, 1024, False)
    la = shift_mix(p, mu_lora, T, 8448, 768, 768, True)
    y_rwkv = rwkv_scan(z_rkv.reshape(B, T, -1), la.reshape(B, T, -1),
                       _pad_rows(w2, LORA_PAD).astype(BF16), _pad_rows(a2, LORA_PAD).astype(BF16),
                       g2.astype(BF16), w0, a0, k_k.reshape(1, -1), k_a.reshape(1, -1), r_k.reshape(1, -1),
                       gn_g.reshape(1, -1), gn_b.reshape(1, -1), B, T)
    return attn.reshape(B * T, -1), y_rwkv.reshape(B * T, -1)


def odd_mixer(x, sc, sh, w_in, conv_w, a_log, dt_bias, norm_g, B, T):
    M = B * T
    nC = T // CHUNK
    wb = w_in.astype(BF16)
    n_main = GDN_QKV + GDN_VAL_WIDTH
    p = mm_mod(x, sc, sh, wb[:, :n_main], T, 1024, 1536, F32)
    ba = mm_mod(x, sc, sh, wb[:, n_main:], T, 512, 128, F32)
    qkv = gdn_conv_op(p, conv_w, T)
    nh = 2 * GDN_V_HEADS
    bg = gdn_gates(ba, jnp.pad(a_log.reshape(1, -1), ((0, 0), (nh, 0))),
                   jnp.pad(dt_bias.reshape(1, -1), ((0, 0), (nh, 0))), T)
    bg = bg.reshape(B, nC, CHUNK, 2, 2, GDN_K_HEADS, 2)
    bg = jnp.transpose(bg, (0, 1, 3, 4, 5, 6, 2)).reshape(B, nC, 4 * GDN_K_HEADS, PAIR)
    o = gdn_scan(qkv.reshape(B, T, -1), p.reshape(B, T, -1), GDN_QKV // (2 * GDN_D), bg, norm_g.reshape(1, -1), B, T)
    return o.reshape(M, -1)


def kernel(x, c, positions, ada_w, ada_b, ln_g, ln_b, e_w_in, e_shift_mu, mla_q_norm, mla_w_uq, mla_kv_norm,
           mla_w_ukv, rwkv_w0, rwkv_w2, rwkv_a0, rwkv_a2, rwkv_g2, rwkv_k_k, rwkv_k_a, rwkv_r_k, rwkv_gn_g,
           rwkv_gn_b, e_w_out, o_w_in, gdn_conv, gdn_a_log, gdn_dt_bias, gdn_norm, o_w_out, moe_router,
           moe_w_gate, moe_w_up, moe_w_down):
    B, T, D = x.shape
    M = B * T
    depth = ada_w.shape[0]
    mod = ada_modulation(c, ada_w, ada_b)
    tables = rope_tables(positions)
    xf = x.reshape(M, D)
    n_exp = moe_w_gate.shape[1]
    wg_all = moe_w_gate.astype(BF16).reshape(depth * n_exp, *moe_w_gate.shape[2:])
    wu_all = moe_w_up.astype(BF16).reshape(depth * n_exp, *moe_w_up.shape[2:])
    wd_all = moe_w_down.astype(BF16).reshape(depth * n_exp, *moe_w_down.shape[2:])
    for i in range(depth):
        m6 = mod[i].reshape(B, 6, 1, D)
        sh_m, sc_m, g_m, sh_f, sc_f, g_f = (m6[:, n] for n in range(6))
        j = i // 2
        lg0, lb0 = ln_g[i, 0].reshape(1, D), ln_b[i, 0].reshape(1, D)
        lg1, lb1 = ln_g[i, 1].reshape(1, D), ln_b[i, 1].reshape(1, D)
        if i % 2 == 0:
            a0, a1 = even_mixer(xf, sc_m, sh_m, tables, e_w_in[j], e_shift_mu[j], mla_q_norm[j], mla_w_uq[j],
                                mla_kv_norm[j], mla_w_ukv[j], rwkv_w0[j], rwkv_w2[j], rwkv_a0[j], rwkv_a2[j],
                                rwkv_g2[j], rwkv_k_k[j], rwkv_k_a[j], rwkv_r_k[j], rwkv_gn_g[j], rwkv_gn_b[j],
                                B, T)
            xf = mm_ln(a0, 0, a1, 0, e_w_out[j].astype(BF16), xf, g_m, lg0, lb0, T, 512)
        else:
            o = odd_mixer(xf, sc_m, sh_m, o_w_in[j], gdn_conv[j], gdn_a_log[j], gdn_dt_bias[j], gdn_norm[j], B, T)
            xf = mm_ln(o, 0, o, 1, o_w_out[j].astype(BF16), xf, g_m, lg0, lb0, T, 512)
        xf = moe_layer(xf, sc_f, sh_f, g_f, lg1, lb1, moe_router[i], wg_all, wu_all, wd_all, i, B, T)
    return xf.reshape(B, T, D)
```
